```python
import jax, jax.numpy as jnp
from jax import lax
import numpy as np

D_MODEL = 1024
BATCH = 4
SEQ = 4096
DEPTH = 1
DEC_BATCH = 128
DEC_SEQ = 1
PAST_LEN = 8192
PAGE_SIZE = 128

HEAD_DIM = 64
N_RET_HEADS = 8
N_ATT_HEADS = 8
RET_WIDTH = N_RET_HEADS * HEAD_DIM
ATT_WIDTH = N_ATT_HEADS * HEAD_DIM
MIX_WIDTH = RET_WIDTH + ATT_WIDTH
IN_PROJ_WIDTH = 4 * RET_WIDTH + 3 * ATT_WIDTH
RET_CHUNK = 128
DILATED_PATTERNS = ((128, 1), (512, 4), (2048, 16))
MAX_WINDOW = 2048
ROPE_THETA = 10000.0
N_EXPERTS = 256
TOP_K = 8
N_EXPERT_GROUPS = 8
TOPK_GROUPS = 4
EXPERT_DIM = 256
SHARED_DIM = 256
ROUTED_SCALE = 2.5
EXPERT_BLOCK_LARGE = 128
EXPERT_BLOCK_SMALL = 16
NORM_EPS = 1e-6
GN_EPS = 1e-5
NEG_INF = -1e30

kernel_name = 'hymba_retnet_longnet_moe_step'


def rmsnorm(x, g):
    xf = x.astype(jnp.float32)
    y = xf * lax.rsqrt(jnp.mean(xf * xf, axis=-1, keepdims=True) + NORM_EPS)
    return y.astype(x.dtype) * g


def head_groupnorm(o, g):
    B, L, H, D = o.shape
    of = o.astype(jnp.float32)
    mu = jnp.mean(of, axis=-1, keepdims=True)
    var = jnp.mean(jnp.square(of - mu), axis=-1, keepdims=True)
    y = ((of - mu) * lax.rsqrt(var + GN_EPS)).astype(o.dtype)
    return y.reshape(B, L, H * D) * g


def rotary(x, pos):
    half = x.shape[-1] // 2
    inv = ROPE_THETA ** (-jnp.arange(half, dtype=jnp.float32) / half)
    ang = pos.astype(jnp.float32)[:, None] * inv[None, :]
    cos = jnp.cos(ang)[None, :, None, :]
    sin = jnp.sin(ang)[None, :, None, :]
    xf = x.astype(jnp.float32)
    x1, x2 = xf[..., :half], xf[..., half:]
    return jnp.concatenate([x1 * cos - x2 * sin, x1 * sin + x2 * cos], axis=-1).astype(x.dtype)


def modulation(c, w_ada, b_ada):
    m = jax.nn.silu(c) @ w_ada + b_ada
    return [t[:, None, :] for t in jnp.split(m, 6, axis=-1)]


def retention(q, k, v, S0):
    B, L, H, D = q.shape
    C = min(RET_CHUNK, L)
    n = L // C
    log_g = jnp.log1p(-(2.0 ** (-5.0 - jnp.arange(H, dtype=jnp.float32))))
    i = jnp.arange(C)
    diff = i[:, None] - i[None, :]
    dmask = jnp.where(diff[None] >= 0, jnp.exp(log_g[:, None, None] * jnp.maximum(diff, 0)[None]), 0.0).astype(q.dtype)
    q_decay = jnp.exp(log_g[None, :] * (i[:, None] + 1)).astype(q.dtype)
    k_decay = jnp.exp(log_g[None, :] * (C - 1 - i)[:, None]).astype(q.dtype)
    s_decay = jnp.exp(log_g * C).astype(q.dtype)

    def step(S, qkv):
        qc, kc, vc = qkv
        sc = jnp.einsum('bihd,bjhd->bhij', qc, kc) * dmask[None]
        o = jnp.einsum('bhij,bjhe->bihe', sc, vc) + jnp.einsum('bihd,bhde->bihe', qc, S) * q_decay[None, :, :, None]
        S = S * s_decay[None, :, None, None] + jnp.einsum('bjhd,bjhe->bhde', kc * k_decay[None, :, :, None], vc)
        return S, o

    def chunks(t):
        return t.reshape(B, n, C, H, D).transpose(1, 0, 2, 3, 4)

    S, o = lax.scan(step, S0, (chunks(q), chunks(k), chunks(v)))
    return o.transpose(1, 0, 2, 3, 4).reshape(B, L, H, D), S


def softmax_stats(s, valid_mask):
    s = jnp.where(valid_mask, s, NEG_INF)
    m = jnp.max(s, axis=-1, keepdims=True)
    p = jnp.exp(s - m)
    l = jnp.sum(p, axis=-1, keepdims=True)
    return p / l, (m + jnp.log(l))[..., 0]


def strided_band_attention(q, k, v, window, dilation):
    B, L, H, D = q.shape
    d = dilation
    band = window // d
    Ls = L // d

    def to_sub(t):
        return t.reshape(B, Ls, d, H, D).transpose(0, 2, 1, 3, 4).reshape(B * d, Ls, H, D)

    nb = -(-Ls // band)
    Lp = nb * band
    padw = ((0, 0), (0, Lp - Ls), (0, 0), (0, 0))
    qs, ks, vs = [jnp.pad(to_sub(t), padw) for t in (q, k, v)]
    qb = qs.reshape(B * d, nb, band, H, D)

    def with_prev(t):
        tb = t.reshape(B * d, nb, band, H, D)
        prev = jnp.pad(tb[:, :-1], ((0, 0), (1, 0), (0, 0), (0, 0), (0, 0)))
        return jnp.concatenate([prev, tb], axis=2)

    kk, vv = with_prev(ks), with_prev(vs)
    s = jnp.einsum('bnqhd,bnkhd->bnhqk', qb, kk).astype(jnp.float32) * (HEAD_DIM ** -0.5)
    qi = jnp.arange(band)
    ki = jnp.arange(2 * band)
    blk = jnp.arange(nb)
    dist = qi[:, None] + band - ki[None, :]
    in_band = (dist >= 0) & (dist <= band)
    in_seq = (blk[:, None] * band + ki[None, :] - band) >= 0
    mask = in_band[None, None] & in_seq[:, None, None, :]
    p, lse = softmax_stats(s, mask[None])
    o = jnp.einsum('bnhqk,bnkhd->bnqhd', p.astype(v.dtype), vv)
    o = o.reshape(B * d, Lp, H, D)[:, :Ls]
    lse = lse.transpose(0, 1, 3, 2).reshape(B * d, Lp, H)[:, :Ls]
    o = o.reshape(B, d, Ls, H, D).transpose(0, 2, 1, 3, 4).reshape(B, L, H, D)
    lse = lse.reshape(B, d, Ls, H).transpose(0, 2, 1, 3).reshape(B, L, H)
    return o, lse


def combine_patterns(outs, lses):
    wts = jax.nn.softmax(jnp.stack(lses, axis=0), axis=0)
    return jnp.einsum('pblh,pblhd->blhd', wts.astype(outs[0].dtype), jnp.stack(outs, axis=0))


def dilated_prompt(q, k, v):
    outs, lses = [], []
    for window, d in DILATED_PATTERNS:
        o, lse = strided_band_attention(q, k, v, window, d)
        outs.append(o)
        lses.append(lse)
    return combine_patterns(outs, lses)


def dilated_sample(q, k_new, v_new, k_buf, v_buf):
    W = k_buf.shape[1]
    L = q.shape[1]
    kk = jnp.concatenate([k_buf, k_new], axis=1)
    vv = jnp.concatenate([v_buf, v_new], axis=1)
    outs, lses = [], []
    for window, d in DILATED_PATTERNS:
        j = jnp.arange(window // d + 1)
        idx = W + jnp.arange(L)[:, None] - d * j[None, :]
        valid = idx >= 0
        idx = jnp.maximum(idx, 0)
        kg, vg = kk[:, idx], vv[:, idx]
        s = jnp.einsum('blhd,bljhd->blhj', q, kg).astype(jnp.float32) * (HEAD_DIM ** -0.5)
        p, lse = softmax_stats(s, valid[None, :, None, :])
        outs.append(jnp.einsum('blhj,bljhd->blhd', p.astype(v_new.dtype), vg))
        lses.append(lse)
    keep = min(MAX_WINDOW, W + L)
    return combine_patterns(outs, lses), kk[:, W + L - keep:], vv[:, W + L - keep:]


def token_mixer(h, pos, S0, k_buf, v_buf, w_in, gn_ret, q_norm, k_norm, w_out):
    B, L, _ = h.shape
    proj = h @ w_in
    R, A = RET_WIDTH, ATT_WIDTH
    qr, kr, vr, gr, qa, ka, va = jnp.split(proj, [R, 2 * R, 3 * R, 4 * R, 4 * R + A, 4 * R + 2 * A], axis=-1)

    def heads(t, n):
        return t.reshape(B, L, n, HEAD_DIM)

    qr = rotary(heads(qr, N_RET_HEADS), pos)
    kr = rotary(heads(kr, N_RET_HEADS), pos) * (HEAD_DIM ** -0.5)
    if S0 is None:
        S0 = jnp.zeros((B, N_RET_HEADS, HEAD_DIM, HEAD_DIM), h.dtype)
    o_r, S_new = retention(qr, kr, heads(vr, N_RET_HEADS), S0)
    o_r = head_groupnorm(o_r, gn_ret) * jax.nn.silu(gr)

    qa = rotary(rmsnorm(heads(qa, N_ATT_HEADS), q_norm), pos)
    ka = rotary(rmsnorm(heads(ka, N_ATT_HEADS), k_norm), pos)
    va = heads(va, N_ATT_HEADS)
    if k_buf is None:
        o_a = dilated_prompt(qa, ka, va)
        keep = min(MAX_WINDOW, L)
        new_k, new_v = ka[:, L - keep:], va[:, L - keep:]
    else:
        o_a, new_k, new_v = dilated_sample(qa, ka, va, k_buf, v_buf)

    y = jnp.concatenate([o_r, o_a.reshape(B, L, ATT_WIDTH)], axis=-1) @ w_out
    return y, S_new, new_k, new_v


def route(x2d, w_router, router_bias):
    N = x2d.shape[0]
    s = jax.nn.sigmoid((x2d @ w_router).astype(jnp.float32))
    choice = s + router_bias.astype(jnp.float32)
    grp = choice.reshape(N, N_EXPERT_GROUPS, N_EXPERTS // N_EXPERT_GROUPS)
    gscore = jnp.sum(lax.top_k(grp, 2)[0], axis=-1)
    _, gidx = lax.top_k(gscore, TOPK_GROUPS)
    gmask = jnp.sum(jax.nn.one_hot(gidx, N_EXPERT_GROUPS, dtype=jnp.float32), axis=1) > 0
    emask = jnp.repeat(gmask, N_EXPERTS // N_EXPERT_GROUPS, axis=1)
    _, eidx = lax.top_k(jnp.where(emask, choice, -jnp.inf), TOP_K)
    w = jnp.take_along_axis(s, eidx, axis=1)
    w = w / jnp.sum(w, axis=-1, keepdims=True) * ROUTED_SCALE
    return eidx, w.astype(x2d.dtype)


def routed_experts(x2d, eidx, wts, w_exp_gate, w_exp_up, w_exp_down):
    N, D = x2d.shape
    A = N * TOP_K
    block = EXPERT_BLOCK_LARGE if A >= 2 * N_EXPERTS * EXPERT_BLOCK_LARGE else EXPERT_BLOCK_SMALL
    e = eidx.reshape(-1).astype(jnp.int32)
    tok = jnp.repeat(jnp.arange(N, dtype=jnp.int32), TOP_K)
    w = wts.reshape(-1)
    order = jnp.argsort(e)
    se, st, sw = e[order], tok[order], w[order]
    counts = jnp.bincount(e, length=N_EXPERTS)
    padded = (counts + block - 1) // block * block
    pad_end = jnp.cumsum(padded)
    pad_off = pad_end - padded
    start = jnp.cumsum(counts) - counts
    dest = pad_off[se] + jnp.arange(A) - start[se]
    P = (A + N_EXPERTS * block + block - 1) // block * block
    n_blocks = P // block
    tok_buf = jnp.full((P,), N, jnp.int32).at[dest].set(st)
    w_buf = jnp.zeros((P,), x2d.dtype).at[dest].set(sw)
    blk_exp = jnp.minimum(jnp.searchsorted(pad_end, jnp.arange(n_blocks) * block, side='right'), N_EXPERTS - 1)
    x_pad = jnp.concatenate([x2d, jnp.zeros((1, D), x2d.dtype)], axis=0)

    def expert_block(args):
        tb, eb, wb = args
        xb = x_pad[tb]
        hb = jax.nn.silu(xb @ w_exp_gate[eb]) * (xb @ w_exp_up[eb])
        return (hb @ w_exp_down[eb]) * wb[:, None]

    out = lax.map(expert_block, (tok_buf.reshape(n_blocks, block), blk_exp, w_buf.reshape(n_blocks, block)))
    return jax.ops.segment_sum(out.reshape(P, D), tok_buf, num_segments=N + 1)[:N]


def moe(h, w_router, router_bias, w_exp_gate, w_exp_up, w_exp_down, w_sh_gate, w_sh_up, w_sh_down):
    B, L, D = h.shape
    x2d = h.reshape(B * L, D)
    eidx, wts = route(x2d, w_router, router_bias)
    routed = routed_experts(x2d, eidx, wts, w_exp_gate, w_exp_up, w_exp_down)
    shared = (jax.nn.silu(x2d @ w_sh_gate) * (x2d @ w_sh_up)) @ w_sh_down
    return (routed + shared).reshape(B, L, D)


def layer(x, c, pos, S0, k_buf, v_buf, norm1, w_ada, b_ada, w_in, gn_ret, q_norm, k_norm, w_out,
          norm2, w_router, router_bias, w_exp_gate, w_exp_up, w_exp_down, w_sh_gate, w_sh_up, w_sh_down):
    sh1, sc1, g1, sh2, sc2, g2 = modulation(c, w_ada, b_ada)
    h = rmsnorm(x, norm1) * (1.0 + sc1) + sh1
    mix, S_new, k_new, v_new = token_mixer(h, pos, S0, k_buf, v_buf, w_in, gn_ret, q_norm, k_norm, w_out)
    x = x + g1 * mix
    h2 = rmsnorm(x, norm2) * (1.0 + sc2) + sh2
    x = x + g2 * moe(h2, w_router, router_bias, w_exp_gate, w_exp_up, w_exp_down, w_sh_gate, w_sh_up, w_sh_down)
    return x, S_new, k_new, v_new


def setup_inputs(seed: int = 0) -> dict:
    key = jax.random.key(seed)
    ks = jax.random.split(key, 32)
    f32 = jnp.float32

    def nrm(k, shape, scale):
        return jax.random.normal(k, shape, f32) * scale

    win = min(MAX_WINDOW, PAST_LEN)
    D = D_MODEL
    return {
        'x_prompt': nrm(ks[0], (BATCH, SEQ, D), 1.0),
        'x_sample': nrm(ks[1], (DEC_BATCH, DEC_SEQ, D), 1.0),
        'c_prompt': nrm(ks[2], (BATCH, D), 1.0),
        'c_sample': nrm(ks[3], (DEC_BATCH, D), 1.0),
        'state_ret': nrm(ks[4], (DEPTH, DEC_BATCH, N_RET_HEADS, HEAD_DIM, HEAD_DIM), 0.1),
        'cache_win_k': nrm(ks[5], (DEPTH, DEC_BATCH, win, N_ATT_HEADS, HEAD_DIM), 1.0),
        'cache_win_v': nrm(ks[6], (DEPTH, DEC_BATCH, win, N_ATT_HEADS, HEAD_DIM), 1.0),
        'norm1': 1.0 + nrm(ks[7], (DEPTH, D), 0.1),
        'w_ada': nrm(ks[8], (DEPTH, D, 6 * D), 0.5 * D ** -0.5),
        'b_ada': nrm(ks[9], (DEPTH, 6 * D), 0.02),
        'w_in': nrm(ks[10], (DEPTH, D, IN_PROJ_WIDTH), D ** -0.5),
        'gn_ret': 1.0 + nrm(ks[11], (DEPTH, RET_WIDTH), 0.1),
        'q_norm': 1.0 + nrm(ks[12], (DEPTH, HEAD_DIM), 0.1),
        'k_norm': 1.0 + nrm(ks[13], (DEPTH, HEAD_DIM), 0.1),
        'w_out': nrm(ks[14], (DEPTH, MIX_WIDTH, D), MIX_WIDTH ** -0.5),
        'norm2': 1.0 + nrm(ks[15], (DEPTH, D), 0.1),
        'w_router': nrm(ks[16], (DEPTH, D, N_EXPERTS), D ** -0.5),
        'router_bias': nrm(ks[17], (DEPTH, N_EXPERTS), 0.01),
        'w_exp_gate': nrm(ks[18], (DEPTH, N_EXPERTS, D, EXPERT_DIM), D ** -0.5),
        'w_exp_up': nrm(ks[19], (DEPTH, N_EXPERTS, D, EXPERT_DIM), D ** -0.5),
        'w_exp_down': nrm(ks[20], (DEPTH, N_EXPERTS, EXPERT_DIM, D), EXPERT_DIM ** -0.5),
        'w_sh_gate': nrm(ks[21], (DEPTH, D, SHARED_DIM), D ** -0.5),
        'w_sh_up': nrm(ks[22], (DEPTH, D, SHARED_DIM), D ** -0.5),
        'w_sh_down': nrm(ks[23], (DEPTH, SHARED_DIM, D), SHARED_DIM ** -0.5),
    }


def reference(x_prompt, x_sample, c_prompt, c_sample, state_ret, cache_win_k, cache_win_v,
              norm1, w_ada, b_ada, w_in, gn_ret, q_norm, k_norm, w_out, norm2,
              w_router, router_bias, w_exp_gate, w_exp_up, w_exp_down, w_sh_gate, w_sh_up, w_sh_down):
    yp, ys = x_prompt, x_sample
    pos_p = jnp.arange(x_prompt.shape[1], dtype=jnp.int32)
    pos_s = PAST_LEN + jnp.arange(x_sample.shape[1], dtype=jnp.int32)
    sp, kp, vp, ss, ksm, vsm = [], [], [], [], [], []
    for l in range(DEPTH):
        w = (norm1[l], w_ada[l], b_ada[l], w_in[l], gn_ret[l], q_norm[l], k_norm[l], w_out[l], norm2[l],
             w_router[l], router_bias[l], w_exp_gate[l], w_exp_up[l], w_exp_down[l],
             w_sh_gate[l], w_sh_up[l], w_sh_down[l])
        yp, s1, k1, v1 = layer(yp, c_prompt, pos_p, None, None, None, *w)
        ys, s2, k2, v2 = layer(ys, c_sample, pos_s, state_ret[l], cache_win_k[l], cache_win_v[l], *w)
        sp.append(s1)
        kp.append(k1)
        vp.append(v1)
        ss.append(s2)
        ksm.append(k2)
        vsm.append(v2)
    return (yp, ys, jnp.stack(sp), jnp.stack(kp), jnp.stack(vp), jnp.stack(ss), jnp.stack(ksm), jnp.stack(vsm))
```

```python
import functools

import jax
import jax.numpy as jnp
from jax import lax
from jax.experimental import pallas as pl
from jax.experimental.pallas import tpu as pltpu

HEAD_DIM = 64
N_RET_HEADS = 8
N_ATT_HEADS = 8
RET_WIDTH = N_RET_HEADS * HEAD_DIM
ATT_WIDTH = N_ATT_HEADS * HEAD_DIM
RET_CHUNK = 128
DILATED_PATTERNS = ((128, 1), (512, 4), (2048, 16))
MAX_WINDOW = 2048
PAST_LEN = 8192
ROPE_THETA = 10000.0
N_EXPERTS = 256
TOP_K = 8
N_EXPERT_GROUPS = 8
TOPK_GROUPS = 4
ROUTED_SCALE = 2.5
NORM_EPS = 1e-6
GN_EPS = 1e-5
NEG_INF = -1e30

F32 = jnp.float32
BF16 = jnp.bfloat16

VMEM_LIMIT_BYTES = 56 * 1024 * 1024
EXPERT_TILE = 256
TOKEN_TILE = 512


def _cparams(*sem):
    return pltpu.CompilerParams(dimension_semantics=sem, vmem_limit_bytes=VMEM_LIMIT_BYTES)


def _dot(a, b):
    return jnp.dot(a, b, preferred_element_type=F32)


def _dot_nt(a, b):
    return lax.dot_general(a, b, (((1,), (1,)), ((), ())), preferred_element_type=F32)


def _dot_tn(a, b):
    return lax.dot_general(a, b, (((0,), (0,)), ((), ())), preferred_element_type=F32)


def _mod_kernel(c_ref, w_ref, b_ref, o_ref):
    c = c_ref[...]
    a = (c * jax.nn.sigmoid(c)).astype(BF16)
    o_ref[...] = _dot(a, w_ref[...].astype(BF16)) + b_ref[...]


def _modulation(c, w_ada, b_ada):
    R, D = c.shape
    n_out = w_ada.shape[1]
    tn = 1024
    return pl.pallas_call(
        _mod_kernel,
        grid=(n_out // tn,),
        in_specs=[
            pl.BlockSpec((R, D), lambda j: (0, 0)),
            pl.BlockSpec((D, tn), lambda j: (0, j)),
            pl.BlockSpec((1, tn), lambda j: (0, j)),
        ],
        out_specs=pl.BlockSpec((R, tn), lambda j: (0, j)),
        out_shape=jax.ShapeDtypeStruct((R, n_out), F32),
        compiler_params=_cparams("parallel"),
        name="modulation",
    )(c, w_ada, b_ada.reshape(1, n_out))


def _swap_halves(x, first_half):
    n = x.shape[-1]
    return jnp.where(first_half, pltpu.roll(x, n - HEAD_DIM // 2, 1), pltpu.roll(x, HEAD_DIM // 2, 1))


def _head_mean(x, g):
    hi = x.astype(BF16)
    lo = (x - hi.astype(F32)).astype(BF16)
    return _dot(hi, g) + _dot(lo, g)


def _in_proj_kernel(x_ref, sc_ref, sh_ref, cos_ref, sin_ref, n1_ref, qn_ref, kn_ref, g_ref, w_ref,
                    qr_ref, kr_ref, vr_ref, sg_ref, qa_ref, ka_ref, va_ref):
    x = x_ref[...]
    ms = jnp.mean(x * x, axis=-1, keepdims=True)
    h = x * lax.rsqrt(ms + NORM_EPS) * n1_ref[...]
    h = (h * (1.0 + sc_ref[0]) + sh_ref[0]).astype(BF16)
    cos = cos_ref[...]
    sin = sin_ref[...]
    W = RET_WIDTH
    lane = lax.broadcasted_iota(jnp.int32, (1, W), 1)
    first_half = (lane % HEAD_DIM) < (HEAD_DIM // 2)

    def proj(c):
        return _dot(h, w_ref[:, c * W:(c + 1) * W])

    def rot(t):
        return t * cos + _swap_halves(t, first_half) * sin

    def qk_norm(t, gain):
        return t * lax.rsqrt(_head_mean(t * t, g_ref[...]) + NORM_EPS) * gain

    qr_ref[...] = rot(proj(0))
    kr_ref[...] = rot(proj(1)) * (HEAD_DIM ** -0.5)
    vr_ref[...] = proj(2)
    gr = proj(3)
    sg_ref[...] = gr * jax.nn.sigmoid(gr)
    qa_ref[...] = rot(qk_norm(proj(4), qn_ref[...]))
    ka_ref[...] = rot(qk_norm(proj(5), kn_ref[...]))
    va_ref[...] = proj(6)


def _in_proj(x, sc, sh, cos, sin, norm1, q_norm_t, k_norm_t, gmat, w_in_bf, tm, rows_per_mod, pos_blocks):
    N, D = x.shape
    W = RET_WIDTH
    R = sc.shape[1]
    mod_spec = pl.BlockSpec((1, R, D), lambda i: (i // rows_per_mod, 0, 0))
    pos_spec = pl.BlockSpec((cos.shape[0] // pos_blocks, W), lambda i: (i % pos_blocks, 0))
    const = lambda shape: pl.BlockSpec(shape, lambda i: (0,) * len(shape))
    out_spec = pl.BlockSpec((tm, W), lambda i: (i, 0))
    return pl.pallas_call(
        _in_proj_kernel,
        grid=(N // tm,),
        in_specs=[
            pl.BlockSpec((tm, D), lambda i: (i, 0)),
            mod_spec, mod_spec, pos_spec, pos_spec,
            const((1, D)), const((1, W)), const((1, W)), const((W, W)), const(w_in_bf.shape),
        ],
        out_specs=[out_spec] * 7,
        out_shape=[jax.ShapeDtypeStruct((N, W), F32)] * 7,
        compiler_params=_cparams("parallel"),
        name="in_proj",
    )(x, sc, sh, cos, sin, norm1, q_norm_t, k_norm_t, gmat, w_in_bf)


def _ret_kernel(q_ref, k_ref, v_ref, sg_ref, dm_ref, qd_ref, kd_ref, sd_ref, gn_ref, o_ref, s_ref):
    @pl.when(pl.program_id(1) == 0)
    def _():
        s_ref[...] = jnp.zeros_like(s_ref)

    q = q_ref[0]
    k = k_ref[0]
    v = v_ref[0]
    kd = k * kd_ref[...]
    qd = qd_ref[...]
    outs = []
    for h in range(N_RET_HEADS):
        sl = slice(HEAD_DIM * h, HEAD_DIM * (h + 1))
        qh = q[:, sl].astype(BF16)
        kh = k[:, sl].astype(BF16)
        vh = v[:, sl].astype(BF16)
        S = s_ref[0, h]
        sc = _dot_nt(qh, kh) * dm_ref[h]
        o = _dot(sc.astype(BF16), vh) + _dot(qh, S.astype(BF16)) * qd[:, sl]
        s_ref[0, h] = S * sd_ref[h] + _dot_tn(kd[:, sl].astype(BF16), vh)
        mu = jnp.mean(o, axis=-1, keepdims=True)
        oc = o - mu
        var = jnp.mean(oc * oc, axis=-1, keepdims=True)
        outs.append(oc * lax.rsqrt(var + GN_EPS))
    o_ref[0] = jnp.concatenate(outs, axis=1) * gn_ref[...] * sg_ref[0]


def _decay_tables(C):
    H = N_RET_HEADS
    log_g = jnp.log1p(-(2.0 ** (-5.0 - jnp.arange(H, dtype=F32))))
    i = jnp.arange(C)
    diff = i[:, None] - i[None, :]
    dmask = jnp.where(diff[None] >= 0, jnp.exp(log_g[:, None, None] * jnp.maximum(diff, 0)[None]), 0.0).astype(F32)
    q_decay = jnp.exp(log_g[None, :] * (i[:, None] + 1)).astype(F32)
    k_decay = jnp.exp(log_g[None, :] * (C - 1 - i)[:, None]).astype(F32)
    s_decay = jnp.exp(log_g * C).astype(F32)
    return dmask, q_decay, k_decay, s_decay


def _retention_prompt(q, k, v, sg, gn):
    B, L, W = q.shape
    C = RET_CHUNK
    H, Dh = N_RET_HEADS, HEAD_DIM
    dmask, q_decay, k_decay, s_decay = _decay_tables(C)
    qd = jnp.repeat(q_decay, Dh, axis=1)
    kd = jnp.repeat(k_decay, Dh, axis=1)
    sd = jnp.broadcast_to(s_decay[:, None, None], (H, Dh, Dh))
    blk = pl.BlockSpec((1, C, W), lambda b, c: (b, c, 0))
    const = lambda shape: pl.BlockSpec(shape, lambda b, c: (0,) * len(shape))
    return pl.pallas_call(
        _ret_kernel,
        grid=(B, L // C),
        in_specs=[blk, blk, blk, blk, const((H, C, C)), const((C, W)), const((C, W)), const((H, Dh, Dh)), const((1, W))],
        out_specs=[blk, pl.BlockSpec((1, H, Dh, Dh), lambda b, c: (b, 0, 0, 0))],
        out_shape=[jax.ShapeDtypeStruct((B, L, W), F32), jax.ShapeDtypeStruct((B, H, Dh, Dh), F32)],
        compiler_params=_cparams("parallel", "arbitrary"),
        name="retention_prompt",
    )(q, k, v, sg, dmask, qd, kd, sd, gn)


def _attn_kernel(q_ref, k_ref, v_ref, o_ref, oacc, lacc):
    L = q_ref.shape[1]
    P2 = q_ref.shape[2]
    lane = lax.broadcasted_iota(jnp.int32, (1, P2), 1)
    head0 = lane < HEAD_DIM
    scale = HEAD_DIM ** -0.5

    for p, (window, d) in enumerate(DILATED_PATTERNS):
        band = window // d
        nb = L // (d * band)
        qi = lax.broadcasted_iota(jnp.int32, (band, 2 * band), 0)
        ki = lax.broadcasted_iota(jnp.int32, (band, 2 * band), 1)
        dist = qi + band - ki
        in_band = (dist >= 0) & (dist <= band)

        def body(idx, carry, d=d, band=band, nb=nb, p=p, in_band=in_band, ki=ki):
            r = idx // nb
            n = idx % nb
            qs = r + d * band * n
            ps = jnp.maximum(qs - d * band, r)
            cur = pl.ds(qs, band, stride=d) if d > 1 else pl.ds(qs, band)
            prev = pl.ds(ps, band, stride=d) if d > 1 else pl.ds(ps, band)
            qb = q_ref[0, cur, :]
            kk = jnp.concatenate([k_ref[0, prev, :], k_ref[0, cur, :]], axis=0).astype(BF16)
            vv = jnp.concatenate([v_ref[0, prev, :], v_ref[0, cur, :]], axis=0).astype(BF16)
            mask = in_band & ((ki >= band) | (n > 0))
            res = []
            for hm in (head0, jnp.logical_not(head0)):
                qh = jnp.where(hm, qb, 0.0).astype(BF16)
                s = _dot_nt(qh, kk) * scale
                s = jnp.where(mask, s, NEG_INF)
                m = jnp.max(s, axis=-1, keepdims=True)
                e = jnp.exp(s - m)
                l = jnp.sum(e, axis=-1, keepdims=True)
                o = _dot((e / l).astype(BF16), vv)
                res.append((o, m + jnp.log(l)))
            oacc[p, cur, :] = jnp.where(head0, res[0][0], res[1][0])
            lacc[p, cur, :] = jnp.where(head0, res[0][1], res[1][1])
            return carry

        lax.fori_loop(0, d * nb, body, 0)

    rows = 512

    def combine(i, carry):
        sl = pl.ds(pl.multiple_of(i * rows, rows), rows)
        l0, l1, l2 = lacc[0, sl, :], lacc[1, sl, :], lacc[2, sl, :]
        m = jnp.maximum(jnp.maximum(l0, l1), l2)
        w0, w1, w2 = jnp.exp(l0 - m), jnp.exp(l1 - m), jnp.exp(l2 - m)
        tot = w0 + w1 + w2
        o_ref[0, sl, :] = (w0 * oacc[0, sl, :] + w1 * oacc[1, sl, :] + w2 * oacc[2, sl, :]) / tot
        return carry

    lax.fori_loop(0, L // rows, combine, 0)


def _attention_prompt(q, k, v):
    B, L, W = q.shape
    P2 = 2 * HEAD_DIM
    blk = pl.BlockSpec((1, L, P2), lambda b, hp: (b, 0, hp))
    return pl.pallas_call(
        _attn_kernel,
        grid=(B, W // P2),
        in_specs=[blk, blk, blk],
        out_specs=blk,
        out_shape=jax.ShapeDtypeStruct((B, L, W), F32),
        scratch_shapes=[pltpu.VMEM((len(DILATED_PATTERNS), L, P2), F32)] * 2,
        compiler_params=_cparams("parallel", "parallel"),
        name="attention_prompt",
    )(q, k, v)


def _post_kernel(or_ref, oa_ref, x_ref, g1_ref, sc_ref, sh_ref, g2_ref, n2_ref, wo_ref, wrh_ref, wrl_ref,
                 wsg_ref, wsu_ref, wsd_ref, base_ref, h2_ref, s_ref):
    W = RET_WIDTH
    mix = _dot(or_ref[...].astype(BF16), wo_ref[:W, :]) + _dot(oa_ref[...].astype(BF16), wo_ref[W:, :])
    x2 = x_ref[...] + g1_ref[0] * mix
    ms = jnp.mean(x2 * x2, axis=-1, keepdims=True)
    h2 = x2 * lax.rsqrt(ms + NORM_EPS) * n2_ref[...]
    h2 = h2 * (1.0 + sc_ref[0]) + sh_ref[0]
    hb = h2.astype(BF16)
    hl = (h2 - hb.astype(F32)).astype(BF16)
    logits = _dot(hb, wrh_ref[...]) + (_dot(hl, wrh_ref[...]) + _dot(hb, wrl_ref[...]))
    s_ref[...] = jax.nn.sigmoid(logits)
    h2_ref[...] = hb
    g = _dot(hb, wsg_ref[...])
    u = _dot(hb, wsu_ref[...])
    shared = _dot((g * jax.nn.sigmoid(g) * u).astype(BF16), wsd_ref[...])
    base_ref[...] = x2 + g2_ref[0] * shared


def _post_mixer(o_r, o_a, x, g1, sc2, sh2, g2, norm2, wo, wrh, wrl, wsg, wsu, wsd, tm, rows_per_mod):
    N, D = x.shape
    W = RET_WIDTH
    E = wrh.shape[1]
    R = g1.shape[1]
    mod_spec = pl.BlockSpec((1, R, D), lambda i: (i // rows_per_mod, 0, 0))
    const = lambda a: pl.BlockSpec(a.shape, lambda i: (0,) * a.ndim)
    return pl.pallas_call(
        _post_kernel,
        grid=(N // tm,),
        in_specs=[
            pl.BlockSpec((tm, W), lambda i: (i, 0)),
            pl.BlockSpec((tm, W), lambda i: (i, 0)),
            pl.BlockSpec((tm, D), lambda i: (i, 0)),
            mod_spec, mod_spec, mod_spec, mod_spec,
            const(norm2), const(wo), const(wrh), const(wrl), const(wsg), const(wsu), const(wsd),
        ],
        out_specs=[
            pl.BlockSpec((tm, D), lambda i: (i, 0)),
            pl.BlockSpec((tm, D), lambda i: (i, 0)),
            pl.BlockSpec((tm, E), lambda i: (i, 0)),
        ],
        out_shape=[
            jax.ShapeDtypeStruct((N, D), F32),
            jax.ShapeDtypeStruct((N, D), BF16),
            jax.ShapeDtypeStruct((N, E), F32),
        ],
        compiler_params=_cparams("parallel"),
        name="post_mixer",
    )(o_r, o_a, x, g1, sc2, sh2, g2, norm2, wo, wrh, wrl, wsg, wsu, wsd)


def _expert_kernel(be_ref, first_ref, nused_ref, x_ref, wrow_ref, wg_ref, wu_ref, wd_ref, y_ref, wg_s, wu_s, wd_s):
    i = pl.program_id(0)

    @pl.when(first_ref[i] == 1)
    def _():
        wg_s[...] = wg_ref[0].astype(BF16)
        wu_s[...] = wu_ref[0].astype(BF16)
        wd_s[...] = wd_ref[0].astype(BF16)

    @pl.when(i < nused_ref[0])
    def _():
        x = x_ref[...]
        g = _dot(x, wg_s[...])
        u = _dot(x, wu_s[...])
        hmid = (g * jax.nn.sigmoid(g) * u).astype(BF16)
        y_ref[...] = _dot(hmid, wd_s[...]) * wrow_ref[...]

    @pl.when(i >= nused_ref[0])
    def _():
        y_ref[...] = jnp.zeros_like(y_ref)


def _expert_matmul(xs, wrow, blk_exp, first, n_used, w_gate, w_up, w_down):
    P, D = xs.shape
    T = EXPERT_TILE
    F = w_gate.shape[2]
    return pl.pallas_call(
        _expert_kernel,
        grid_spec=pltpu.PrefetchScalarGridSpec(
            num_scalar_prefetch=3,
            grid=(P // T,),
            in_specs=[
                pl.BlockSpec((T, D), lambda i, be, fi, nu: (i, 0)),
                pl.BlockSpec((T, 1), lambda i, be, fi, nu: (i, 0)),
                pl.BlockSpec((1, D, F), lambda i, be, fi, nu: (be[i], 0, 0)),
                pl.BlockSpec((1, D, F), lambda i, be, fi, nu: (be[i], 0, 0)),
                pl.BlockSpec((1, F, D), lambda i, be, fi, nu: (be[i], 0, 0)),
            ],
            out_specs=pl.BlockSpec((T, D), lambda i, be, fi, nu: (i, 0)),
            scratch_shapes=[pltpu.VMEM((D, F), BF16), pltpu.VMEM((D, F), BF16), pltpu.VMEM((F, D), BF16)],
        ),
        out_shape=jax.ShapeDtypeStruct((P, D), F32),
        compiler_params=_cparams("arbitrary"),
        name="routed_experts",
    )(blk_exp, first, n_used, xs, wrow, w_gate, w_up, w_down)


def _route(s, router_bias):
    N, E = s.shape
    choice = s + router_bias.astype(F32)
    grp = choice.reshape(N, N_EXPERT_GROUPS, E // N_EXPERT_GROUPS)
    gscore = jnp.sum(lax.top_k(grp, 2)[0], axis=-1)
    _, gidx = lax.top_k(gscore, TOPK_GROUPS)
    gmask = jnp.sum(jax.nn.one_hot(gidx, N_EXPERT_GROUPS, dtype=F32), axis=1) > 0
    emask = jnp.repeat(gmask, E // N_EXPERT_GROUPS, axis=1)
    _, eidx = lax.top_k(jnp.where(emask, choice, -jnp.inf), TOP_K)
    w = jnp.take_along_axis(s, eidx, axis=1)
    w = w / jnp.sum(w, axis=-1, keepdims=True) * ROUTED_SCALE
    return eidx, w


def _moe_routed(h2, s, router_bias, w_gate, w_up, w_down):
    N, D = h2.shape
    E = N_EXPERTS
    T = EXPERT_TILE
    eidx, wts = _route(s, router_bias)
    A = N * TOP_K
    e = eidx.reshape(-1).astype(jnp.int32)
    tok = jnp.repeat(jnp.arange(N, dtype=jnp.int32), TOP_K)
    w = wts.reshape(-1)
    order = jnp.argsort(e)
    se, st, sw = e[order], tok[order], w[order]
    counts = jnp.bincount(e, length=E)
    padded = (counts + T - 1) // T * T
    pad_end = jnp.cumsum(padded)
    pad_off = pad_end - padded
    start = jnp.cumsum(counts) - counts
    dest = (pad_off[se] + jnp.arange(A) - start[se]).astype(jnp.int32)
    P = (A + E * T + T - 1) // T * T
    n_tiles = P // T
    tok_buf = jnp.full((P,), N, jnp.int32).at[dest].set(st)
    w_buf = jnp.zeros((P,), F32).at[dest].set(sw)
    tile_start = jnp.arange(n_tiles, dtype=jnp.int32) * T
    blk_exp = jnp.minimum(jnp.searchsorted(pad_end, tile_start, side='right'), E - 1).astype(jnp.int32)
    first = jnp.concatenate([jnp.ones((1,), jnp.int32), (blk_exp[1:] != blk_exp[:-1]).astype(jnp.int32)])
    n_used = (pad_end[-1] // T).astype(jnp.int32).reshape(1)
    x_pad = jnp.concatenate([h2, jnp.zeros((1, D), h2.dtype)], axis=0)
    xs = x_pad[tok_buf]
    ys = _expert_matmul(xs, w_buf.reshape(P, 1), blk_exp, first, n_used, w_gate, w_up, w_down)
    pos = jnp.zeros((A,), jnp.int32).at[order].set(dest).reshape(N, TOP_K)
    return jnp.sum(ys[pos], axis=1)


def _ret_sample_kernel(q_ref, k_ref, v_ref, sg_ref, gn_ref, dec_ref, s_ref, o_ref, so_ref):
    q = q_ref[0]
    k = k_ref[0]
    v = v_ref[0]
    gamma = dec_ref[pl.program_id(0)]
    qk = jnp.sum(q * k, axis=0, keepdims=True)
    o = qk * v
    Dh = q.shape[0]
    acc = jnp.zeros_like(v)
    for d in range(Dh):
        S = s_ref[0, d]
        acc = acc + q[d:d + 1, :] * S
        so_ref[0, d] = S * gamma + k[d:d + 1, :] * v
    o = o + acc * gamma
    mu = jnp.mean(o, axis=0, keepdims=True)
    oc = o - mu
    var = jnp.mean(oc * oc, axis=0, keepdims=True)
    o_ref[0] = oc * lax.rsqrt(var + GN_EPS) * gn_ref[0] * sg_ref[0]


def _retention_sample(qT, kT, vT, sgT, gn_col, gamma, S):
    H, Dh, Bt = qT.shape
    vec = pl.BlockSpec((1, Dh, Bt), lambda h: (h, 0, 0))
    st = pl.BlockSpec((1, Dh, Dh, Bt), lambda h: (h, 0, 0, 0))
    return pl.pallas_call(
        _ret_sample_kernel,
        grid=(H,),
        in_specs=[vec, vec, vec, vec, pl.BlockSpec((1, Dh, 1), lambda h: (h, 0, 0)),
                  pl.BlockSpec(memory_space=pltpu.SMEM), st],
        out_specs=[vec, st],
        out_shape=[jax.ShapeDtypeStruct((H, Dh, Bt), F32), jax.ShapeDtypeStruct((H, Dh, Dh, Bt), F32)],
        compiler_params=_cparams("parallel"),
        name="retention_sample",
    )(qT, kT, vT, sgT, gn_col, gamma, S)


def _attn_sample_kernel(q_ref, kn_ref, vn_ref, knT_ref, vnT_ref, kb_ref, vb_ref, o_ref, ko_ref, vo_ref):
    b = pl.program_id(0)
    H, Dh, Wn = kb_ref.shape[1], kb_ref.shape[2], kb_ref.shape[3]
    HD = H * Dh
    scale = HEAD_DIM ** -0.5
    q = q_ref[0]
    kn = kn_ref[0]
    vn = vn_ref[0]
    row = lax.broadcasted_iota(jnp.int32, (H, HD), 0)
    col = lax.broadcasted_iota(jnp.int32, (H, HD), 1)
    own = (col // Dh) == row
    q_bd = jnp.where(own, q, 0.0)
    s_new = jnp.sum(q_bd * kn, axis=1, keepdims=True) * scale

    bl = lax.broadcasted_iota(jnp.int32, knT_ref.shape, 1)
    k_col = jnp.sum(jnp.where(bl == b, knT_ref[...], 0.0), axis=1, keepdims=True)
    v_col = jnp.sum(jnp.where(bl == b, vnT_ref[...], 0.0), axis=1, keepdims=True)

    t = lax.broadcasted_iota(jnp.int32, (1, Wn), 1)
    last = t == (Wn - 1)
    q_bf = q_bd.astype(BF16)
    outs = []
    for h in range(H):
        K = kb_ref[0, h]
        V = vb_ref[0, h]
        hs = slice(h * Dh, (h + 1) * Dh)
        ko_ref[0, h] = jnp.where(last, k_col[hs], pltpu.roll(K, Wn - 1, 1))
        vo_ref[0, h] = jnp.where(last, v_col[hs], pltpu.roll(V, Wn - 1, 1))
        s = _dot(q_bf[:, hs], K.astype(BF16))[h:h + 1] * scale
        sn = s_new[h:h + 1]
        Vb = V.astype(BF16)
        vn_h = vn[:, hs]
        o_p, lse_p = [], []
        for window, d in DILATED_PATTERNS:
            valid = ((t % d) == 0) & (t >= Wn - window)
            sm = jnp.where(valid, s, NEG_INF)
            m = jnp.maximum(jnp.max(sm, axis=1, keepdims=True), sn)
            e = jnp.exp(sm - m)
            en = jnp.exp(sn - m)
            l = jnp.sum(e, axis=1, keepdims=True) + en
            pe = jnp.broadcast_to((e / l).astype(BF16), (8, Wn))
            o = _dot_nt(pe, Vb)[0:1] + (en / l).astype(BF16).astype(F32) * vn_h.astype(BF16).astype(F32)
            o_p.append(o)
            lse_p.append(m + jnp.log(l))
        mm = jnp.maximum(jnp.maximum(lse_p[0], lse_p[1]), lse_p[2])
        ws = [jnp.exp(x - mm) for x in lse_p]
        tot = ws[0] + ws[1] + ws[2]
        outs.append((ws[0] * o_p[0] + ws[1] * o_p[1] + ws[2] * o_p[2]) / tot)
    o_ref[0] = jnp.concatenate(outs, axis=1)


def _attention_sample(q, kn, vn, knT, vnT, k_buf, v_buf):
    Bt, H, Dh, Wn = k_buf.shape
    HD = H * Dh
    vec = pl.BlockSpec((1, 1, HD), lambda b: (b, 0, 0))
    tr = pl.BlockSpec((HD, Bt), lambda b: (0, 0))
    cache = pl.BlockSpec((1, H, Dh, Wn), lambda b: (b, 0, 0, 0))
    return pl.pallas_call(
        _attn_sample_kernel,
        grid=(Bt,),
        in_specs=[vec, vec, vec, tr, tr, cache, cache],
        out_specs=[vec, cache, cache],
        out_shape=[jax.ShapeDtypeStruct((Bt, 1, HD), F32),
                   jax.ShapeDtypeStruct(k_buf.shape, F32), jax.ShapeDtypeStruct(v_buf.shape, F32)],
        compiler_params=_cparams("parallel"),
        name="attention_sample",
    )(q, kn, vn, knT, vnT, k_buf, v_buf)


def _final_kernel(base_ref, routed_ref, g2_ref, o_ref):
    o_ref[...] = base_ref[...] + g2_ref[0] * routed_ref[...]


def _final(base, routed, g2, tm, rows_per_mod):
    N, D = base.shape
    R = g2.shape[1]
    blk = pl.BlockSpec((tm, D), lambda i: (i, 0))
    return pl.pallas_call(
        _final_kernel,
        grid=(N // tm,),
        in_specs=[blk, blk, pl.BlockSpec((1, R, D), lambda i: (i // rows_per_mod, 0, 0))],
        out_specs=blk,
        out_shape=jax.ShapeDtypeStruct((N, D), F32),
        compiler_params=_cparams("parallel"),
        name="final_combine",
    )(base, routed, g2)


def _rope_tables(pos):
    half = HEAD_DIM // 2
    inv = ROPE_THETA ** (-jnp.arange(half, dtype=F32) / half)
    ang = pos.astype(F32)[:, None] * inv[None, :]
    cos, sin = jnp.cos(ang), jnp.sin(ang)
    cos_t = jnp.tile(cos, (1, 2 * N_RET_HEADS))
    sin_t = jnp.tile(jnp.concatenate([-sin, sin], axis=1), (1, N_RET_HEADS))
    return cos_t, sin_t


def _layer(x_prompt, x_sample, c_prompt, c_sample, state_ret, cache_k, cache_v,
           norm1, w_ada, b_ada, w_in, gn_ret, q_norm, k_norm, w_out, norm2,
           w_router, router_bias, w_exp_gate, w_exp_up, w_exp_down, w_sh_gate, w_sh_up, w_sh_down):
    B, L, D = x_prompt.shape
    Bs = x_sample.shape[0]
    H, Dh, W = N_RET_HEADS, HEAD_DIM, RET_WIDTH
    Np = B * L

    pad = (-(B + Bs)) % 8
    c_all = jnp.concatenate([c_prompt, c_sample, jnp.zeros((pad, D), F32)], axis=0)
    mod = _modulation(c_all, w_ada, b_ada)
    mod_p = [m.reshape(B, 1, D) for m in jnp.split(mod[:B], 6, axis=-1)]
    mod_s = [m.reshape(1, Bs, D) for m in jnp.split(mod[B:B + Bs], 6, axis=-1)]

    w_in_bf = w_in.astype(BF16)
    gi = jnp.arange(W) // Dh
    gmat = jnp.where(gi[:, None] == gi[None, :], 1.0 / Dh, 0.0).astype(BF16)
    qn_t = jnp.tile(q_norm, N_ATT_HEADS).reshape(1, W)
    kn_t = jnp.tile(k_norm, N_ATT_HEADS).reshape(1, W)
    n1 = norm1.reshape(1, D)
    n2 = norm2.reshape(1, D)
    gn = gn_ret.reshape(1, W)

    tm = TOKEN_TILE
    cos_p, sin_p = _rope_tables(jnp.arange(L, dtype=jnp.int32))
    cos_s, sin_s = _rope_tables(PAST_LEN + jnp.arange(1, dtype=jnp.int32))

    xp = x_prompt.reshape(Np, D)
    xs = x_sample.reshape(Bs, D)
    proj_p = _in_proj(xp, mod_p[1], mod_p[0], cos_p, sin_p, n1, qn_t, kn_t, gmat, w_in_bf, tm, L // tm, L // tm)
    proj_s = _in_proj(xs, mod_s[1], mod_s[0], cos_s, sin_s, n1, qn_t, kn_t, gmat, w_in_bf, Bs, 1, 1)
    qr, kr, vr, sg, qa, ka, va = [t.reshape(B, L, W) for t in proj_p]
    qr_s, kr_s, vr_s, sg_s, qa_s, ka_s, va_s = proj_s

    o_r, state_p = _retention_prompt(qr, kr, vr, sg, gn)
    o_a = _attention_prompt(qa, ka, va)
    keep = min(MAX_WINDOW, L)
    cache_kp = ka[:, L - keep:].reshape(B, keep, N_ATT_HEADS, Dh)
    cache_vp = va[:, L - keep:].reshape(B, keep, N_ATT_HEADS, Dh)

    log_g = jnp.log1p(-(2.0 ** (-5.0 - jnp.arange(H, dtype=F32))))
    gamma = jnp.exp(log_g).astype(F32)
    tr = lambda t: t.T.reshape(H, Dh, Bs)
    S_t = jnp.transpose(state_ret, (1, 2, 3, 0))
    orT, S_new_t = _retention_sample(tr(qr_s), tr(kr_s), tr(vr_s), tr(sg_s), gn_ret.reshape(H, Dh, 1), gamma, S_t)
    o_r_s = orT.reshape(W, Bs).T
    state_s = jnp.transpose(S_new_t, (3, 0, 1, 2))

    kb_t = jnp.transpose(cache_k, (0, 2, 3, 1))
    vb_t = jnp.transpose(cache_v, (0, 2, 3, 1))
    o_a_s, ko_t, vo_t = _attention_sample(qa_s.reshape(Bs, 1, W), ka_s.reshape(Bs, 1, W), va_s.reshape(Bs, 1, W),
                                          ka_s.T, va_s.T, kb_t, vb_t)
    cache_ks = jnp.transpose(ko_t, (0, 3, 1, 2))
    cache_vs = jnp.transpose(vo_t, (0, 3, 1, 2))

    wo = w_out.astype(BF16)
    wrh = w_router.astype(BF16)
    wrl = (w_router - wrh.astype(F32)).astype(BF16)
    wsg, wsu, wsd = w_sh_gate.astype(BF16), w_sh_up.astype(BF16), w_sh_down.astype(BF16)
    base_p, h2_p, s_p = _post_mixer(o_r.reshape(Np, W), o_a.reshape(Np, W), xp, mod_p[2], mod_p[4], mod_p[3],
                                    mod_p[5], n2, wo, wrh, wrl, wsg, wsu, wsd, tm, L // tm)
    base_s, h2_s, s_s = _post_mixer(o_r_s, o_a_s.reshape(Bs, W), xs, mod_s[2], mod_s[4], mod_s[3],
                                    mod_s[5], n2, wo, wrh, wrl, wsg, wsu, wsd, Bs, 1)

    h2_all = jnp.concatenate([h2_p, h2_s], axis=0)
    s_all = jnp.concatenate([s_p, s_s], axis=0)
    routed = _moe_routed(h2_all, s_all, router_bias, w_exp_gate, w_exp_up, w_exp_down)
    y_p = _final(base_p, routed[:Np], mod_p[5], tm, L // tm)
    y_s = _final(base_s, routed[Np:], mod_s[5], Bs, 1)

    return (y_p.reshape(B, L, D), y_s.reshape(Bs, 1, D), state_p, cache_kp, cache_vp, state_s, cache_ks, cache_vs)


def kernel(x_prompt, x_sample, c_prompt, c_sample, state_ret, cache_win_k, cache_win_v, norm1, w_ada, b_ada, w_in,
           gn_ret, q_norm, k_norm, w_out, norm2, w_router, router_bias, w_exp_gate, w_exp_up, w_exp_down,
           w_sh_gate, w_sh_up, w_sh_down):
    assert w_in.shape[0] == 1, "single-layer step"
    assert x_sample.shape[1] == 1, "one new token per sample sequence"
    outs = _layer(x_prompt, x_sample, c_prompt, c_sample, state_ret[0], cache_win_k[0], cache_win_v[0],
                  norm1[0], w_ada[0], b_ada[0], w_in[0], gn_ret[0], q_norm[0], k_norm[0], w_out[0], norm2[0],
                  w_router[0], router_bias[0], w_exp_gate[0], w_exp_up[0], w_exp_down[0],
                  w_sh_gate[0], w_sh_up[0], w_sh_down[0])
    yp, ys, sp, kp, vp, ss, ksm, vsm = outs
    return (yp, ys, sp[None], kp[None], vp[None], ss[None], ksm[None], vsm[None])
```

```python
import functools

import jax
import jax.numpy as jnp
from jax import lax
from jax.experimental import pallas as pl
from jax.experimental.pallas import tpu as pltpu

HEAD_DIM = 64
N_RET_HEADS = 8
N_ATT_HEADS = 8
RET_WIDTH = N_RET_HEADS * HEAD_DIM
ATT_WIDTH = N_ATT_HEADS * HEAD_DIM
RET_CHUNK = 128
DILATED_PATTERNS = ((128, 1), (512, 4), (2048, 16))
MAX_WINDOW = 2048
PAST_LEN = 8192
ROPE_THETA = 10000.0
N_EXPERTS = 256
TOP_K = 8
N_EXPERT_GROUPS = 8
TOPK_GROUPS = 4
ROUTED_SCALE = 2.5
NORM_EPS = 1e-6
GN_EPS = 1e-5
NEG_INF = -1e30

F32 = jnp.float32
BF16 = jnp.bfloat16

VMEM_LIMIT_BYTES = 56 * 1024 * 1024
EXPERT_TILE = 256
TOKEN_TILE = 512
COMBINE_TILE = 256


def _cparams(*sem):
    return pltpu.CompilerParams(dimension_semantics=sem, vmem_limit_bytes=VMEM_LIMIT_BYTES)


def _dot(a, b):
    return jnp.dot(a, b, preferred_element_type=F32)


def _dot_nt(a, b):
    return lax.dot_general(a, b, (((1,), (1,)), ((), ())), preferred_element_type=F32)


def _dot_tn(a, b):
    return lax.dot_general(a, b, (((0,), (0,)), ((), ())), preferred_element_type=F32)


def _mod_kernel(c_ref, w_ref, b_ref, o_ref):
    c = c_ref[...]
    a = (c * jax.nn.sigmoid(c)).astype(BF16)
    o_ref[...] = _dot(a, w_ref[...].astype(BF16)) + b_ref[...]


def _modulation(c, w_ada, b_ada):
    R, D = c.shape
    n_out = w_ada.shape[1]
    tn = 1024
    return pl.pallas_call(
        _mod_kernel,
        grid=(n_out // tn,),
        in_specs=[
            pl.BlockSpec((R, D), lambda j: (0, 0)),
            pl.BlockSpec((D, tn), lambda j: (0, j)),
            pl.BlockSpec((1, tn), lambda j: (0, j)),
        ],
        out_specs=pl.BlockSpec((R, tn), lambda j: (0, j)),
        out_shape=jax.ShapeDtypeStruct((R, n_out), F32),
        compiler_params=_cparams("parallel"),
        name="modulation",
    )(c, w_ada, b_ada.reshape(1, n_out))


def _swap_halves(x, first_half):
    n = x.shape[-1]
    return jnp.where(first_half, pltpu.roll(x, n - HEAD_DIM // 2, 1), pltpu.roll(x, HEAD_DIM // 2, 1))


def _head_mean(x, g):
    hi = x.astype(BF16)
    lo = (x - hi.astype(F32)).astype(BF16)
    return _dot(hi, g) + _dot(lo, g)


def _in_proj_kernel(x_ref, sc_ref, sh_ref, cos_ref, sin_ref, n1_ref, qn_ref, kn_ref, g_ref, w_ref,
                    qr_ref, kr_ref, vr_ref, sg_ref, qa_ref, ka_ref, va_ref):
    x = x_ref[...]
    ms = jnp.mean(x * x, axis=-1, keepdims=True)
    h = x * lax.rsqrt(ms + NORM_EPS) * n1_ref[...]
    h = (h * (1.0 + sc_ref[0]) + sh_ref[0]).astype(BF16)
    cos = cos_ref[...]
    sin = sin_ref[...]
    W = RET_WIDTH
    lane = lax.broadcasted_iota(jnp.int32, (1, W), 1)
    first_half = (lane % HEAD_DIM) < (HEAD_DIM // 2)

    def proj(c):
        return _dot(h, w_ref[:, c * W:(c + 1) * W])

    def rot(t):
        return t * cos + _swap_halves(t, first_half) * sin

    def qk_norm(t, gain):
        return t * lax.rsqrt(_head_mean(t * t, g_ref[...]) + NORM_EPS) * gain

    qr_ref[...] = rot(proj(0))
    kr_ref[...] = rot(proj(1)) * (HEAD_DIM ** -0.5)
    vr_ref[...] = proj(2)
    gr = proj(3)
    sg_ref[...] = gr * jax.nn.sigmoid(gr)
    qa_ref[...] = rot(qk_norm(proj(4), qn_ref[...]))
    ka_ref[...] = rot(qk_norm(proj(5), kn_ref[...]))
    va_ref[...] = proj(6)


def _in_proj(x, sc, sh, cos, sin, norm1, q_norm_t, k_norm_t, gmat, w_in_bf, tm, rows_per_mod, pos_blocks):
    N, D = x.shape
    W = RET_WIDTH
    R = sc.shape[1]
    mod_spec = pl.BlockSpec((1, R, D), lambda i: (i // rows_per_mod, 0, 0))
    pos_spec = pl.BlockSpec((cos.shape[0] // pos_blocks, W), lambda i: (i % pos_blocks, 0))
    const = lambda shape: pl.BlockSpec(shape, lambda i: (0,) * len(shape))
    out_spec = pl.BlockSpec((tm, W), lambda i: (i, 0))
    return pl.pallas_call(
        _in_proj_kernel,
        grid=(N // tm,),
        in_specs=[
            pl.BlockSpec((tm, D), lambda i: (i, 0)),
            mod_spec, mod_spec, pos_spec, pos_spec,
            const((1, D)), const((1, W)), const((1, W)), const((W, W)), const(w_in_bf.shape),
        ],
        out_specs=[out_spec] * 7,
        out_shape=[jax.ShapeDtypeStruct((N, W), F32)] * 7,
        compiler_params=_cparams("parallel"),
        name="in_proj",
    )(x, sc, sh, cos, sin, norm1, q_norm_t, k_norm_t, gmat, w_in_bf)


def _ret_kernel(q_ref, k_ref, v_ref, sg_ref, dm_ref, qd_ref, kd_ref, sd_ref, gn_ref, o_ref, s_ref):
    @pl.when(pl.program_id(1) == 0)
    def _():
        s_ref[...] = jnp.zeros_like(s_ref)

    q = q_ref[0]
    k = k_ref[0]
    v = v_ref[0]
    kd = k * kd_ref[...]
    qd = qd_ref[...]
    outs = []
    for h in range(N_RET_HEADS):
        sl = slice(HEAD_DIM * h, HEAD_DIM * (h + 1))
        qh = q[:, sl].astype(BF16)
        kh = k[:, sl].astype(BF16)
        vh = v[:, sl].astype(BF16)
        S = s_ref[0, h]
        sc = _dot_nt(qh, kh) * dm_ref[h]
        o = _dot(sc.astype(BF16), vh) + _dot(qh, S.astype(BF16)) * qd[:, sl]
        s_ref[0, h] = S * sd_ref[h] + _dot_tn(kd[:, sl].astype(BF16), vh)
        mu = jnp.mean(o, axis=-1, keepdims=True)
        oc = o - mu
        var = jnp.mean(oc * oc, axis=-1, keepdims=True)
        outs.append(oc * lax.rsqrt(var + GN_EPS))
    o_ref[0] = jnp.concatenate(outs, axis=1) * gn_ref[...] * sg_ref[0]


def _decay_tables(C):
    H = N_RET_HEADS
    log_g = jnp.log1p(-(2.0 ** (-5.0 - jnp.arange(H, dtype=F32))))
    i = jnp.arange(C)
    diff = i[:, None] - i[None, :]
    dmask = jnp.where(diff[None] >= 0, jnp.exp(log_g[:, None, None] * jnp.maximum(diff, 0)[None]), 0.0).astype(F32)
    q_decay = jnp.exp(log_g[None, :] * (i[:, None] + 1)).astype(F32)
    k_decay = jnp.exp(log_g[None, :] * (C - 1 - i)[:, None]).astype(F32)
    s_decay = jnp.exp(log_g * C).astype(F32)
    return dmask, q_decay, k_decay, s_decay


def _retention_prompt(q, k, v, sg, gn):
    B, L, W = q.shape
    C = RET_CHUNK
    H, Dh = N_RET_HEADS, HEAD_DIM
    dmask, q_decay, k_decay, s_decay = _decay_tables(C)
    qd = jnp.repeat(q_decay, Dh, axis=1)
    kd = jnp.repeat(k_decay, Dh, axis=1)
    sd = jnp.broadcast_to(s_decay[:, None, None], (H, Dh, Dh))
    blk = pl.BlockSpec((1, C, W), lambda b, c: (b, c, 0))
    const = lambda shape: pl.BlockSpec(shape, lambda b, c: (0,) * len(shape))
    return pl.pallas_call(
        _ret_kernel,
        grid=(B, L // C),
        in_specs=[blk, blk, blk, blk, const((H, C, C)), const((C, W)), const((C, W)), const((H, Dh, Dh)), const((1, W))],
        out_specs=[blk, pl.BlockSpec((1, H, Dh, Dh), lambda b, c: (b, 0, 0, 0))],
        out_shape=[jax.ShapeDtypeStruct((B, L, W), F32), jax.ShapeDtypeStruct((B, H, Dh, Dh), F32)],
        compiler_params=_cparams("parallel", "arbitrary"),
        name="retention_prompt",
    )(q, k, v, sg, dmask, qd, kd, sd, gn)


def _attn_kernel(q_ref, k_ref, v_ref, o_ref, oacc, lacc):
    L = q_ref.shape[1]
    P2 = q_ref.shape[2]
    lane = lax.broadcasted_iota(jnp.int32, (1, P2), 1)
    head0 = lane < HEAD_DIM
    scale = HEAD_DIM ** -0.5

    for p, (window, d) in enumerate(DILATED_PATTERNS):
        band = window // d
        nb = L // (d * band)
        qi = lax.broadcasted_iota(jnp.int32, (band, 2 * band), 0)
        ki = lax.broadcasted_iota(jnp.int32, (band, 2 * band), 1)
        dist = qi + band - ki
        in_band = (dist >= 0) & (dist <= band)

        def body(idx, carry, d=d, band=band, nb=nb, p=p, in_band=in_band, ki=ki):
            r = idx // nb
            n = idx % nb
            qs = r + d * band * n
            ps = jnp.maximum(qs - d * band, r)
            cur = pl.ds(qs, band, stride=d) if d > 1 else pl.ds(qs, band)
            prev = pl.ds(ps, band, stride=d) if d > 1 else pl.ds(ps, band)
            qb = q_ref[0, cur, :]
            kk = jnp.concatenate([k_ref[0, prev, :], k_ref[0, cur, :]], axis=0).astype(BF16)
            vv = jnp.concatenate([v_ref[0, prev, :], v_ref[0, cur, :]], axis=0).astype(BF16)
            mask = in_band & ((ki >= band) | (n > 0))
            res = []
            for hm in (head0, jnp.logical_not(head0)):
                qh = jnp.where(hm, qb, 0.0).astype(BF16)
                s = _dot_nt(qh, kk) * scale
                s = jnp.where(mask, s, NEG_INF)
                m = jnp.max(s, axis=-1, keepdims=True)
                e = jnp.exp(s - m)
                l = jnp.sum(e, axis=-1, keepdims=True)
                o = _dot((e / l).astype(BF16), vv)
                res.append((o, m + jnp.log(l)))
            oacc[p, cur, :] = jnp.where(head0, res[0][0], res[1][0])
            lacc[p, cur, :] = jnp.where(head0, res[0][1], res[1][1])
            return carry

        lax.fori_loop(0, d * nb, body, 0)

    rows = 512

    def combine(i, carry):
        sl = pl.ds(pl.multiple_of(i * rows, rows), rows)
        l0, l1, l2 = lacc[0, sl, :], lacc[1, sl, :], lacc[2, sl, :]
        m = jnp.maximum(jnp.maximum(l0, l1), l2)
        w0, w1, w2 = jnp.exp(l0 - m), jnp.exp(l1 - m), jnp.exp(l2 - m)
        tot = w0 + w1 + w2
        o_ref[0, sl, :] = (w0 * oacc[0, sl, :] + w1 * oacc[1, sl, :] + w2 * oacc[2, sl, :]) / tot
        return carry

    lax.fori_loop(0, L // rows, combine, 0)


def _attention_prompt(q, k, v):
    B, L, W = q.shape
    P2 = 2 * HEAD_DIM
    blk = pl.BlockSpec((1, L, P2), lambda b, hp: (b, 0, hp))
    return pl.pallas_call(
        _attn_kernel,
        grid=(B, W // P2),
        in_specs=[blk, blk, blk],
        out_specs=blk,
        out_shape=jax.ShapeDtypeStruct((B, L, W), F32),
        scratch_shapes=[pltpu.VMEM((len(DILATED_PATTERNS), L, P2), F32)] * 2,
        compiler_params=_cparams("parallel", "parallel"),
        name="attention_prompt",
    )(q, k, v)


def _pack_bf16_pair(a, b):
    ua = pltpu.bitcast(a.astype(BF16).astype(F32), jnp.uint32)
    ub = pltpu.bitcast(b.astype(BF16).astype(F32), jnp.uint32)
    return ua | (ub >> 16)


def _unpack_bf16_pair(p):
    a = pltpu.bitcast(p & jnp.uint32(0xFFFF0000), F32)
    b = pltpu.bitcast(p << 16, F32)
    return a, b


ROUTE_LANES = 128


def _route_chunk(s, bias, carry):
    E, n = s.shape
    G, GS = N_EXPERT_GROUPS, E // N_EXPERT_GROUPS
    NEG = -jnp.inf
    choice = s + bias
    row = lax.broadcasted_iota(jnp.int32, (E, n), 0).astype(F32)
    lrow = row[:GS]
    gs_rows = []
    for g in range(G):
        c = choice[g * GS:(g + 1) * GS]
        m1 = jnp.max(c, axis=0, keepdims=True)
        i1 = jnp.min(jnp.where(c == m1, lrow, float(GS)), axis=0, keepdims=True)
        m2 = jnp.max(jnp.where(lrow == i1, NEG, c), axis=0, keepdims=True)
        gs_rows.append(m1 + m2)
    gs = jnp.concatenate(gs_rows, axis=0)
    grow = lax.broadcasted_iota(jnp.int32, (G, n), 0).astype(F32)
    gsel = jnp.zeros((G, n), F32)
    for _ in range(TOPK_GROUPS):
        gm = jnp.max(gs, axis=0, keepdims=True)
        gi = jnp.min(jnp.where(gs == gm, grow, float(G)), axis=0, keepdims=True)
        hit = grow == gi
        gsel = jnp.where(hit, 1.0, gsel)
        gs = jnp.where(hit, NEG, gs)
    emask = jnp.concatenate([jnp.broadcast_to(gsel[g:g + 1], (GS, n)) for g in range(G)], axis=0)
    masked = jnp.where(emask > 0.5, choice, NEG)
    ids, sks = [], []
    member = jnp.zeros((E, n), F32)
    for _ in range(TOP_K):
        mk = jnp.max(masked, axis=0, keepdims=True)
        ik = jnp.min(jnp.where(masked == mk, row, float(E)), axis=0, keepdims=True)
        sel = row == ik
        sks.append(jnp.sum(jnp.where(sel, s, 0.0), axis=0, keepdims=True))
        masked = jnp.where(sel, NEG, masked)
        member = jnp.where(sel, 1.0, member)
        ids.append(ik)
    eidx = jnp.concatenate(ids, axis=0)
    sk = jnp.concatenate(sks, axis=0)
    w = sk / jnp.sum(sk, axis=0, keepdims=True) * ROUTED_SCALE
    ti = lax.broadcasted_iota(jnp.int32, (n, n), 0)
    tj = lax.broadcasted_iota(jnp.int32, (n, n), 1)
    upper = jnp.where(ti <= tj, 1.0, 0.0).astype(BF16)
    incl = _dot(member.astype(BF16), upper)
    rank_dense = carry + incl - member
    ranks = [jnp.sum(jnp.where(row == ids[k], rank_dense, 0.0), axis=0, keepdims=True) for k in range(TOP_K)]
    return eidx, w, jnp.concatenate(ranks, axis=0), carry + incl[:, n - 1:n]


def _post_kernel(or_ref, oa_ref, x_ref, g1_ref, sc_ref, sh_ref, g2_ref, n2_ref, wo_ref, wrh_ref, wrl_ref, rb_ref,
                 cin_ref, wsg_ref, wsu_ref, wsd_ref, base_ref, h2_ref, eidx_ref, w_ref, rank_ref, cnt_ref):
    @pl.when(pl.program_id(0) == 0)
    def _():
        cnt_ref[...] = cin_ref[...]

    W = RET_WIDTH
    mix = _dot(or_ref[...].astype(BF16), wo_ref[:W, :]) + _dot(oa_ref[...].astype(BF16), wo_ref[W:, :])
    x2 = x_ref[...] + g1_ref[0] * mix
    ms = jnp.mean(x2 * x2, axis=-1, keepdims=True)
    h2 = x2 * lax.rsqrt(ms + NORM_EPS) * n2_ref[...]
    h2 = h2 * (1.0 + sc_ref[0]) + sh_ref[0]
    hb = h2.astype(BF16)
    hl = (h2 - hb.astype(F32)).astype(BF16)
    half = h2.shape[1] // 2
    h2_ref[...] = _pack_bf16_pair(h2[:, :half], h2[:, half:])
    g = _dot(hb, wsg_ref[...])
    u = _dot(hb, wsu_ref[...])
    shared = _dot((g * jax.nn.sigmoid(g) * u).astype(BF16), wsd_ref[...])
    base_ref[...] = x2 + g2_ref[0] * shared

    logits = _dot_nt(wrh_ref[...], hb) + (_dot_nt(wrh_ref[...], hl) + _dot_nt(wrl_ref[...], hb))
    s = jax.nn.sigmoid(logits)
    carry = cnt_ref[...]
    n = ROUTE_LANES
    for j in range(s.shape[1] // n):
        sl = slice(j * n, (j + 1) * n)
        eidx, w, rank, carry = _route_chunk(s[:, sl], rb_ref[...], carry)
        eidx_ref[:, sl] = eidx.astype(jnp.int32)
        w_ref[:, sl] = w
        rank_ref[:, sl] = rank.astype(jnp.int32)
    cnt_ref[...] = carry


def _post_mixer(o_r, o_a, x, g1, sc2, sh2, g2, norm2, wo, wrh_t, wrl_t, rbias, counts_in, wsg, wsu, wsd, tm,
                rows_per_mod):
    N, D = x.shape
    W = RET_WIDTH
    E = wrh_t.shape[0]
    R = g1.shape[1]
    mod_spec = pl.BlockSpec((1, R, D), lambda i: (i // rows_per_mod, 0, 0))
    const = lambda a: pl.BlockSpec(a.shape, lambda i: (0,) * a.ndim)
    tok = lambda rows: pl.BlockSpec((rows, tm), lambda i: (0, i))
    return pl.pallas_call(
        _post_kernel,
        grid=(N // tm,),
        in_specs=[
            pl.BlockSpec((tm, W), lambda i: (i, 0)),
            pl.BlockSpec((tm, W), lambda i: (i, 0)),
            pl.BlockSpec((tm, D), lambda i: (i, 0)),
            mod_spec, mod_spec, mod_spec, mod_spec,
            const(norm2), const(wo), const(wrh_t), const(wrl_t), const(rbias), const(counts_in),
            const(wsg), const(wsu), const(wsd),
        ],
        out_specs=[
            pl.BlockSpec((tm, D), lambda i: (i, 0)),
            pl.BlockSpec((tm, D // 2), lambda i: (i, 0)),
            tok(TOP_K), tok(TOP_K), tok(TOP_K),
            pl.BlockSpec((E, 1), lambda i: (0, 0)),
        ],
        out_shape=[
            jax.ShapeDtypeStruct((N, D), F32),
            jax.ShapeDtypeStruct((N, D // 2), jnp.uint32),
            jax.ShapeDtypeStruct((TOP_K, N), jnp.int32),
            jax.ShapeDtypeStruct((TOP_K, N), F32),
            jax.ShapeDtypeStruct((TOP_K, N), jnp.int32),
            jax.ShapeDtypeStruct((E, 1), F32),
        ],
        compiler_params=_cparams("arbitrary"),
        name="post_mixer",
    )(o_r, o_a, x, g1, sc2, sh2, g2, norm2, wo, wrh_t, wrl_t, rbias, counts_in, wsg, wsu, wsd)


def _expert_kernel(be_ref, first_ref, nused_ref, x_ref, wg_ref, wu_ref, wd_ref, y_ref, wg_s, wu_s, wd_s):
    i = pl.program_id(0)

    @pl.when(first_ref[i] == 1)
    def _():
        wg_s[...] = wg_ref[0].astype(BF16)
        wu_s[...] = wu_ref[0].astype(BF16)
        wd_s[...] = wd_ref[0].astype(BF16)

    @pl.when(i < nused_ref[0])
    def _():
        xa, xb = _unpack_bf16_pair(x_ref[...])
        xa = xa.astype(BF16)
        xb = xb.astype(BF16)
        half = xa.shape[1]
        g = _dot(xa, wg_s[:half, :]) + _dot(xb, wg_s[half:, :])
        u = _dot(xa, wu_s[:half, :]) + _dot(xb, wu_s[half:, :])
        hmid = (g * jax.nn.sigmoid(g) * u).astype(BF16)
        y_ref[...] = _pack_bf16_pair(_dot(hmid, wd_s[:, :half]), _dot(hmid, wd_s[:, half:]))

    @pl.when(i >= nused_ref[0])
    def _():
        y_ref[...] = jnp.zeros_like(y_ref)


def _expert_matmul(xs, blk_exp, first, n_used, w_gate, w_up, w_down):
    P, Dh2 = xs.shape
    T = EXPERT_TILE
    D, F = w_gate.shape[1], w_gate.shape[2]
    row = pl.BlockSpec((T, Dh2), lambda i, be, fi, nu: (i, 0))
    return pl.pallas_call(
        _expert_kernel,
        grid_spec=pltpu.PrefetchScalarGridSpec(
            num_scalar_prefetch=3,
            grid=(P // T,),
            in_specs=[
                row,
                pl.BlockSpec((1, D, F), lambda i, be, fi, nu: (be[i], 0, 0)),
                pl.BlockSpec((1, D, F), lambda i, be, fi, nu: (be[i], 0, 0)),
                pl.BlockSpec((1, F, D), lambda i, be, fi, nu: (be[i], 0, 0)),
            ],
            out_specs=row,
            scratch_shapes=[pltpu.VMEM((D, F), BF16), pltpu.VMEM((D, F), BF16), pltpu.VMEM((F, D), BF16)],
        ),
        out_shape=jax.ShapeDtypeStruct((P, Dh2), jnp.uint32),
        compiler_params=_cparams("arbitrary"),
        name="routed_experts",
    )(blk_exp, first, n_used, xs, w_gate, w_up, w_down)


def _dest_kernel(eidx_ref, rank_ref, off_ref, dest_ref):
    E = off_ref.shape[0]
    n = eidx_ref.shape[1]
    row = lax.broadcasted_iota(jnp.int32, (E, n), 0)
    off = off_ref[...]
    rows = []
    for k in range(eidx_ref.shape[0]):
        hit = row == eidx_ref[k:k + 1, :]
        rows.append(jnp.sum(jnp.where(hit, off, 0.0), axis=0, keepdims=True))
    dest_ref[...] = rank_ref[...] + jnp.concatenate(rows, axis=0).astype(jnp.int32)


def _dest_rows(eidx, rank, pad_off):
    K, N = eidx.shape
    tn = ROUTE_LANES
    blk = pl.BlockSpec((K, tn), lambda i: (0, i))
    return pl.pallas_call(
        _dest_kernel,
        grid=(N // tn,),
        in_specs=[blk, blk, pl.BlockSpec(pad_off.shape, lambda i: (0, 0))],
        out_specs=blk,
        out_shape=jax.ShapeDtypeStruct((K, N), jnp.int32),
        compiler_params=_cparams("parallel"),
        name="dest_rows",
    )(eidx, rank, pad_off)


def _row_copy(src, dst, sem):
    return pltpu.make_async_copy(src, dst, sem)


def _dispatch_kernel(dest_ref, h_ref, xs_in_ref, xs_ref, sem):
    del xs_in_ref
    K, tm = dest_ref.shape

    def issue(t, carry):
        for k in range(K):
            _row_copy(h_ref.at[pl.ds(t, 1)], xs_ref.at[pl.ds(dest_ref[k, t], 1)], sem).start()
        return carry

    lax.fori_loop(0, tm, issue, 0)
    for _ in range(K):
        _row_copy(h_ref, xs_ref.at[pl.ds(0, tm)], sem).wait()


def _dispatch(dest, h2p, xs, tm):
    K, N = dest.shape
    Dh2 = h2p.shape[1]
    return pl.pallas_call(
        _dispatch_kernel,
        grid=(N // tm,),
        in_specs=[
            pl.BlockSpec((K, tm), lambda i: (0, i), memory_space=pltpu.SMEM),
            pl.BlockSpec((tm, Dh2), lambda i: (i, 0)),
            pl.BlockSpec(memory_space=pl.ANY),
        ],
        out_specs=pl.BlockSpec(memory_space=pl.ANY),
        out_shape=jax.ShapeDtypeStruct(xs.shape, xs.dtype),
        scratch_shapes=[pltpu.SemaphoreType.DMA],
        input_output_aliases={2: 0},
        compiler_params=_cparams("arbitrary"),
        name="dispatch",
    )(dest, h2p, xs)


def _combine_kernel(dest_ref, base_ref, w_ref, g2_ref, ys_ref, o_ref, gbuf, sem):
    K, tm = dest_ref.shape

    def issue(t, carry):
        for k in range(K):
            _row_copy(ys_ref.at[pl.ds(dest_ref[k, t], 1)], gbuf.at[k, pl.ds(t, 1)], sem).start()
        return carry

    lax.fori_loop(0, tm, issue, 0)
    for k in range(K):
        _row_copy(ys_ref.at[pl.ds(0, tm)], gbuf.at[k], sem).wait()
    half = gbuf.shape[2]
    w = w_ref[...]
    acc_a = jnp.zeros((tm, half), F32)
    acc_b = jnp.zeros((tm, half), F32)
    for k in range(K):
        a, b = _unpack_bf16_pair(gbuf[k])
        wk = w[:, k:k + 1]
        acc_a = acc_a + wk * a
        acc_b = acc_b + wk * b
    g2 = g2_ref[0]
    o_ref[:, :half] = base_ref[:, :half] + g2[:, :half] * acc_a
    o_ref[:, half:] = base_ref[:, half:] + g2[:, half:] * acc_b


def _combine(dest, base, w_tok, g2, ys, tm, rows_per_mod):
    K, N = dest.shape
    D = base.shape[1]
    R = g2.shape[1]
    return pl.pallas_call(
        _combine_kernel,
        grid=(N // tm,),
        in_specs=[
            pl.BlockSpec((K, tm), lambda i: (0, i), memory_space=pltpu.SMEM),
            pl.BlockSpec((tm, D), lambda i: (i, 0)),
            pl.BlockSpec((tm, K), lambda i: (i, 0)),
            pl.BlockSpec((1, R, D), lambda i: (i // rows_per_mod, 0, 0)),
            pl.BlockSpec(memory_space=pl.ANY),
        ],
        out_specs=pl.BlockSpec((tm, D), lambda i: (i, 0)),
        out_shape=jax.ShapeDtypeStruct((N, D), F32),
        scratch_shapes=[pltpu.VMEM((K, tm, D // 2), jnp.uint32), pltpu.SemaphoreType.DMA],
        compiler_params=_cparams("arbitrary"),
        name="combine",
    )(dest, base, w_tok, g2, ys)


def _tile_plan(counts, n_tiles):
    T = EXPERT_TILE
    E = counts.shape[0]
    padded = (counts + T - 1) // T * T
    pad_end = jnp.cumsum(padded)
    pad_off = pad_end - padded
    tile_start = jnp.arange(n_tiles, dtype=jnp.int32) * T
    blk_exp = jnp.minimum(jnp.searchsorted(pad_end, tile_start, side='right'), E - 1).astype(jnp.int32)
    first = jnp.concatenate([jnp.ones((1,), jnp.int32), (blk_exp[1:] != blk_exp[:-1]).astype(jnp.int32)])
    n_used = (pad_end[-1] // T).astype(jnp.int32).reshape(1)
    return pad_off, blk_exp, first, n_used


def _ret_sample_kernel(q_ref, k_ref, v_ref, sg_ref, gn_ref, dec_ref, s_ref, o_ref, so_ref):
    q = q_ref[0]
    k = k_ref[0]
    v = v_ref[0]
    gamma = dec_ref[pl.program_id(0)]
    qk = jnp.sum(q * k, axis=0, keepdims=True)
    o = qk * v
    Dh = q.shape[0]
    acc = jnp.zeros_like(v)
    for d in range(Dh):
        S = s_ref[0, d]
        acc = acc + q[d:d + 1, :] * S
        so_ref[0, d] = S * gamma + k[d:d + 1, :] * v
    o = o + acc * gamma
    mu = jnp.mean(o, axis=0, keepdims=True)
    oc = o - mu
    var = jnp.mean(oc * oc, axis=0, keepdims=True)
    o_ref[0] = oc * lax.rsqrt(var + GN_EPS) * gn_ref[0] * sg_ref[0]


def _retention_sample(qT, kT, vT, sgT, gn_col, gamma, S):
    H, Dh, Bt = qT.shape
    vec = pl.BlockSpec((1, Dh, Bt), lambda h: (h, 0, 0))
    st = pl.BlockSpec((1, Dh, Dh, Bt), lambda h: (h, 0, 0, 0))
    return pl.pallas_call(
        _ret_sample_kernel,
        grid=(H,),
        in_specs=[vec, vec, vec, vec, pl.BlockSpec((1, Dh, 1), lambda h: (h, 0, 0)),
                  pl.BlockSpec(memory_space=pltpu.SMEM), st],
        out_specs=[vec, st],
        out_shape=[jax.ShapeDtypeStruct((H, Dh, Bt), F32), jax.ShapeDtypeStruct((H, Dh, Dh, Bt), F32)],
        compiler_params=_cparams("parallel"),
        name="retention_sample",
    )(qT, kT, vT, sgT, gn_col, gamma, S)


def _attn_sample_kernel(q_ref, kn_ref, vn_ref, knT_ref, vnT_ref, kb_ref, vb_ref, o_ref, ko_ref, vo_ref):
    b = pl.program_id(0)
    H, Dh, Wn = kb_ref.shape[1], kb_ref.shape[2], kb_ref.shape[3]
    HD = H * Dh
    scale = HEAD_DIM ** -0.5
    q = q_ref[0]
    kn = kn_ref[0]
    vn = vn_ref[0]
    row = lax.broadcasted_iota(jnp.int32, (H, HD), 0)
    col = lax.broadcasted_iota(jnp.int32, (H, HD), 1)
    own = (col // Dh) == row
    q_bd = jnp.where(own, q, 0.0)
    s_new = jnp.sum(q_bd * kn, axis=1, keepdims=True) * scale

    bl = lax.broadcasted_iota(jnp.int32, knT_ref.shape, 1)
    k_col = jnp.sum(jnp.where(bl == b, knT_ref[...], 0.0), axis=1, keepdims=True)
    v_col = jnp.sum(jnp.where(bl == b, vnT_ref[...], 0.0), axis=1, keepdims=True)

    t = lax.broadcasted_iota(jnp.int32, (1, Wn), 1)
    last = t == (Wn - 1)
    q_bf = q_bd.astype(BF16)
    outs = []
    for h in range(H):
        K = kb_ref[0, h]
        V = vb_ref[0, h]
        hs = slice(h * Dh, (h + 1) * Dh)
        ko_ref[0, h] = jnp.where(last, k_col[hs], pltpu.roll(K, Wn - 1, 1))
        vo_ref[0, h] = jnp.where(last, v_col[hs], pltpu.roll(V, Wn - 1, 1))
        s = _dot(q_bf[:, hs], K.astype(BF16))[h:h + 1] * scale
        sn = s_new[h:h + 1]
        Vb = V.astype(BF16)
        vn_h = vn[:, hs]
        o_p, lse_p = [], []
        for window, d in DILATED_PATTERNS:
            valid = ((t % d) == 0) & (t >= Wn - window)
            sm = jnp.where(valid, s, NEG_INF)
            m = jnp.maximum(jnp.max(sm, axis=1, keepdims=True), sn)
            e = jnp.exp(sm - m)
            en = jnp.exp(sn - m)
            l = jnp.sum(e, axis=1, keepdims=True) + en
            pe = jnp.broadcast_to((e / l).astype(BF16), (8, Wn))
            o = _dot_nt(pe, Vb)[0:1] + (en / l).astype(BF16).astype(F32) * vn_h.astype(BF16).astype(F32)
            o_p.append(o)
            lse_p.append(m + jnp.log(l))
        mm = jnp.maximum(jnp.maximum(lse_p[0], lse_p[1]), lse_p[2])
        ws = [jnp.exp(x - mm) for x in lse_p]
        tot = ws[0] + ws[1] + ws[2]
        outs.append((ws[0] * o_p[0] + ws[1] * o_p[1] + ws[2] * o_p[2]) / tot)
    o_ref[0] = jnp.concatenate(outs, axis=1)


def _attention_sample(q, kn, vn, knT, vnT, k_buf, v_buf):
    Bt, H, Dh, Wn = k_buf.shape
    HD = H * Dh
    vec = pl.BlockSpec((1, 1, HD), lambda b: (b, 0, 0))
    tr = pl.BlockSpec((HD, Bt), lambda b: (0, 0))
    cache = pl.BlockSpec((1, H, Dh, Wn), lambda b: (b, 0, 0, 0))
    return pl.pallas_call(
        _attn_sample_kernel,
        grid=(Bt,),
        in_specs=[vec, vec, vec, tr, tr, cache, cache],
        out_specs=[vec, cache, cache],
        out_shape=[jax.ShapeDtypeStruct((Bt, 1, HD), F32),
                   jax.ShapeDtypeStruct(k_buf.shape, F32), jax.ShapeDtypeStruct(v_buf.shape, F32)],
        compiler_params=_cparams("parallel"),
        name="attention_sample",
    )(q, kn, vn, knT, vnT, k_buf, v_buf)


def _rope_tables(pos):
    half = HEAD_DIM // 2
    inv = ROPE_THETA ** (-jnp.arange(half, dtype=F32) / half)
    ang = pos.astype(F32)[:, None] * inv[None, :]
    cos, sin = jnp.cos(ang), jnp.sin(ang)
    cos_t = jnp.tile(cos, (1, 2 * N_RET_HEADS))
    sin_t = jnp.tile(jnp.concatenate([-sin, sin], axis=1), (1, N_RET_HEADS))
    return cos_t, sin_t


def _layer(x_prompt, x_sample, c_prompt, c_sample, state_ret, cache_k, cache_v,
           norm1, w_ada, b_ada, w_in, gn_ret, q_norm, k_norm, w_out, norm2,
           w_router, router_bias, w_exp_gate, w_exp_up, w_exp_down, w_sh_gate, w_sh_up, w_sh_down):
    B, L, D = x_prompt.shape
    Bs = x_sample.shape[0]
    H, Dh, W = N_RET_HEADS, HEAD_DIM, RET_WIDTH
    Np = B * L

    pad = (-(B + Bs)) % 8
    c_all = jnp.concatenate([c_prompt, c_sample, jnp.zeros((pad, D), F32)], axis=0)
    mod = _modulation(c_all, w_ada, b_ada)
    mod_p = [m.reshape(B, 1, D) for m in jnp.split(mod[:B], 6, axis=-1)]
    mod_s = [m.reshape(1, Bs, D) for m in jnp.split(mod[B:B + Bs], 6, axis=-1)]

    w_in_bf = w_in.astype(BF16)
    gi = jnp.arange(W) // Dh
    gmat = jnp.where(gi[:, None] == gi[None, :], 1.0 / Dh, 0.0).astype(BF16)
    qn_t = jnp.tile(q_norm, N_ATT_HEADS).reshape(1, W)
    kn_t = jnp.tile(k_norm, N_ATT_HEADS).reshape(1, W)
    n1 = norm1.reshape(1, D)
    n2 = norm2.reshape(1, D)
    gn = gn_ret.reshape(1, W)

    tm = TOKEN_TILE
    cos_p, sin_p = _rope_tables(jnp.arange(L, dtype=jnp.int32))
    cos_s, sin_s = _rope_tables(PAST_LEN + jnp.arange(1, dtype=jnp.int32))

    xp = x_prompt.reshape(Np, D)
    xs = x_sample.reshape(Bs, D)
    proj_p = _in_proj(xp, mod_p[1], mod_p[0], cos_p, sin_p, n1, qn_t, kn_t, gmat, w_in_bf, tm, L // tm, L // tm)
    proj_s = _in_proj(xs, mod_s[1], mod_s[0], cos_s, sin_s, n1, qn_t, kn_t, gmat, w_in_bf, Bs, 1, 1)
    qr, kr, vr, sg, qa, ka, va = [t.reshape(B, L, W) for t in proj_p]
    qr_s, kr_s, vr_s, sg_s, qa_s, ka_s, va_s = proj_s

    o_r, state_p = _retention_prompt(qr, kr, vr, sg, gn)
    o_a = _attention_prompt(qa, ka, va)
    keep = min(MAX_WINDOW, L)
    cache_kp = ka[:, L - keep:].reshape(B, keep, N_ATT_HEADS, Dh)
    cache_vp = va[:, L - keep:].reshape(B, keep, N_ATT_HEADS, Dh)

    log_g = jnp.log1p(-(2.0 ** (-5.0 - jnp.arange(H, dtype=F32))))
    gamma = jnp.exp(log_g).astype(F32)
    tr = lambda t: t.T.reshape(H, Dh, Bs)
    S_t = jnp.transpose(state_ret, (1, 2, 3, 0))
    orT, S_new_t = _retention_sample(tr(qr_s), tr(kr_s), tr(vr_s), tr(sg_s), gn_ret.reshape(H, Dh, 1), gamma, S_t)
    o_r_s = orT.reshape(W, Bs).T
    state_s = jnp.transpose(S_new_t, (3, 0, 1, 2))

    kb_t = jnp.transpose(cache_k, (0, 2, 3, 1))
    vb_t = jnp.transpose(cache_v, (0, 2, 3, 1))
    o_a_s, ko_t, vo_t = _attention_sample(qa_s.reshape(Bs, 1, W), ka_s.reshape(Bs, 1, W), va_s.reshape(Bs, 1, W),
                                          ka_s.T, va_s.T, kb_t, vb_t)
    cache_ks = jnp.transpose(ko_t, (0, 3, 1, 2))
    cache_vs = jnp.transpose(vo_t, (0, 3, 1, 2))

    wo = w_out.astype(BF16)
    wr_t = w_router.T
    wrh = wr_t.astype(BF16)
    wrl = (wr_t - wrh.astype(F32)).astype(BF16)
    rbias = router_bias.astype(F32).reshape(N_EXPERTS, 1)
    wsg, wsu, wsd = w_sh_gate.astype(BF16), w_sh_up.astype(BF16), w_sh_down.astype(BF16)
    zero_counts = jnp.zeros((N_EXPERTS, 1), F32)
    base_p, h2_p, eidx_p, w_p, rank_p, counts_p = _post_mixer(
        o_r.reshape(Np, W), o_a.reshape(Np, W), xp, mod_p[2], mod_p[4], mod_p[3], mod_p[5], n2, wo, wrh, wrl, rbias,
        zero_counts, wsg, wsu, wsd, tm, L // tm)
    base_s, h2_s, eidx_s, w_s, rank_s, counts = _post_mixer(
        o_r_s, o_a_s.reshape(Bs, W), xs, mod_s[2], mod_s[4], mod_s[3], mod_s[5], n2, wo, wrh, wrl, rbias,
        counts_p, wsg, wsu, wsd, Bs, 1)

    T = EXPERT_TILE
    n_tiles = ((Np + Bs) * TOP_K + N_EXPERTS * T + T - 1) // T
    pad_off, blk_exp, first, n_used = _tile_plan(counts.reshape(N_EXPERTS).astype(jnp.int32), n_tiles)
    pad_off = pad_off.astype(F32).reshape(N_EXPERTS, 1)
    dest_p = _dest_rows(eidx_p, rank_p, pad_off)
    dest_s = _dest_rows(eidx_s, rank_s, pad_off)
    grouped = jnp.zeros((n_tiles * T, D // 2), jnp.uint32)
    grouped = _dispatch(dest_p, h2_p, grouped, tm)
    grouped = _dispatch(dest_s, h2_s, grouped, Bs)
    ys = _expert_matmul(grouped, blk_exp, first, n_used, w_exp_gate, w_exp_up, w_exp_down)
    tc = COMBINE_TILE
    y_p = _combine(dest_p, base_p, w_p.T, mod_p[5], ys, tc, L // tc)
    y_s = _combine(dest_s, base_s, w_s.T, mod_s[5], ys, Bs, 1)

    return (y_p.reshape(B, L, D), y_s.reshape(Bs, 1, D), state_p, cache_kp, cache_vp, state_s, cache_ks, cache_vs)


def kernel(x_prompt, x_sample, c_prompt, c_sample, state_ret, cache_win_k, cache_win_v, norm1, w_ada, b_ada, w_in,
           gn_ret, q_norm, k_norm, w_out, norm2, w_router, router_bias, w_exp_gate, w_exp_up, w_exp_down,
           w_sh_gate, w_sh_up, w_sh_down):
    assert w_in.shape[0] == 1, "single-layer step"
    assert x_sample.shape[1] == 1, "one new token per sample sequence"
    outs = _layer(x_prompt, x_sample, c_prompt, c_sample, state_ret[0], cache_win_k[0], cache_win_v[0],
                  norm1[0], w_ada[0], b_ada[0], w_in[0], gn_ret[0], q_norm[0], k_norm[0], w_out[0], norm2[0],
                  w_router[0], router_bias[0], w_exp_gate[0], w_exp_up[0], w_exp_down[0],
                  w_sh_gate[0], w_sh_up[0], w_sh_down[0])
    yp, ys, sp, kp, vp, ss, ksm, vsm = outs
    return (yp, ys, sp[None], kp[None], vp[None], ss[None], ksm[None], vsm[None])
```

```python
import functools

import jax
import jax.numpy as jnp
from jax import lax
from jax.experimental import pallas as pl
from jax.experimental.pallas import tpu as pltpu

HEAD_DIM = 64
N_RET_HEADS = 8
N_ATT_HEADS = 8
RET_WIDTH = N_RET_HEADS * HEAD_DIM
ATT_WIDTH = N_ATT_HEADS * HEAD_DIM
RET_CHUNK = 128
DILATED_PATTERNS = ((128, 1), (512, 4), (2048, 16))
MAX_WINDOW = 2048
PAST_LEN = 8192
ROPE_THETA = 10000.0
N_EXPERTS = 256
TOP_K = 8
N_EXPERT_GROUPS = 8
TOPK_GROUPS = 4
ROUTED_SCALE = 2.5
NORM_EPS = 1e-6
GN_EPS = 1e-5
NEG_INF = -1e30

F32 = jnp.float32
BF16 = jnp.bfloat16

VMEM_LIMIT_BYTES = 56 * 1024 * 1024
EXPERT_TILE = 256
TOKEN_TILE = 512
COMBINE_TILE = 256
ATTN_UNROLL = 8


def _cparams(*sem):
    return pltpu.CompilerParams(dimension_semantics=sem, vmem_limit_bytes=VMEM_LIMIT_BYTES)


def _dot(a, b):
    return jnp.dot(a, b, preferred_element_type=F32)


def _dot_nt(a, b):
    return lax.dot_general(a, b, (((1,), (1,)), ((), ())), preferred_element_type=F32)


def _dot_tn(a, b):
    return lax.dot_general(a, b, (((0,), (0,)), ((), ())), preferred_element_type=F32)


def _mod_kernel(c_ref, w_ref, b_ref, o_ref):
    c = c_ref[...]
    a = (c * jax.nn.sigmoid(c)).astype(BF16)
    o_ref[...] = _dot(a, w_ref[...].astype(BF16)) + b_ref[...]


def _modulation(c, w_ada, b_ada):
    R, D = c.shape
    n_out = w_ada.shape[1]
    tn = 1024
    return pl.pallas_call(
        _mod_kernel,
        grid=(n_out // tn,),
        in_specs=[
            pl.BlockSpec((R, D), lambda j: (0, 0)),
            pl.BlockSpec((D, tn), lambda j: (0, j)),
            pl.BlockSpec((1, tn), lambda j: (0, j)),
        ],
        out_specs=pl.BlockSpec((R, tn), lambda j: (0, j)),
        out_shape=jax.ShapeDtypeStruct((R, n_out), F32),
        compiler_params=_cparams("parallel"),
        name="modulation",
    )(c, w_ada, b_ada.reshape(1, n_out))


def _swap_halves(x, first_half):
    n = x.shape[-1]
    return jnp.where(first_half, pltpu.roll(x, n - HEAD_DIM // 2, 1), pltpu.roll(x, HEAD_DIM // 2, 1))


def _head_mean(x, g):
    hi = x.astype(BF16)
    lo = (x - hi.astype(F32)).astype(BF16)
    return _dot(hi, g) + _dot(lo, g)


def _in_proj_kernel(x_ref, sc_ref, sh_ref, cos_ref, sin_ref, n1_ref, qn_ref, kn_ref, g_ref, w_ref,
                    qr_ref, kr_ref, vr_ref, sg_ref, qa_ref, ka_ref, va_ref):
    x = x_ref[...]
    ms = jnp.mean(x * x, axis=-1, keepdims=True)
    h = x * lax.rsqrt(ms + NORM_EPS) * n1_ref[...]
    h = (h * (1.0 + sc_ref[0]) + sh_ref[0]).astype(BF16)
    cos = cos_ref[...]
    sin = sin_ref[...]
    W = RET_WIDTH
    lane = lax.broadcasted_iota(jnp.int32, (1, W), 1)
    first_half = (lane % HEAD_DIM) < (HEAD_DIM // 2)

    def proj(c):
        return _dot(h, w_ref[:, c * W:(c + 1) * W])

    def rot(t):
        return t * cos + _swap_halves(t, first_half) * sin

    def qk_norm(t, gain):
        return t * lax.rsqrt(_head_mean(t * t, g_ref[...]) + NORM_EPS) * gain

    qr_ref[...] = rot(proj(0))
    kr_ref[...] = rot(proj(1)) * (HEAD_DIM ** -0.5)
    vr_ref[...] = proj(2)
    gr = proj(3)
    sg_ref[...] = gr * jax.nn.sigmoid(gr)
    qa_ref[...] = rot(qk_norm(proj(4), qn_ref[...])) * (HEAD_DIM ** -0.5)
    ka_ref[...] = rot(qk_norm(proj(5), kn_ref[...]))
    va_ref[...] = proj(6)


def _in_proj(x, sc, sh, cos, sin, norm1, q_norm_t, k_norm_t, gmat, w_in_bf, tm, rows_per_mod, pos_blocks):
    N, D = x.shape
    W = RET_WIDTH
    R = sc.shape[1]
    mod_spec = pl.BlockSpec((1, R, D), lambda i: (i // rows_per_mod, 0, 0))
    pos_spec = pl.BlockSpec((cos.shape[0] // pos_blocks, W), lambda i: (i % pos_blocks, 0))
    const = lambda shape: pl.BlockSpec(shape, lambda i: (0,) * len(shape))
    out_spec = pl.BlockSpec((tm, W), lambda i: (i, 0))
    return pl.pallas_call(
        _in_proj_kernel,
        grid=(N // tm,),
        in_specs=[
            pl.BlockSpec((tm, D), lambda i: (i, 0)),
            mod_spec, mod_spec, pos_spec, pos_spec,
            const((1, D)), const((1, W)), const((1, W)), const((W, W)), const(w_in_bf.shape),
        ],
        out_specs=[out_spec] * 7,
        out_shape=[jax.ShapeDtypeStruct((N, W), F32)] * 7,
        compiler_params=_cparams("parallel"),
        name="in_proj",
    )(x, sc, sh, cos, sin, norm1, q_norm_t, k_norm_t, gmat, w_in_bf)


def _ret_kernel(q_ref, k_ref, v_ref, sg_ref, dm_ref, qd_ref, kd_ref, sd_ref, gn_ref, o_ref, s_ref):
    @pl.when(pl.program_id(1) == 0)
    def _():
        s_ref[...] = jnp.zeros_like(s_ref)

    q = q_ref[0]
    k = k_ref[0]
    v = v_ref[0]
    kd = k * kd_ref[...]
    qd = qd_ref[...]
    outs = []
    for h in range(N_RET_HEADS):
        sl = slice(HEAD_DIM * h, HEAD_DIM * (h + 1))
        qh = q[:, sl].astype(BF16)
        kh = k[:, sl].astype(BF16)
        vh = v[:, sl].astype(BF16)
        S = s_ref[0, h]
        sc = _dot_nt(qh, kh) * dm_ref[h]
        o = _dot(sc.astype(BF16), vh) + _dot(qh, S.astype(BF16)) * qd[:, sl]
        s_ref[0, h] = S * sd_ref[h] + _dot_tn(kd[:, sl].astype(BF16), vh)
        mu = jnp.mean(o, axis=-1, keepdims=True)
        oc = o - mu
        var = jnp.mean(oc * oc, axis=-1, keepdims=True)
        outs.append(oc * lax.rsqrt(var + GN_EPS))
    o_ref[0] = jnp.concatenate(outs, axis=1) * gn_ref[...] * sg_ref[0]


def _decay_tables(C):
    H = N_RET_HEADS
    log_g = jnp.log1p(-(2.0 ** (-5.0 - jnp.arange(H, dtype=F32))))
    i = jnp.arange(C)
    diff = i[:, None] - i[None, :]
    dmask = jnp.where(diff[None] >= 0, jnp.exp(log_g[:, None, None] * jnp.maximum(diff, 0)[None]), 0.0).astype(F32)
    q_decay = jnp.exp(log_g[None, :] * (i[:, None] + 1)).astype(F32)
    k_decay = jnp.exp(log_g[None, :] * (C - 1 - i)[:, None]).astype(F32)
    s_decay = jnp.exp(log_g * C).astype(F32)
    return dmask, q_decay, k_decay, s_decay


def _retention_prompt(q, k, v, sg, gn):
    B, L, W = q.shape
    C = RET_CHUNK
    H, Dh = N_RET_HEADS, HEAD_DIM
    dmask, q_decay, k_decay, s_decay = _decay_tables(C)
    qd = jnp.repeat(q_decay, Dh, axis=1)
    kd = jnp.repeat(k_decay, Dh, axis=1)
    sd = jnp.broadcast_to(s_decay[:, None, None], (H, Dh, Dh))
    blk = pl.BlockSpec((1, C, W), lambda b, c: (b, c, 0))
    const = lambda shape: pl.BlockSpec(shape, lambda b, c: (0,) * len(shape))
    return pl.pallas_call(
        _ret_kernel,
        grid=(B, L // C),
        in_specs=[blk, blk, blk, blk, const((H, C, C)), const((C, W)), const((C, W)), const((H, Dh, Dh)), const((1, W))],
        out_specs=[blk, pl.BlockSpec((1, H, Dh, Dh), lambda b, c: (b, 0, 0, 0))],
        out_shape=[jax.ShapeDtypeStruct((B, L, W), F32), jax.ShapeDtypeStruct((B, H, Dh, Dh), F32)],
        compiler_params=_cparams("parallel", "arbitrary"),
        name="retention_prompt",
    )(q, k, v, sg, dmask, qd, kd, sd, gn)


def _attn_kernel(q_ref, k_ref, v_ref, o_ref, oacc, lacc):
    L = q_ref.shape[1]
    P2 = q_ref.shape[2]
    lane = lax.broadcasted_iota(jnp.int32, (1, P2), 1)
    head0 = lane < HEAD_DIM

    for p, (window, d) in enumerate(DILATED_PATTERNS):
        band = window // d
        nb = L // (d * band)
        qi = lax.broadcasted_iota(jnp.int32, (2 * band, 2 * band), 0) % band
        ki = lax.broadcasted_iota(jnp.int32, (2 * band, 2 * band), 1)
        dist = qi + band - ki
        in_band = (dist >= 0) & (dist <= band)
        bias_any = jnp.where(in_band, 0.0, NEG_INF)
        bias_first = jnp.where(in_band & (ki >= band), 0.0, NEG_INF)

        def block(idx, d=d, band=band, nb=nb, p=p, bias_any=bias_any, bias_first=bias_first):
            r = idx // nb
            n = idx % nb
            qs = r + d * band * n
            ps = jnp.maximum(qs - d * band, r)
            cur = pl.ds(qs, band, stride=d) if d > 1 else pl.ds(qs, band)
            prev = pl.ds(ps, band, stride=d) if d > 1 else pl.ds(ps, band)
            qb = q_ref[0, cur, :]
            kk = jnp.concatenate([k_ref[0, prev, :], k_ref[0, cur, :]], axis=0).astype(BF16)
            vv = jnp.concatenate([v_ref[0, prev, :], v_ref[0, cur, :]], axis=0).astype(BF16)
            q2 = jnp.concatenate([jnp.where(head0, qb, 0.0), jnp.where(head0, 0.0, qb)], axis=0).astype(BF16)
            s = _dot_nt(q2, kk) + jnp.where(n > 0, bias_any, bias_first)
            m = jnp.max(jnp.maximum(s[:, :band], s[:, band:]), axis=-1, keepdims=True)
            e = jnp.exp(s - m)
            l = jnp.sum(e[:, :band] + e[:, band:], axis=-1, keepdims=True)
            o = _dot(e.astype(BF16), vv) * (1.0 / l)
            lse = m + jnp.log(l)
            oacc[p, cur, :] = jnp.where(head0, o[:band], o[band:])
            lacc[p, cur, :] = jnp.where(head0, lse[:band], lse[band:])

        def body(i, carry, block=block):
            for u in range(ATTN_UNROLL):
                block(i * ATTN_UNROLL + u)
            return carry

        lax.fori_loop(0, d * nb // ATTN_UNROLL, body, 0)

    rows = 512

    def combine(i, carry):
        sl = pl.ds(pl.multiple_of(i * rows, rows), rows)
        l0, l1, l2 = lacc[0, sl, :], lacc[1, sl, :], lacc[2, sl, :]
        m = jnp.maximum(jnp.maximum(l0, l1), l2)
        w0, w1, w2 = jnp.exp(l0 - m), jnp.exp(l1 - m), jnp.exp(l2 - m)
        tot = w0 + w1 + w2
        o_ref[0, sl, :] = (w0 * oacc[0, sl, :] + w1 * oacc[1, sl, :] + w2 * oacc[2, sl, :]) / tot
        return carry

    lax.fori_loop(0, L // rows, combine, 0)


def _attention_prompt(q, k, v):
    B, L, W = q.shape
    P2 = 2 * HEAD_DIM
    blk = pl.BlockSpec((1, L, P2), lambda b, hp: (b, 0, hp))
    return pl.pallas_call(
        _attn_kernel,
        grid=(B, W // P2),
        in_specs=[blk, blk, blk],
        out_specs=blk,
        out_shape=jax.ShapeDtypeStruct((B, L, W), F32),
        scratch_shapes=[pltpu.VMEM((len(DILATED_PATTERNS), L, P2), F32)] * 2,
        compiler_params=_cparams("parallel", "parallel"),
        name="attention_prompt",
    )(q, k, v)


def _pack_bf16_pair(a, b):
    ua = pltpu.bitcast(a.astype(BF16).astype(F32), jnp.uint32)
    ub = pltpu.bitcast(b.astype(BF16).astype(F32), jnp.uint32)
    return ua | (ub >> 16)


def _unpack_bf16_pair(p):
    a = pltpu.bitcast(p & jnp.uint32(0xFFFF0000), F32)
    b = pltpu.bitcast(p << 16, F32)
    return a, b


ROUTE_LANES = 128


def _route_chunk(s, bias, carry):
    E, n = s.shape
    G, GS = N_EXPERT_GROUPS, E // N_EXPERT_GROUPS
    NEG = -jnp.inf
    choice = s + bias
    row = lax.broadcasted_iota(jnp.int32, (E, n), 0).astype(F32)
    lrow = row[:GS]
    gs_rows = []
    for g in range(G):
        c = choice[g * GS:(g + 1) * GS]
        m1 = jnp.max(c, axis=0, keepdims=True)
        i1 = jnp.min(jnp.where(c == m1, lrow, float(GS)), axis=0, keepdims=True)
        m2 = jnp.max(jnp.where(lrow == i1, NEG, c), axis=0, keepdims=True)
        gs_rows.append(m1 + m2)
    gs = jnp.concatenate(gs_rows, axis=0)
    grow = lax.broadcasted_iota(jnp.int32, (G, n), 0).astype(F32)
    gsel = jnp.zeros((G, n), F32)
    for _ in range(TOPK_GROUPS):
        gm = jnp.max(gs, axis=0, keepdims=True)
        gi = jnp.min(jnp.where(gs == gm, grow, float(G)), axis=0, keepdims=True)
        hit = grow == gi
        gsel = jnp.where(hit, 1.0, gsel)
        gs = jnp.where(hit, NEG, gs)
    emask = jnp.concatenate([jnp.broadcast_to(gsel[g:g + 1], (GS, n)) for g in range(G)], axis=0)
    masked = jnp.where(emask > 0.5, choice, NEG)
    ids, sks = [], []
    member = jnp.zeros((E, n), F32)
    for _ in range(TOP_K):
        mk = jnp.max(masked, axis=0, keepdims=True)
        ik = jnp.min(jnp.where(masked == mk, row, float(E)), axis=0, keepdims=True)
        sel = row == ik
        sks.append(jnp.sum(jnp.where(sel, s, 0.0), axis=0, keepdims=True))
        masked = jnp.where(sel, NEG, masked)
        member = jnp.where(sel, 1.0, member)
        ids.append(ik)
    eidx = jnp.concatenate(ids, axis=0)
    sk = jnp.concatenate(sks, axis=0)
    w = sk / jnp.sum(sk, axis=0, keepdims=True) * ROUTED_SCALE
    ti = lax.broadcasted_iota(jnp.int32, (n, n), 0)
    tj = lax.broadcasted_iota(jnp.int32, (n, n), 1)
    upper = jnp.where(ti <= tj, 1.0, 0.0).astype(BF16)
    incl = _dot(member.astype(BF16), upper)
    rank_dense = carry + incl - member
    ranks = [jnp.sum(jnp.where(row == ids[k], rank_dense, 0.0), axis=0, keepdims=True) for k in range(TOP_K)]
    return eidx, w, jnp.concatenate(ranks, axis=0), carry + incl[:, n - 1:n]


def _post_kernel(or_ref, oa_ref, x_ref, g1_ref, sc_ref, sh_ref, g2_ref, n2_ref, wo_ref, wrh_ref, wrl_ref, rb_ref,
                 cin_ref, wsg_ref, wsu_ref, wsd_ref, base_ref, h2_ref, eidx_ref, w_ref, rank_ref, cnt_ref):
    @pl.when(pl.program_id(0) == 0)
    def _():
        cnt_ref[...] = cin_ref[...]

    W = RET_WIDTH
    mix = _dot(or_ref[...].astype(BF16), wo_ref[:W, :]) + _dot(oa_ref[...].astype(BF16), wo_ref[W:, :])
    x2 = x_ref[...] + g1_ref[0] * mix
    ms = jnp.mean(x2 * x2, axis=-1, keepdims=True)
    h2 = x2 * lax.rsqrt(ms + NORM_EPS) * n2_ref[...]
    h2 = h2 * (1.0 + sc_ref[0]) + sh_ref[0]
    hb = h2.astype(BF16)
    hl = (h2 - hb.astype(F32)).astype(BF16)
    half = h2.shape[1] // 2
    h2_ref[...] = _pack_bf16_pair(h2[:, :half], h2[:, half:])
    g = _dot(hb, wsg_ref[...])
    u = _dot(hb, wsu_ref[...])
    shared = _dot((g * jax.nn.sigmoid(g) * u).astype(BF16), wsd_ref[...])
    base_ref[...] = x2 + g2_ref[0] * shared

    logits = _dot_nt(wrh_ref[...], hb) + (_dot_nt(wrh_ref[...], hl) + _dot_nt(wrl_ref[...], hb))
    s = jax.nn.sigmoid(logits)
    carry = cnt_ref[...]
    n = ROUTE_LANES
    for j in range(s.shape[1] // n):
        sl = slice(j * n, (j + 1) * n)
        eidx, w, rank, carry = _route_chunk(s[:, sl], rb_ref[...], carry)
        eidx_ref[:, sl] = eidx.astype(jnp.int32)
        w_ref[:, sl] = w
        rank_ref[:, sl] = rank.astype(jnp.int32)
    cnt_ref[...] = carry


def _post_mixer(o_r, o_a, x, g1, sc2, sh2, g2, norm2, wo, wrh_t, wrl_t, rbias, counts_in, wsg, wsu, wsd, tm,
                rows_per_mod):
    N, D = x.shape
    W = RET_WIDTH
    E = wrh_t.shape[0]
    R = g1.shape[1]
    mod_spec = pl.BlockSpec((1, R, D), lambda i: (i // rows_per_mod, 0, 0))
    const = lambda a: pl.BlockSpec(a.shape, lambda i: (0,) * a.ndim)
    tok = lambda rows: pl.BlockSpec((rows, tm), lambda i: (0, i))
    return pl.pallas_call(
        _post_kernel,
        grid=(N // tm,),
        in_specs=[
            pl.BlockSpec((tm, W), lambda i: (i, 0)),
            pl.BlockSpec((tm, W), lambda i: (i, 0)),
            pl.BlockSpec((tm, D), lambda i: (i, 0)),
            mod_spec, mod_spec, mod_spec, mod_spec,
            const(norm2), const(wo), const(wrh_t), const(wrl_t), const(rbias), const(counts_in),
            const(wsg), const(wsu), const(wsd),
        ],
        out_specs=[
            pl.BlockSpec((tm, D), lambda i: (i, 0)),
            pl.BlockSpec((tm, D // 2), lambda i: (i, 0)),
            tok(TOP_K), tok(TOP_K), tok(TOP_K),
            pl.BlockSpec((E, 1), lambda i: (0, 0)),
        ],
        out_shape=[
            jax.ShapeDtypeStruct((N, D), F32),
            jax.ShapeDtypeStruct((N, D // 2), jnp.uint32),
            jax.ShapeDtypeStruct((TOP_K, N), jnp.int32),
            jax.ShapeDtypeStruct((TOP_K, N), F32),
            jax.ShapeDtypeStruct((TOP_K, N), jnp.int32),
            jax.ShapeDtypeStruct((E, 1), F32),
        ],
        compiler_params=_cparams("arbitrary"),
        name="post_mixer",
    )(o_r, o_a, x, g1, sc2, sh2, g2, norm2, wo, wrh_t, wrl_t, rbias, counts_in, wsg, wsu, wsd)


def _expert_kernel(be_ref, first_ref, nused_ref, x_ref, wg_ref, wu_ref, wd_ref, y_ref, wg_s, wu_s, wd_s):
    i = pl.program_id(0)
    is_first = first_ref[i] == 1
    used = i < nused_ref[0]

    def compute():
        xa, xb = _unpack_bf16_pair(x_ref[...])
        xa = xa.astype(BF16)
        xb = xb.astype(BF16)
        half = xa.shape[1]
        g = _dot(xa, wg_s[:half, :]) + _dot(xb, wg_s[half:, :])
        u = _dot(xa, wu_s[:half, :]) + _dot(xb, wu_s[half:, :])
        hmid = (g * jax.nn.sigmoid(g) * u).astype(BF16)
        y_ref[...] = _pack_bf16_pair(_dot(hmid, wd_s[:, :half]), _dot(hmid, wd_s[:, half:]))

    @pl.when(is_first)
    def _():
        wg_s[...] = wg_ref[0].astype(BF16)
        wu_s[...] = wu_ref[0].astype(BF16)
        wd_s[...] = wd_ref[0].astype(BF16)
        compute()

    @pl.when(jnp.logical_and(jnp.logical_not(is_first), used))
    def _():
        compute()

    @pl.when(jnp.logical_and(jnp.logical_not(is_first), jnp.logical_not(used)))
    def _():
        y_ref[...] = jnp.zeros_like(y_ref)


def _expert_matmul(xs, blk_exp, first, n_used, w_gate, w_up, w_down):
    P, Dh2 = xs.shape
    T = EXPERT_TILE
    D, F = w_gate.shape[1], w_gate.shape[2]
    row = pl.BlockSpec((T, Dh2), lambda i, be, fi, nu: (i, 0))
    return pl.pallas_call(
        _expert_kernel,
        grid_spec=pltpu.PrefetchScalarGridSpec(
            num_scalar_prefetch=3,
            grid=(P // T,),
            in_specs=[
                row,
                pl.BlockSpec((1, D, F), lambda i, be, fi, nu: (be[i], 0, 0)),
                pl.BlockSpec((1, D, F), lambda i, be, fi, nu: (be[i], 0, 0)),
                pl.BlockSpec((1, F, D), lambda i, be, fi, nu: (be[i], 0, 0)),
            ],
            out_specs=row,
            scratch_shapes=[pltpu.VMEM((D, F), BF16), pltpu.VMEM((D, F), BF16), pltpu.VMEM((F, D), BF16)],
        ),
        out_shape=jax.ShapeDtypeStruct((P, Dh2), jnp.uint32),
        compiler_params=_cparams("arbitrary"),
        name="routed_experts",
    )(blk_exp, first, n_used, xs, w_gate, w_up, w_down)


def _dest_kernel(eidx_ref, rank_ref, off_ref, dest_ref):
    E = off_ref.shape[0]
    n = eidx_ref.shape[1]
    row = lax.broadcasted_iota(jnp.int32, (E, n), 0)
    off = off_ref[...]
    rows = []
    for k in range(eidx_ref.shape[0]):
        hit = row == eidx_ref[k:k + 1, :]
        rows.append(jnp.sum(jnp.where(hit, off, 0.0), axis=0, keepdims=True))
    dest_ref[...] = rank_ref[...] + jnp.concatenate(rows, axis=0).astype(jnp.int32)


def _dest_rows(eidx, rank, pad_off):
    K, N = eidx.shape
    tn = ROUTE_LANES
    blk = pl.BlockSpec((K, tn), lambda i: (0, i))
    return pl.pallas_call(
        _dest_kernel,
        grid=(N // tn,),
        in_specs=[blk, blk, pl.BlockSpec(pad_off.shape, lambda i: (0, 0))],
        out_specs=blk,
        out_shape=jax.ShapeDtypeStruct((K, N), jnp.int32),
        compiler_params=_cparams("parallel"),
        name="dest_rows",
    )(eidx, rank, pad_off)


def _row_copy(src, dst, sem):
    return pltpu.make_async_copy(src, dst, sem)


def _dispatch_kernel(dest_ref, h_ref, xs_in_ref, xs_ref, sem):
    del xs_in_ref
    K, tm = dest_ref.shape

    def issue(t, carry):
        for k in range(K):
            _row_copy(h_ref.at[pl.ds(t, 1)], xs_ref.at[pl.ds(dest_ref[k, t], 1)], sem).start()
        return carry

    lax.fori_loop(0, tm, issue, 0)
    for _ in range(K):
        _row_copy(h_ref, xs_ref.at[pl.ds(0, tm)], sem).wait()


def _dispatch(dest, h2p, xs, tm):
    K, N = dest.shape
    Dh2 = h2p.shape[1]
    return pl.pallas_call(
        _dispatch_kernel,
        grid=(N // tm,),
        in_specs=[
            pl.BlockSpec((K, tm), lambda i: (0, i), memory_space=pltpu.SMEM),
            pl.BlockSpec((tm, Dh2), lambda i: (i, 0)),
            pl.BlockSpec(memory_space=pl.ANY),
        ],
        out_specs=pl.BlockSpec(memory_space=pl.ANY),
        out_shape=jax.ShapeDtypeStruct(xs.shape, xs.dtype),
        scratch_shapes=[pltpu.SemaphoreType.DMA],
        input_output_aliases={2: 0},
        compiler_params=_cparams("arbitrary"),
        name="dispatch",
    )(dest, h2p, xs)


def _combine_kernel(dest_ref, base_ref, w_ref, g2_ref, ys_ref, o_ref, gbuf, sem):
    K, tm = dest_ref.shape

    def issue(t, carry):
        for k in range(K):
            _row_copy(ys_ref.at[pl.ds(dest_ref[k, t], 1)], gbuf.at[k, pl.ds(t, 1)], sem).start()
        return carry

    lax.fori_loop(0, tm, issue, 0)
    for k in range(K):
        _row_copy(ys_ref.at[pl.ds(0, tm)], gbuf.at[k], sem).wait()
    half = gbuf.shape[2]
    w = w_ref[...]
    acc_a = jnp.zeros((tm, half), F32)
    acc_b = jnp.zeros((tm, half), F32)
    for k in range(K):
        a, b = _unpack_bf16_pair(gbuf[k])
        wk = w[:, k:k + 1]
        acc_a = acc_a + wk * a
        acc_b = acc_b + wk * b
    g2 = g2_ref[0]
    o_ref[:, :half] = base_ref[:, :half] + g2[:, :half] * acc_a
    o_ref[:, half:] = base_ref[:, half:] + g2[:, half:] * acc_b


def _combine(dest, base, w_tok, g2, ys, tm, rows_per_mod):
    K, N = dest.shape
    D = base.shape[1]
    R = g2.shape[1]
    return pl.pallas_call(
        _combine_kernel,
        grid=(N // tm,),
        in_specs=[
            pl.BlockSpec((K, tm), lambda i: (0, i), memory_space=pltpu.SMEM),
            pl.BlockSpec((tm, D), lambda i: (i, 0)),
            pl.BlockSpec((tm, K), lambda i: (i, 0)),
            pl.BlockSpec((1, R, D), lambda i: (i // rows_per_mod, 0, 0)),
            pl.BlockSpec(memory_space=pl.ANY),
        ],
        out_specs=pl.BlockSpec((tm, D), lambda i: (i, 0)),
        out_shape=jax.ShapeDtypeStruct((N, D), F32),
        scratch_shapes=[pltpu.VMEM((K, tm, D // 2), jnp.uint32), pltpu.SemaphoreType.DMA],
        compiler_params=_cparams("arbitrary"),
        name="combine",
    )(dest, base, w_tok, g2, ys)


def _tile_plan(counts, n_tiles):
    T = EXPERT_TILE
    E = counts.shape[0]
    padded = (counts + T - 1) // T * T
    pad_end = jnp.cumsum(padded)
    pad_off = pad_end - padded
    tile_start = jnp.arange(n_tiles, dtype=jnp.int32) * T
    blk_exp = jnp.minimum(jnp.searchsorted(pad_end, tile_start, side='right'), E - 1).astype(jnp.int32)
    first = jnp.concatenate([jnp.ones((1,), jnp.int32), (blk_exp[1:] != blk_exp[:-1]).astype(jnp.int32)])
    n_used = (pad_end[-1] // T).astype(jnp.int32).reshape(1)
    return pad_off, blk_exp, first, n_used


def _ret_sample_kernel(q_ref, k_ref, v_ref, sg_ref, gn_ref, dec_ref, s_ref, o_ref, so_ref):
    q = q_ref[0]
    k = k_ref[0]
    v = v_ref[0]
    gamma = dec_ref[pl.program_id(0)]
    qk = jnp.sum(q * k, axis=0, keepdims=True)
    o = qk * v
    Dh = q.shape[0]
    acc = jnp.zeros_like(v)
    for d in range(Dh):
        S = s_ref[0, d]
        acc = acc + q[d:d + 1, :] * S
        so_ref[0, d] = S * gamma + k[d:d + 1, :] * v
    o = o + acc * gamma
    mu = jnp.mean(o, axis=0, keepdims=True)
    oc = o - mu
    var = jnp.mean(oc * oc, axis=0, keepdims=True)
    o_ref[0] = oc * lax.rsqrt(var + GN_EPS) * gn_ref[0] * sg_ref[0]


def _retention_sample(qT, kT, vT, sgT, gn_col, gamma, S):
    H, Dh, Bt = qT.shape
    vec = pl.BlockSpec((1, Dh, Bt), lambda h: (h, 0, 0))
    st = pl.BlockSpec((1, Dh, Dh, Bt), lambda h: (h, 0, 0, 0))
    return pl.pallas_call(
        _ret_sample_kernel,
        grid=(H,),
        in_specs=[vec, vec, vec, vec, pl.BlockSpec((1, Dh, 1), lambda h: (h, 0, 0)),
                  pl.BlockSpec(memory_space=pltpu.SMEM), st],
        out_specs=[vec, st],
        out_shape=[jax.ShapeDtypeStruct((H, Dh, Bt), F32), jax.ShapeDtypeStruct((H, Dh, Dh, Bt), F32)],
        compiler_params=_cparams("parallel"),
        name="retention_sample",
    )(qT, kT, vT, sgT, gn_col, gamma, S)


def _attn_sample_kernel(q_ref, kn_ref, vn_ref, knT_ref, vnT_ref, kb_ref, vb_ref, o_ref, ko_ref, vo_ref):
    b = pl.program_id(0)
    H, Dh, Wn = kb_ref.shape[1], kb_ref.shape[2], kb_ref.shape[3]
    HD = H * Dh
    q = q_ref[0]
    kn = kn_ref[0]
    vn = vn_ref[0]
    row = lax.broadcasted_iota(jnp.int32, (H, HD), 0)
    col = lax.broadcasted_iota(jnp.int32, (H, HD), 1)
    own = (col // Dh) == row
    q_bd = jnp.where(own, q, 0.0)
    s_new = jnp.sum(q_bd * kn, axis=1, keepdims=True)

    bl = lax.broadcasted_iota(jnp.int32, knT_ref.shape, 1)
    k_col = jnp.sum(jnp.where(bl == b, knT_ref[...], 0.0), axis=1, keepdims=True)
    v_col = jnp.sum(jnp.where(bl == b, vnT_ref[...], 0.0), axis=1, keepdims=True)

    t = lax.broadcasted_iota(jnp.int32, (1, Wn), 1)
    last = t == (Wn - 1)
    q_bf = q_bd.astype(BF16)
    outs = []
    for h in range(H):
        K = kb_ref[0, h]
        V = vb_ref[0, h]
        hs = slice(h * Dh, (h + 1) * Dh)
        ko_ref[0, h] = jnp.where(last, k_col[hs], pltpu.roll(K, Wn - 1, 1))
        vo_ref[0, h] = jnp.where(last, v_col[hs], pltpu.roll(V, Wn - 1, 1))
        s = _dot(q_bf[:, hs], K.astype(BF16))[h:h + 1]
        sn = s_new[h:h + 1]
        Vb = V.astype(BF16)
        vn_h = vn[:, hs]
        o_p, lse_p = [], []
        for window, d in DILATED_PATTERNS:
            valid = (((Wn - t) % d) == 0) & (t >= Wn - window)
            sm = jnp.where(valid, s, NEG_INF)
            m = jnp.maximum(jnp.max(sm, axis=1, keepdims=True), sn)
            e = jnp.exp(sm - m)
            en = jnp.exp(sn - m)
            l = jnp.sum(e, axis=1, keepdims=True) + en
            pe = jnp.broadcast_to((e / l).astype(BF16), (8, Wn))
            o_p.append(_dot_nt(pe, Vb)[0:1] + (en / l) * vn_h)
            lse_p.append(m + jnp.log(l))
        mm = jnp.maximum(jnp.maximum(lse_p[0], lse_p[1]), lse_p[2])
        ws = [jnp.exp(x - mm) for x in lse_p]
        tot = ws[0] + ws[1] + ws[2]
        outs.append((ws[0] * o_p[0] + ws[1] * o_p[1] + ws[2] * o_p[2]) / tot)
    o_ref[0] = jnp.concatenate(outs, axis=1)


def _attention_sample(q, kn, vn, knT, vnT, k_buf, v_buf):
    Bt, H, Dh, Wn = k_buf.shape
    HD = H * Dh
    vec = pl.BlockSpec((1, 1, HD), lambda b: (b, 0, 0))
    tr = pl.BlockSpec((HD, Bt), lambda b: (0, 0))
    cache = pl.BlockSpec((1, H, Dh, Wn), lambda b: (b, 0, 0, 0))
    return pl.pallas_call(
        _attn_sample_kernel,
        grid=(Bt,),
        in_specs=[vec, vec, vec, tr, tr, cache, cache],
        out_specs=[vec, cache, cache],
        out_shape=[jax.ShapeDtypeStruct((Bt, 1, HD), F32),
                   jax.ShapeDtypeStruct(k_buf.shape, F32), jax.ShapeDtypeStruct(v_buf.shape, F32)],
        compiler_params=_cparams("parallel"),
        name="attention_sample",
    )(q, kn, vn, knT, vnT, k_buf, v_buf)


def _rope_tables(pos):
    half = HEAD_DIM // 2
    inv = ROPE_THETA ** (-jnp.arange(half, dtype=F32) / half)
    ang = pos.astype(F32)[:, None] * inv[None, :]
    cos, sin = jnp.cos(ang), jnp.sin(ang)
    cos_t = jnp.tile(cos, (1, 2 * N_RET_HEADS))
    sin_t = jnp.tile(jnp.concatenate([-sin, sin], axis=1), (1, N_RET_HEADS))
    return cos_t, sin_t


def _layer(x_prompt, x_sample, c_prompt, c_sample, state_ret, cache_k, cache_v,
           norm1, w_ada, b_ada, w_in, gn_ret, q_norm, k_norm, w_out, norm2,
           w_router, router_bias, w_exp_gate, w_exp_up, w_exp_down, w_sh_gate, w_sh_up, w_sh_down):
    B, L, D = x_prompt.shape
    Bs = x_sample.shape[0]
    H, Dh, W = N_RET_HEADS, HEAD_DIM, RET_WIDTH
    Np = B * L

    pad = (-(B + Bs)) % 8
    c_all = jnp.concatenate([c_prompt, c_sample, jnp.zeros((pad, D), F32)], axis=0)
    mod = _modulation(c_all, w_ada, b_ada)
    mod_p = [m.reshape(B, 1, D) for m in jnp.split(mod[:B], 6, axis=-1)]
    mod_s = [m.reshape(1, Bs, D) for m in jnp.split(mod[B:B + Bs], 6, axis=-1)]

    w_in_bf = w_in.astype(BF16)
    gi = jnp.arange(W) // Dh
    gmat = jnp.where(gi[:, None] == gi[None, :], 1.0 / Dh, 0.0).astype(BF16)
    qn_t = jnp.tile(q_norm, N_ATT_HEADS).reshape(1, W)
    kn_t = jnp.tile(k_norm, N_ATT_HEADS).reshape(1, W)
    n1 = norm1.reshape(1, D)
    n2 = norm2.reshape(1, D)
    gn = gn_ret.reshape(1, W)

    tm = TOKEN_TILE
    cos_p, sin_p = _rope_tables(jnp.arange(L, dtype=jnp.int32))
    cos_s, sin_s = _rope_tables(PAST_LEN + jnp.arange(1, dtype=jnp.int32))

    xp = x_prompt.reshape(Np, D)
    xs = x_sample.reshape(Bs, D)
    proj_p = _in_proj(xp, mod_p[1], mod_p[0], cos_p, sin_p, n1, qn_t, kn_t, gmat, w_in_bf, tm, L // tm, L // tm)
    proj_s = _in_proj(xs, mod_s[1], mod_s[0], cos_s, sin_s, n1, qn_t, kn_t, gmat, w_in_bf, Bs, 1, 1)
    qr, kr, vr, sg, qa, ka, va = [t.reshape(B, L, W) for t in proj_p]
    qr_s, kr_s, vr_s, sg_s, qa_s, ka_s, va_s = proj_s

    o_r, state_p = _retention_prompt(qr, kr, vr, sg, gn)
    o_a = _attention_prompt(qa, ka, va)
    keep = min(MAX_WINDOW, L)
    cache_kp = ka[:, L - keep:].reshape(B, keep, N_ATT_HEADS, Dh)
    cache_vp = va[:, L - keep:].reshape(B, keep, N_ATT_HEADS, Dh)

    log_g = jnp.log1p(-(2.0 ** (-5.0 - jnp.arange(H, dtype=F32))))
    gamma = jnp.exp(log_g).astype(F32)
    tr = lambda t: t.T.reshape(H, Dh, Bs)
    S_t = jnp.transpose(state_ret, (1, 2, 3, 0))
    orT, S_new_t = _retention_sample(tr(qr_s), tr(kr_s), tr(vr_s), tr(sg_s), gn_ret.reshape(H, Dh, 1), gamma, S_t)
    o_r_s = orT.reshape(W, Bs).T
    state_s = jnp.transpose(S_new_t, (3, 0, 1, 2))

    kb_t = jnp.transpose(cache_k, (0, 2, 3, 1))
    vb_t = jnp.transpose(cache_v, (0, 2, 3, 1))
    o_a_s, ko_t, vo_t = _attention_sample(qa_s.reshape(Bs, 1, W), ka_s.reshape(Bs, 1, W), va_s.reshape(Bs, 1, W),
                                          ka_s.T, va_s.T, kb_t, vb_t)
    cache_ks = jnp.transpose(ko_t, (0, 3, 1, 2))
    cache_vs = jnp.transpose(vo_t, (0, 3, 1, 2))

    wo = w_out.astype(BF16)
    wr_t = w_router.T
    wrh = wr_t.astype(BF16)
    wrl = (wr_t - wrh.astype(F32)).astype(BF16)
    rbias = router_bias.astype(F32).reshape(N_EXPERTS, 1)
    wsg, wsu, wsd = w_sh_gate.astype(BF16), w_sh_up.astype(BF16), w_sh_down.astype(BF16)
    zero_counts = jnp.zeros((N_EXPERTS, 1), F32)
    base_p, h2_p, eidx_p, w_p, rank_p, counts_p = _post_mixer(
        o_r.reshape(Np, W), o_a.reshape(Np, W), xp, mod_p[2], mod_p[4], mod_p[3], mod_p[5], n2, wo, wrh, wrl, rbias,
        zero_counts, wsg, wsu, wsd, tm, L // tm)
    base_s, h2_s, eidx_s, w_s, rank_s, counts = _post_mixer(
        o_r_s, o_a_s.reshape(Bs, W), xs, mod_s[2], mod_s[4], mod_s[3], mod_s[5], n2, wo, wrh, wrl, rbias,
        counts_p, wsg, wsu, wsd, Bs, 1)

    T = EXPERT_TILE
    n_tiles = ((Np + Bs) * TOP_K + N_EXPERTS * T + T - 1) // T
    pad_off, blk_exp, first, n_used = _tile_plan(counts.reshape(N_EXPERTS).astype(jnp.int32), n_tiles)
    pad_off = pad_off.astype(F32).reshape(N_EXPERTS, 1)
    dest_p = _dest_rows(eidx_p, rank_p, pad_off)
    dest_s = _dest_rows(eidx_s, rank_s, pad_off)
    grouped = jnp.zeros((n_tiles * T, D // 2), jnp.uint32)
    grouped = _dispatch(dest_p, h2_p, grouped, tm)
    grouped = _dispatch(dest_s, h2_s, grouped, Bs)
    ys = _expert_matmul(grouped, blk_exp, first, n_used, w_exp_gate, w_exp_up, w_exp_down)
    tc = COMBINE_TILE
    y_p = _combine(dest_p, base_p, w_p.T, mod_p[5], ys, tc, L // tc)
    y_s = _combine(dest_s, base_s, w_s.T, mod_s[5], ys, Bs, 1)

    return (y_p.reshape(B, L, D), y_s.reshape(Bs, 1, D), state_p, cache_kp, cache_vp, state_s, cache_ks, cache_vs)


def kernel(x_prompt, x_sample, c_prompt, c_sample, state_ret, cache_win_k, cache_win_v, norm1, w_ada, b_ada, w_in,
           gn_ret, q_norm, k_norm, w_out, norm2, w_router, router_bias, w_exp_gate, w_exp_up, w_exp_down,
           w_sh_gate, w_sh_up, w_sh_down):
    assert w_in.shape[0] == 1, "single-layer step"
    assert x_sample.shape[1] == 1, "one new token per sample sequence"
    outs = _layer(x_prompt, x_sample, c_prompt, c_sample, state_ret[0], cache_win_k[0], cache_win_v[0],
                  norm1[0], w_ada[0], b_ada[0], w_in[0], gn_ret[0], q_norm[0], k_norm[0], w_out[0], norm2[0],
                  w_router[0], router_bias[0], w_exp_gate[0], w_exp_up[0], w_exp_down[0],
                  w_sh_gate[0], w_sh_up[0], w_sh_down[0])
    yp, ys, sp, kp, vp, ss, ksm, vsm = outs
    return (yp, ys, sp[None], kp[None], vp[None], ss[None], ksm[None], vsm[None])
```

```python
import functools

import jax
import jax.numpy as jnp
from jax import lax
from jax.experimental import pallas as pl
from jax.experimental.pallas import tpu as pltpu

HEAD_DIM = 64
N_RET_HEADS = 8
N_ATT_HEADS = 8
RET_WIDTH = N_RET_HEADS * HEAD_DIM
ATT_WIDTH = N_ATT_HEADS * HEAD_DIM
RET_CHUNK = 128
DILATED_PATTERNS = ((128, 1), (512, 4), (2048, 16))
MAX_WINDOW = 2048
PAST_LEN = 8192
ROPE_THETA = 10000.0
N_EXPERTS = 256
TOP_K = 8
N_EXPERT_GROUPS = 8
TOPK_GROUPS = 4
ROUTED_SCALE = 2.5
NORM_EPS = 1e-6
GN_EPS = 1e-5
NEG_INF = -1e30

F32 = jnp.float32
BF16 = jnp.bfloat16

VMEM_LIMIT_BYTES = 56 * 1024 * 1024
EXPERT_TILE = 256
TOKEN_TILE = 512
COMBINE_TILE = 256
DMA_QUEUES = 2
DEST_TILE = 2048
ATTN_UNROLL = 8


def _cparams(*sem):
    return pltpu.CompilerParams(dimension_semantics=sem, vmem_limit_bytes=VMEM_LIMIT_BYTES)


def _dot(a, b):
    return jnp.dot(a, b, preferred_element_type=F32)


def _dot_nt(a, b):
    return lax.dot_general(a, b, (((1,), (1,)), ((), ())), preferred_element_type=F32)


def _dot_tn(a, b):
    return lax.dot_general(a, b, (((0,), (0,)), ((), ())), preferred_element_type=F32)


def _mod_kernel(c_ref, w_ref, b_ref, o_ref):
    c = c_ref[...]
    a = (c * jax.nn.sigmoid(c)).astype(BF16)
    o_ref[...] = _dot(a, w_ref[...].astype(BF16)) + b_ref[...]


def _modulation(c, w_ada, b_ada):
    R, D = c.shape
    n_out = w_ada.shape[1]
    tn = 1024
    return pl.pallas_call(
        _mod_kernel,
        grid=(n_out // tn,),
        in_specs=[
            pl.BlockSpec((R, D), lambda j: (0, 0)),
            pl.BlockSpec((D, tn), lambda j: (0, j)),
            pl.BlockSpec((1, tn), lambda j: (0, j)),
        ],
        out_specs=pl.BlockSpec((R, tn), lambda j: (0, j)),
        out_shape=jax.ShapeDtypeStruct((R, n_out), F32),
        compiler_params=_cparams("parallel"),
        name="modulation",
    )(c, w_ada, b_ada.reshape(1, n_out))


def _swap_halves(x, first_half):
    n = x.shape[-1]
    return jnp.where(first_half, pltpu.roll(x, n - HEAD_DIM // 2, 1), pltpu.roll(x, HEAD_DIM // 2, 1))


def _head_mean(x, g):
    hi = x.astype(BF16)
    lo = (x - hi.astype(F32)).astype(BF16)
    return _dot(hi, g) + _dot(lo, g)


def _in_proj_kernel(x_ref, sc_ref, sh_ref, cos_ref, sin_ref, n1_ref, qn_ref, kn_ref, g_ref, w_ref,
                    qr_ref, kr_ref, vr_ref, sg_ref, qa_ref, ka_ref, va_ref):
    x = x_ref[...]
    ms = jnp.mean(x * x, axis=-1, keepdims=True)
    h = x * lax.rsqrt(ms + NORM_EPS) * n1_ref[...]
    h = (h * (1.0 + sc_ref[0]) + sh_ref[0]).astype(BF16)
    cos = cos_ref[...]
    sin = sin_ref[...]
    W = RET_WIDTH
    lane = lax.broadcasted_iota(jnp.int32, (1, W), 1)
    first_half = (lane % HEAD_DIM) < (HEAD_DIM // 2)

    def proj(c):
        return _dot(h, w_ref[:, c * W:(c + 1) * W])

    def rot(t):
        return t * cos + _swap_halves(t, first_half) * sin

    def qk_norm(t, gain):
        return t * lax.rsqrt(_head_mean(t * t, g_ref[...]) + NORM_EPS) * gain

    qr_ref[...] = rot(proj(0))
    kr_ref[...] = rot(proj(1)) * (HEAD_DIM ** -0.5)
    vr_ref[...] = proj(2)
    gr = proj(3)
    sg_ref[...] = gr * jax.nn.sigmoid(gr)
    qa_ref[...] = rot(qk_norm(proj(4), qn_ref[...])) * (HEAD_DIM ** -0.5)
    ka_ref[...] = rot(qk_norm(proj(5), kn_ref[...]))
    va_ref[...] = proj(6)


def _in_proj(x, sc, sh, cos, sin, norm1, q_norm_t, k_norm_t, gmat, w_in_bf, tm, rows_per_mod, pos_blocks):
    N, D = x.shape
    W = RET_WIDTH
    R = sc.shape[1]
    mod_spec = pl.BlockSpec((1, R, D), lambda i: (i // rows_per_mod, 0, 0))
    pos_spec = pl.BlockSpec((cos.shape[0] // pos_blocks, W), lambda i: (i % pos_blocks, 0))
    const = lambda shape: pl.BlockSpec(shape, lambda i: (0,) * len(shape))
    out_spec = pl.BlockSpec((tm, W), lambda i: (i, 0))
    return pl.pallas_call(
        _in_proj_kernel,
        grid=(N // tm,),
        in_specs=[
            pl.BlockSpec((tm, D), lambda i: (i, 0)),
            mod_spec, mod_spec, pos_spec, pos_spec,
            const((1, D)), const((1, W)), const((1, W)), const((W, W)), const(w_in_bf.shape),
        ],
        out_specs=[out_spec] * 7,
        out_shape=[jax.ShapeDtypeStruct((N, W), F32)] * 7,
        compiler_params=_cparams("parallel"),
        name="in_proj",
    )(x, sc, sh, cos, sin, norm1, q_norm_t, k_norm_t, gmat, w_in_bf)


def _ret_kernel(q_ref, k_ref, v_ref, sg_ref, dm_ref, qd_ref, kd_ref, sd_ref, gn_ref, o_ref, s_ref):
    @pl.when(pl.program_id(1) == 0)
    def _():
        s_ref[...] = jnp.zeros_like(s_ref)

    q = q_ref[0]
    k = k_ref[0]
    v = v_ref[0]
    kd = k * kd_ref[...]
    qd = qd_ref[...]
    outs = []
    for h in range(N_RET_HEADS):
        sl = slice(HEAD_DIM * h, HEAD_DIM * (h + 1))
        qh = q[:, sl].astype(BF16)
        kh = k[:, sl].astype(BF16)
        vh = v[:, sl].astype(BF16)
        S = s_ref[0, h]
        sc = _dot_nt(qh, kh) * dm_ref[h]
        o = _dot(sc.astype(BF16), vh) + _dot(qh, S.astype(BF16)) * qd[:, sl]
        s_ref[0, h] = S * sd_ref[h] + _dot_tn(kd[:, sl].astype(BF16), vh)
        mu = jnp.mean(o, axis=-1, keepdims=True)
        oc = o - mu
        var = jnp.mean(oc * oc, axis=-1, keepdims=True)
        outs.append(oc * lax.rsqrt(var + GN_EPS))
    o_ref[0] = jnp.concatenate(outs, axis=1) * gn_ref[...] * sg_ref[0]


def _decay_tables(C):
    H = N_RET_HEADS
    log_g = jnp.log1p(-(2.0 ** (-5.0 - jnp.arange(H, dtype=F32))))
    i = jnp.arange(C)
    diff = i[:, None] - i[None, :]
    dmask = jnp.where(diff[None] >= 0, jnp.exp(log_g[:, None, None] * jnp.maximum(diff, 0)[None]), 0.0).astype(F32)
    q_decay = jnp.exp(log_g[None, :] * (i[:, None] + 1)).astype(F32)
    k_decay = jnp.exp(log_g[None, :] * (C - 1 - i)[:, None]).astype(F32)
    s_decay = jnp.exp(log_g * C).astype(F32)
    return dmask, q_decay, k_decay, s_decay


def _retention_prompt(q, k, v, sg, gn):
    B, L, W = q.shape
    C = RET_CHUNK
    H, Dh = N_RET_HEADS, HEAD_DIM
    dmask, q_decay, k_decay, s_decay = _decay_tables(C)
    qd = jnp.repeat(q_decay, Dh, axis=1)
    kd = jnp.repeat(k_decay, Dh, axis=1)
    sd = jnp.broadcast_to(s_decay[:, None, None], (H, Dh, Dh))
    blk = pl.BlockSpec((1, C, W), lambda b, c: (b, c, 0))
    const = lambda shape: pl.BlockSpec(shape, lambda b, c: (0,) * len(shape))
    return pl.pallas_call(
        _ret_kernel,
        grid=(B, L // C),
        in_specs=[blk, blk, blk, blk, const((H, C, C)), const((C, W)), const((C, W)), const((H, Dh, Dh)), const((1, W))],
        out_specs=[blk, pl.BlockSpec((1, H, Dh, Dh), lambda b, c: (b, 0, 0, 0))],
        out_shape=[jax.ShapeDtypeStruct((B, L, W), F32), jax.ShapeDtypeStruct((B, H, Dh, Dh), F32)],
        compiler_params=_cparams("parallel", "arbitrary"),
        name="retention_prompt",
    )(q, k, v, sg, dmask, qd, kd, sd, gn)


def _attn_kernel(q_ref, k_ref, v_ref, o_ref, oacc, lacc):
    L = q_ref.shape[1]
    P2 = q_ref.shape[2]
    lane = lax.broadcasted_iota(jnp.int32, (1, P2), 1)
    head0 = lane < HEAD_DIM

    for p, (window, d) in enumerate(DILATED_PATTERNS):
        band = window // d
        nb = L // (d * band)
        qi = lax.broadcasted_iota(jnp.int32, (2 * band, 2 * band), 0) % band
        ki = lax.broadcasted_iota(jnp.int32, (2 * band, 2 * band), 1)
        dist = qi + band - ki
        in_band = (dist >= 0) & (dist <= band)
        bias_any = jnp.where(in_band, 0.0, NEG_INF)
        bias_first = jnp.where(in_band & (ki >= band), 0.0, NEG_INF)

        def block(idx, d=d, band=band, nb=nb, p=p, bias_any=bias_any, bias_first=bias_first):
            r = idx // nb
            n = idx % nb
            qs = r + d * band * n
            ps = jnp.maximum(qs - d * band, r)
            cur = pl.ds(qs, band, stride=d) if d > 1 else pl.ds(qs, band)
            prev = pl.ds(ps, band, stride=d) if d > 1 else pl.ds(ps, band)
            qb = q_ref[0, cur, :]
            kk = jnp.concatenate([k_ref[0, prev, :], k_ref[0, cur, :]], axis=0).astype(BF16)
            vv = jnp.concatenate([v_ref[0, prev, :], v_ref[0, cur, :]], axis=0).astype(BF16)
            q2 = jnp.concatenate([jnp.where(head0, qb, 0.0), jnp.where(head0, 0.0, qb)], axis=0).astype(BF16)
            s = _dot_nt(q2, kk) + jnp.where(n > 0, bias_any, bias_first)
            m = jnp.max(jnp.maximum(s[:, :band], s[:, band:]), axis=-1, keepdims=True)
            e = jnp.exp(s - m)
            l = jnp.sum(e[:, :band] + e[:, band:], axis=-1, keepdims=True)
            o = _dot(e.astype(BF16), vv) * (1.0 / l)
            lse = m + jnp.log(l)
            oacc[p, cur, :] = jnp.where(head0, o[:band], o[band:])
            lacc[p, cur, :] = jnp.where(head0, lse[:band], lse[band:])

        def body(i, carry, block=block):
            for u in range(ATTN_UNROLL):
                block(i * ATTN_UNROLL + u)
            return carry

        lax.fori_loop(0, d * nb // ATTN_UNROLL, body, 0)

    rows = 512

    def combine(i, carry):
        sl = pl.ds(pl.multiple_of(i * rows, rows), rows)
        l0, l1, l2 = lacc[0, sl, :], lacc[1, sl, :], lacc[2, sl, :]
        m = jnp.maximum(jnp.maximum(l0, l1), l2)
        w0, w1, w2 = jnp.exp(l0 - m), jnp.exp(l1 - m), jnp.exp(l2 - m)
        tot = w0 + w1 + w2
        o_ref[0, sl, :] = (w0 * oacc[0, sl, :] + w1 * oacc[1, sl, :] + w2 * oacc[2, sl, :]) / tot
        return carry

    lax.fori_loop(0, L // rows, combine, 0)


def _attention_prompt(q, k, v):
    B, L, W = q.shape
    P2 = 2 * HEAD_DIM
    blk = pl.BlockSpec((1, L, P2), lambda b, hp: (b, 0, hp))
    return pl.pallas_call(
        _attn_kernel,
        grid=(B, W // P2),
        in_specs=[blk, blk, blk],
        out_specs=blk,
        out_shape=jax.ShapeDtypeStruct((B, L, W), F32),
        scratch_shapes=[pltpu.VMEM((len(DILATED_PATTERNS), L, P2), F32)] * 2,
        compiler_params=_cparams("parallel", "parallel"),
        name="attention_prompt",
    )(q, k, v)


def _pack_bf16_pair(a, b):
    ua = pltpu.bitcast(a.astype(BF16).astype(F32), jnp.uint32)
    ub = pltpu.bitcast(b.astype(BF16).astype(F32), jnp.uint32)
    return ua | (ub >> 16)


def _unpack_bf16_pair(p):
    a = pltpu.bitcast(p & jnp.uint32(0xFFFF0000), F32)
    b = pltpu.bitcast(p << 16, F32)
    return a, b


ROUTE_LANES = 128
LANES = 128


ROW_PIECES = 4


def _store_rows(ref, packed):
    n = packed.shape[0]
    for c in range(ROW_PIECES):
        ref[pl.ds(c, n, stride=ROW_PIECES), :] = packed[:, c * LANES:(c + 1) * LANES]


def _load_rows(ref):
    n = ref.shape[0] // ROW_PIECES
    return jnp.concatenate([ref[pl.ds(c, n, stride=ROW_PIECES), :] for c in range(ROW_PIECES)], axis=1)


def _token_rows(t):
    return pl.ds(pl.multiple_of(t * ROW_PIECES, ROW_PIECES), ROW_PIECES)


def _route_chunk(s, bias, carry):
    E, n = s.shape
    G, GS = N_EXPERT_GROUPS, E // N_EXPERT_GROUPS
    NEG = -jnp.inf
    choice = s + bias
    row = lax.broadcasted_iota(jnp.int32, (E, n), 0).astype(F32)
    lrow = row[:GS]
    gs_rows = []
    for g in range(G):
        c = choice[g * GS:(g + 1) * GS]
        m1 = jnp.max(c, axis=0, keepdims=True)
        i1 = jnp.min(jnp.where(c == m1, lrow, float(GS)), axis=0, keepdims=True)
        m2 = jnp.max(jnp.where(lrow == i1, NEG, c), axis=0, keepdims=True)
        gs_rows.append(m1 + m2)
    gs = jnp.concatenate(gs_rows, axis=0)
    grow = lax.broadcasted_iota(jnp.int32, (G, n), 0).astype(F32)
    gsel = jnp.zeros((G, n), F32)
    for _ in range(TOPK_GROUPS):
        gm = jnp.max(gs, axis=0, keepdims=True)
        gi = jnp.min(jnp.where(gs == gm, grow, float(G)), axis=0, keepdims=True)
        hit = grow == gi
        gsel = jnp.where(hit, 1.0, gsel)
        gs = jnp.where(hit, NEG, gs)
    emask = jnp.concatenate([jnp.broadcast_to(gsel[g:g + 1], (GS, n)) for g in range(G)], axis=0)
    masked = jnp.where(emask > 0.5, choice, NEG)
    ids, sks = [], []
    member = jnp.zeros((E, n), F32)
    for _ in range(TOP_K):
        mk = jnp.max(masked, axis=0, keepdims=True)
        ik = jnp.min(jnp.where(masked == mk, row, float(E)), axis=0, keepdims=True)
        sel = row == ik
        sks.append(jnp.sum(jnp.where(sel, s, 0.0), axis=0, keepdims=True))
        masked = jnp.where(sel, NEG, masked)
        member = jnp.where(sel, 1.0, member)
        ids.append(ik)
    eidx = jnp.concatenate(ids, axis=0)
    sk = jnp.concatenate(sks, axis=0)
    w = sk / jnp.sum(sk, axis=0, keepdims=True) * ROUTED_SCALE
    ti = lax.broadcasted_iota(jnp.int32, (n, n), 0)
    tj = lax.broadcasted_iota(jnp.int32, (n, n), 1)
    upper = jnp.where(ti <= tj, 1.0, 0.0).astype(BF16)
    incl = _dot(member.astype(BF16), upper)
    rank_dense = carry + incl - member
    ranks = [jnp.sum(jnp.where(row == ids[k], rank_dense, 0.0), axis=0, keepdims=True) for k in range(TOP_K)]
    return eidx, w, jnp.concatenate(ranks, axis=0), carry + incl[:, n - 1:n]


def _post_kernel(or_ref, oa_ref, x_ref, g1_ref, sc_ref, sh_ref, g2_ref, n2_ref, wo_ref, wrh_ref, wrl_ref, rb_ref,
                 cin_ref, wsg_ref, wsu_ref, wsd_ref, base_ref, h2_ref, eidx_ref, w_ref, rank_ref, cnt_ref):
    @pl.when(pl.program_id(0) == 0)
    def _():
        cnt_ref[...] = cin_ref[...]

    W = RET_WIDTH
    mix = _dot(or_ref[...].astype(BF16), wo_ref[:W, :]) + _dot(oa_ref[...].astype(BF16), wo_ref[W:, :])
    x2 = x_ref[...] + g1_ref[0] * mix
    ms = jnp.mean(x2 * x2, axis=-1, keepdims=True)
    h2 = x2 * lax.rsqrt(ms + NORM_EPS) * n2_ref[...]
    h2 = h2 * (1.0 + sc_ref[0]) + sh_ref[0]
    hb = h2.astype(BF16)
    hl = (h2 - hb.astype(F32)).astype(BF16)
    half = h2.shape[1] // 2
    _store_rows(h2_ref, _pack_bf16_pair(h2[:, :half], h2[:, half:]))
    g = _dot(hb, wsg_ref[...])
    u = _dot(hb, wsu_ref[...])
    shared = _dot((g * jax.nn.sigmoid(g) * u).astype(BF16), wsd_ref[...])
    base_ref[...] = x2 + g2_ref[0] * shared

    logits = _dot_nt(wrh_ref[...], hb) + (_dot_nt(wrh_ref[...], hl) + _dot_nt(wrl_ref[...], hb))
    s = jax.nn.sigmoid(logits)
    carry = cnt_ref[...]
    n = ROUTE_LANES
    for j in range(s.shape[1] // n):
        sl = slice(j * n, (j + 1) * n)
        eidx, w, rank, carry = _route_chunk(s[:, sl], rb_ref[...], carry)
        eidx_ref[:, sl] = eidx.astype(jnp.int32)
        w_ref[:, sl] = w
        rank_ref[:, sl] = rank.astype(jnp.int32)
    cnt_ref[...] = carry


def _post_mixer(o_r, o_a, x, g1, sc2, sh2, g2, norm2, wo, wrh_t, wrl_t, rbias, counts_in, wsg, wsu, wsd, tm,
                rows_per_mod):
    N, D = x.shape
    assert D // 2 == ROW_PIECES * LANES
    W = RET_WIDTH
    E = wrh_t.shape[0]
    R = g1.shape[1]
    mod_spec = pl.BlockSpec((1, R, D), lambda i: (i // rows_per_mod, 0, 0))
    const = lambda a: pl.BlockSpec(a.shape, lambda i: (0,) * a.ndim)
    tok = lambda rows: pl.BlockSpec((rows, tm), lambda i: (0, i))
    return pl.pallas_call(
        _post_kernel,
        grid=(N // tm,),
        in_specs=[
            pl.BlockSpec((tm, W), lambda i: (i, 0)),
            pl.BlockSpec((tm, W), lambda i: (i, 0)),
            pl.BlockSpec((tm, D), lambda i: (i, 0)),
            mod_spec, mod_spec, mod_spec, mod_spec,
            const(norm2), const(wo), const(wrh_t), const(wrl_t), const(rbias), const(counts_in),
            const(wsg), const(wsu), const(wsd),
        ],
        out_specs=[
            pl.BlockSpec((tm, D), lambda i: (i, 0)),
            pl.BlockSpec((tm * ROW_PIECES, LANES), lambda i: (i, 0)),
            tok(TOP_K), tok(TOP_K), tok(TOP_K),
            pl.BlockSpec((E, 1), lambda i: (0, 0)),
        ],
        out_shape=[
            jax.ShapeDtypeStruct((N, D), F32),
            jax.ShapeDtypeStruct((N * ROW_PIECES, LANES), jnp.uint32),
            jax.ShapeDtypeStruct((TOP_K, N), jnp.int32),
            jax.ShapeDtypeStruct((TOP_K, N), F32),
            jax.ShapeDtypeStruct((TOP_K, N), jnp.int32),
            jax.ShapeDtypeStruct((E, 1), F32),
        ],
        compiler_params=_cparams("arbitrary"),
        name="post_mixer",
    )(o_r, o_a, x, g1, sc2, sh2, g2, norm2, wo, wrh_t, wrl_t, rbias, counts_in, wsg, wsu, wsd)


def _expert_kernel(be_ref, first_ref, nused_ref, nxt_ref, slot_ref, x_ref, wg_hbm, wu_hbm, wd_hbm, y_ref,
                   wg_f, wu_f, wd_f, wg_s, wu_s, wd_s, sem):
    i = pl.program_id(0)
    is_first = first_ref[i] == 1
    used = i < nused_ref[0]

    def weight_copies(e, s):
        return (pltpu.make_async_copy(wg_hbm.at[e], wg_f.at[s], sem.at[s, 0]),
                pltpu.make_async_copy(wu_hbm.at[e], wu_f.at[s], sem.at[s, 1]),
                pltpu.make_async_copy(wd_hbm.at[e], wd_f.at[s], sem.at[s, 2]))

    def compute():
        xa, xb = _unpack_bf16_pair(_load_rows(x_ref))
        xa = xa.astype(BF16)
        xb = xb.astype(BF16)
        half = xa.shape[1]
        g = _dot(xa, wg_s[:half, :]) + _dot(xb, wg_s[half:, :])
        u = _dot(xa, wu_s[:half, :]) + _dot(xb, wu_s[half:, :])
        hmid = (g * jax.nn.sigmoid(g) * u).astype(BF16)
        _store_rows(y_ref, _pack_bf16_pair(_dot(hmid, wd_s[:, :half]), _dot(hmid, wd_s[:, half:])))

    @pl.when(i == 0)
    def _():
        for c in weight_copies(be_ref[0], 0):
            c.start()

    @pl.when(is_first)
    def _():
        s = slot_ref[i]
        for c in weight_copies(be_ref[i], s):
            c.wait()

        @pl.when(nxt_ref[i] >= 0)
        def _():
            for c in weight_copies(nxt_ref[i], 1 - s):
                c.start()

        wg_s[...] = wg_f[s].astype(BF16)
        wu_s[...] = wu_f[s].astype(BF16)
        wd_s[...] = wd_f[s].astype(BF16)
        compute()

    @pl.when(jnp.logical_and(jnp.logical_not(is_first), used))
    def _():
        compute()

    @pl.when(jnp.logical_not(used))
    def _():
        y_ref[...] = jnp.zeros_like(y_ref)


def _expert_matmul(xs, plan, w_gate, w_up, w_down):
    P = xs.shape[0] // ROW_PIECES
    T = EXPERT_TILE
    D, F = w_gate.shape[1], w_gate.shape[2]
    row = pl.BlockSpec((T * ROW_PIECES, LANES), lambda i, *_: (i, 0))
    hbm = pl.BlockSpec(memory_space=pl.ANY)
    return pl.pallas_call(
        _expert_kernel,
        grid_spec=pltpu.PrefetchScalarGridSpec(
            num_scalar_prefetch=5,
            grid=(P // T,),
            in_specs=[row, hbm, hbm, hbm],
            out_specs=row,
            scratch_shapes=[
                pltpu.VMEM((2, D, F), F32), pltpu.VMEM((2, D, F), F32), pltpu.VMEM((2, F, D), F32),
                pltpu.VMEM((D, F), BF16), pltpu.VMEM((D, F), BF16), pltpu.VMEM((F, D), BF16),
                pltpu.SemaphoreType.DMA((2, 3)),
            ],
        ),
        out_shape=jax.ShapeDtypeStruct(xs.shape, jnp.uint32),
        compiler_params=_cparams("arbitrary"),
        name="routed_experts",
    )(*plan, xs, w_gate, w_up, w_down)


def _dest_kernel(eidx_ref, rank_ref, off_ref, dest_ref):
    E = off_ref.shape[0]
    n = eidx_ref.shape[1]
    row = lax.broadcasted_iota(jnp.int32, (E, n), 0)
    off = off_ref[...]
    rows = []
    for k in range(eidx_ref.shape[0]):
        hit = row == eidx_ref[k:k + 1, :]
        rows.append(jnp.sum(jnp.where(hit, off, 0.0), axis=0, keepdims=True))
    dest_ref[...] = rank_ref[...] + jnp.concatenate(rows, axis=0).astype(jnp.int32)


def _dest_rows(eidx, rank, pad_off):
    K, N = eidx.shape
    tn = min(N, DEST_TILE)
    blk = pl.BlockSpec((K, tn), lambda i: (0, i))
    return pl.pallas_call(
        _dest_kernel,
        grid=(N // tn,),
        in_specs=[blk, blk, pl.BlockSpec(pad_off.shape, lambda i: (0, 0))],
        out_specs=blk,
        out_shape=jax.ShapeDtypeStruct((K, N), jnp.int32),
        compiler_params=_cparams("parallel"),
        name="dest_rows",
    )(eidx, rank, pad_off)


def _row_copy(src, dst, sem):
    return pltpu.make_async_copy(src, dst, sem)


def _dispatch_kernel(dest_ref, h_ref, xs_in_ref, xs_ref, sem):
    del xs_in_ref
    K, tm = dest_ref.shape

    def issue(t, carry):
        for k in range(K):
            _row_copy(h_ref.at[_token_rows(t)], xs_ref.at[_token_rows(dest_ref[k, t])], sem).start(
                priority=k % DMA_QUEUES)
        return carry

    lax.fori_loop(0, tm, issue, 0)
    for _ in range(K):
        _row_copy(h_ref, xs_ref.at[pl.ds(0, tm * ROW_PIECES)], sem).wait()


def _dispatch(dest, h2p, xs, tm):
    K, N = dest.shape
    return pl.pallas_call(
        _dispatch_kernel,
        grid=(N // tm,),
        in_specs=[
            pl.BlockSpec((K, tm), lambda i: (0, i), memory_space=pltpu.SMEM),
            pl.BlockSpec((tm * ROW_PIECES, LANES), lambda i: (i, 0)),
            pl.BlockSpec(memory_space=pl.ANY),
        ],
        out_specs=pl.BlockSpec(memory_space=pl.ANY),
        out_shape=jax.ShapeDtypeStruct(xs.shape, xs.dtype),
        scratch_shapes=[pltpu.SemaphoreType.DMA],
        input_output_aliases={2: 0},
        compiler_params=_cparams("arbitrary"),
        name="dispatch",
    )(dest, h2p, xs)


def _combine_kernel(dest_ref, base_ref, w_ref, g2_ref, ys_ref, o_ref, gbuf, sem):
    K, tm = dest_ref.shape

    def issue(t, carry):
        for k in range(K):
            _row_copy(ys_ref.at[_token_rows(dest_ref[k, t])], gbuf.at[k, _token_rows(t)], sem).start(
                priority=k % DMA_QUEUES)
        return carry

    lax.fori_loop(0, tm, issue, 0)
    for k in range(K):
        _row_copy(ys_ref.at[pl.ds(0, tm * ROW_PIECES)], gbuf.at[k], sem).wait()
    half = ROW_PIECES * LANES
    w = w_ref[...]
    acc_a = jnp.zeros((tm, half), F32)
    acc_b = jnp.zeros((tm, half), F32)
    for k in range(K):
        a, b = _unpack_bf16_pair(_load_rows(gbuf.at[k]))
        wk = w[:, k:k + 1]
        acc_a = acc_a + wk * a
        acc_b = acc_b + wk * b
    g2 = g2_ref[0]
    o_ref[:, :half] = base_ref[:, :half] + g2[:, :half] * acc_a
    o_ref[:, half:] = base_ref[:, half:] + g2[:, half:] * acc_b


def _combine(dest, base, w_tok, g2, ys, tm, rows_per_mod):
    K, N = dest.shape
    D = base.shape[1]
    R = g2.shape[1]
    return pl.pallas_call(
        _combine_kernel,
        grid=(N // tm,),
        in_specs=[
            pl.BlockSpec((K, tm), lambda i: (0, i), memory_space=pltpu.SMEM),
            pl.BlockSpec((tm, D), lambda i: (i, 0)),
            pl.BlockSpec((tm, K), lambda i: (i, 0)),
            pl.BlockSpec((1, R, D), lambda i: (i // rows_per_mod, 0, 0)),
            pl.BlockSpec(memory_space=pl.ANY),
        ],
        out_specs=pl.BlockSpec((tm, D), lambda i: (i, 0)),
        out_shape=jax.ShapeDtypeStruct((N, D), F32),
        scratch_shapes=[pltpu.VMEM((K, tm * ROW_PIECES, LANES), jnp.uint32), pltpu.SemaphoreType.DMA],
        compiler_params=_cparams("arbitrary"),
        name="combine",
    )(dest, base, w_tok, g2, ys)


def _tile_plan(counts, n_tiles):
    T = EXPERT_TILE
    E = counts.shape[0]
    padded = (counts + T - 1) // T * T
    pad_end = jnp.cumsum(padded)
    pad_off = pad_end - padded
    idx = jnp.arange(n_tiles, dtype=jnp.int32)
    tile_start = idx * T
    blk_exp = jnp.minimum(jnp.sum((pad_end[None, :] <= tile_start[:, None]).astype(jnp.int32), axis=1), E - 1)
    used = tile_start < pad_end[-1]
    changed = jnp.concatenate([jnp.ones((1,), bool), blk_exp[1:] != blk_exp[:-1]])
    first = jnp.logical_and(used, changed)
    next_first = lax.cummin(jnp.where(first, idx, n_tiles), reverse=True)
    next_first = jnp.concatenate([next_first[1:], jnp.full((1,), n_tiles, jnp.int32)])
    nxt = jnp.where(next_first < n_tiles, blk_exp[jnp.minimum(next_first, n_tiles - 1)], -1)
    slot = (jnp.cumsum(first.astype(jnp.int32)) - 1) % 2
    n_used = (pad_end[-1] // T).reshape(1)
    as_i32 = lambda a: a.astype(jnp.int32)
    return pad_off, (as_i32(blk_exp), as_i32(first), as_i32(n_used), as_i32(nxt), as_i32(slot))


def _ret_sample_kernel(q_ref, k_ref, v_ref, sg_ref, gn_ref, dec_ref, s_ref, o_ref, so_ref):
    q = q_ref[0]
    k = k_ref[0]
    v = v_ref[0]
    gamma = dec_ref[pl.program_id(0)]
    qk = jnp.sum(q * k, axis=0, keepdims=True)
    o = qk * v
    Dh = q.shape[0]
    acc = jnp.zeros_like(v)
    for d in range(Dh):
        S = s_ref[0, d]
        acc = acc + q[d:d + 1, :] * S
        so_ref[0, d] = S * gamma + k[d:d + 1, :] * v
    o = o + acc * gamma
    mu = jnp.mean(o, axis=0, keepdims=True)
    oc = o - mu
    var = jnp.mean(oc * oc, axis=0, keepdims=True)
    o_ref[0] = oc * lax.rsqrt(var + GN_EPS) * gn_ref[0] * sg_ref[0]


def _retention_sample(qT, kT, vT, sgT, gn_col, gamma, S):
    H, Dh, Bt = qT.shape
    vec = pl.BlockSpec((1, Dh, Bt), lambda h: (h, 0, 0))
    st = pl.BlockSpec((1, Dh, Dh, Bt), lambda h: (h, 0, 0, 0))
    return pl.pallas_call(
        _ret_sample_kernel,
        grid=(H,),
        in_specs=[vec, vec, vec, vec, pl.BlockSpec((1, Dh, 1), lambda h: (h, 0, 0)),
                  pl.BlockSpec(memory_space=pltpu.SMEM), st],
        out_specs=[vec, st],
        out_shape=[jax.ShapeDtypeStruct((H, Dh, Bt), F32), jax.ShapeDtypeStruct((H, Dh, Dh, Bt), F32)],
        compiler_params=_cparams("parallel"),
        name="retention_sample",
    )(qT, kT, vT, sgT, gn_col, gamma, S)


def _attn_sample_kernel(q_ref, kn_ref, vn_ref, knT_ref, vnT_ref, kb_ref, vb_ref, o_ref, ko_ref, vo_ref):
    b = pl.program_id(0)
    H, Dh, Wn = kb_ref.shape[1], kb_ref.shape[2], kb_ref.shape[3]
    HD = H * Dh
    q = q_ref[0]
    kn = kn_ref[0]
    vn = vn_ref[0]
    row = lax.broadcasted_iota(jnp.int32, (H, HD), 0)
    col = lax.broadcasted_iota(jnp.int32, (H, HD), 1)
    own = (col // Dh) == row
    q_bd = jnp.where(own, q, 0.0)
    s_new = jnp.sum(q_bd * kn, axis=1, keepdims=True)

    bl = lax.broadcasted_iota(jnp.int32, knT_ref.shape, 1)
    k_col = jnp.sum(jnp.where(bl == b, knT_ref[...], 0.0), axis=1, keepdims=True)
    v_col = jnp.sum(jnp.where(bl == b, vnT_ref[...], 0.0), axis=1, keepdims=True)

    t = lax.broadcasted_iota(jnp.int32, (1, Wn), 1)
    last = t == (Wn - 1)
    q_bf = q_bd.astype(BF16)
    outs = []
    for h in range(H):
        K = kb_ref[0, h]
        V = vb_ref[0, h]
        hs = slice(h * Dh, (h + 1) * Dh)
        ko_ref[0, h] = jnp.where(last, k_col[hs], pltpu.roll(K, Wn - 1, 1))
        vo_ref[0, h] = jnp.where(last, v_col[hs], pltpu.roll(V, Wn - 1, 1))
        s = _dot(q_bf[:, hs], K.astype(BF16))[h:h + 1]
        sn = s_new[h:h + 1]
        Vb = V.astype(BF16)
        vn_h = vn[:, hs]
        o_p, lse_p = [], []
        for window, d in DILATED_PATTERNS:
            valid = (((Wn - t) % d) == 0) & (t >= Wn - window)
            sm = jnp.where(valid, s, NEG_INF)
            m = jnp.maximum(jnp.max(sm, axis=1, keepdims=True), sn)
            e = jnp.exp(sm - m)
            en = jnp.exp(sn - m)
            l = jnp.sum(e, axis=1, keepdims=True) + en
            pe = jnp.broadcast_to((e / l).astype(BF16), (8, Wn))
            o_p.append(_dot_nt(pe, Vb)[0:1] + (en / l) * vn_h)
            lse_p.append(m + jnp.log(l))
        mm = jnp.maximum(jnp.maximum(lse_p[0], lse_p[1]), lse_p[2])
        ws = [jnp.exp(x - mm) for x in lse_p]
        tot = ws[0] + ws[1] + ws[2]
        outs.append((ws[0] * o_p[0] + ws[1] * o_p[1] + ws[2] * o_p[2]) / tot)
    o_ref[0] = jnp.concatenate(outs, axis=1)


def _attention_sample(q, kn, vn, knT, vnT, k_buf, v_buf):
    Bt, H, Dh, Wn = k_buf.shape
    HD = H * Dh
    vec = pl.BlockSpec((1, 1, HD), lambda b: (b, 0, 0))
    tr = pl.BlockSpec((HD, Bt), lambda b: (0, 0))
    cache = pl.BlockSpec((1, H, Dh, Wn), lambda b: (b, 0, 0, 0))
    return pl.pallas_call(
        _attn_sample_kernel,
        grid=(Bt,),
        in_specs=[vec, vec, vec, tr, tr, cache, cache],
        out_specs=[vec, cache, cache],
        out_shape=[jax.ShapeDtypeStruct((Bt, 1, HD), F32),
                   jax.ShapeDtypeStruct(k_buf.shape, F32), jax.ShapeDtypeStruct(v_buf.shape, F32)],
        compiler_params=_cparams("parallel"),
        name="attention_sample",
    )(q, kn, vn, knT, vnT, k_buf, v_buf)


def _rope_tables(pos):
    half = HEAD_DIM // 2
    inv = ROPE_THETA ** (-jnp.arange(half, dtype=F32) / half)
    ang = pos.astype(F32)[:, None] * inv[None, :]
    cos, sin = jnp.cos(ang), jnp.sin(ang)
    cos_t = jnp.tile(cos, (1, 2 * N_RET_HEADS))
    sin_t = jnp.tile(jnp.concatenate([-sin, sin], axis=1), (1, N_RET_HEADS))
    return cos_t, sin_t


def _layer(x_prompt, x_sample, c_prompt, c_sample, state_ret, cache_k, cache_v,
           norm1, w_ada, b_ada, w_in, gn_ret, q_norm, k_norm, w_out, norm2,
           w_router, router_bias, w_exp_gate, w_exp_up, w_exp_down, w_sh_gate, w_sh_up, w_sh_down):
    B, L, D = x_prompt.shape
    Bs = x_sample.shape[0]
    H, Dh, W = N_RET_HEADS, HEAD_DIM, RET_WIDTH
    Np = B * L

    pad = (-(B + Bs)) % 8
    c_all = jnp.concatenate([c_prompt, c_sample, jnp.zeros((pad, D), F32)], axis=0)
    mod = _modulation(c_all, w_ada, b_ada)
    mod_p = [m.reshape(B, 1, D) for m in jnp.split(mod[:B], 6, axis=-1)]
    mod_s = [m.reshape(1, Bs, D) for m in jnp.split(mod[B:B + Bs], 6, axis=-1)]

    w_in_bf = w_in.astype(BF16)
    gi = jnp.arange(W) // Dh
    gmat = jnp.where(gi[:, None] == gi[None, :], 1.0 / Dh, 0.0).astype(BF16)
    qn_t = jnp.tile(q_norm, N_ATT_HEADS).reshape(1, W)
    kn_t = jnp.tile(k_norm, N_ATT_HEADS).reshape(1, W)
    n1 = norm1.reshape(1, D)
    n2 = norm2.reshape(1, D)
    gn = gn_ret.reshape(1, W)

    tm = TOKEN_TILE
    cos_p, sin_p = _rope_tables(jnp.arange(L, dtype=jnp.int32))
    cos_s, sin_s = _rope_tables(PAST_LEN + jnp.arange(1, dtype=jnp.int32))

    xp = x_prompt.reshape(Np, D)
    xs = x_sample.reshape(Bs, D)
    proj_p = _in_proj(xp, mod_p[1], mod_p[0], cos_p, sin_p, n1, qn_t, kn_t, gmat, w_in_bf, tm, L // tm, L // tm)
    proj_s = _in_proj(xs, mod_s[1], mod_s[0], cos_s, sin_s, n1, qn_t, kn_t, gmat, w_in_bf, Bs, 1, 1)
    qr, kr, vr, sg, qa, ka, va = [t.reshape(B, L, W) for t in proj_p]
    qr_s, kr_s, vr_s, sg_s, qa_s, ka_s, va_s = proj_s

    o_r, state_p = _retention_prompt(qr, kr, vr, sg, gn)
    o_a = _attention_prompt(qa, ka, va)
    keep = min(MAX_WINDOW, L)
    cache_kp = ka[:, L - keep:].reshape(B, keep, N_ATT_HEADS, Dh)
    cache_vp = va[:, L - keep:].reshape(B, keep, N_ATT_HEADS, Dh)

    log_g = jnp.log1p(-(2.0 ** (-5.0 - jnp.arange(H, dtype=F32))))
    gamma = jnp.exp(log_g).astype(F32)
    tr = lambda t: t.T.reshape(H, Dh, Bs)
    S_t = jnp.transpose(state_ret, (1, 2, 3, 0))
    orT, S_new_t = _retention_sample(tr(qr_s), tr(kr_s), tr(vr_s), tr(sg_s), gn_ret.reshape(H, Dh, 1), gamma, S_t)
    o_r_s = orT.reshape(W, Bs).T
    state_s = jnp.transpose(S_new_t, (3, 0, 1, 2))

    kb_t = jnp.transpose(cache_k, (0, 2, 3, 1))
    vb_t = jnp.transpose(cache_v, (0, 2, 3, 1))
    o_a_s, ko_t, vo_t = _attention_sample(qa_s.reshape(Bs, 1, W), ka_s.reshape(Bs, 1, W), va_s.reshape(Bs, 1, W),
                                          ka_s.T, va_s.T, kb_t, vb_t)
    cache_ks = jnp.transpose(ko_t, (0, 3, 1, 2))
    cache_vs = jnp.transpose(vo_t, (0, 3, 1, 2))

    wo = w_out.astype(BF16)
    wr_t = w_router.T
    wrh = wr_t.astype(BF16)
    wrl = (wr_t - wrh.astype(F32)).astype(BF16)
    rbias = router_bias.astype(F32).reshape(N_EXPERTS, 1)
    wsg, wsu, wsd = w_sh_gate.astype(BF16), w_sh_up.astype(BF16), w_sh_down.astype(BF16)
    zero_counts = jnp.zeros((N_EXPERTS, 1), F32)
    base_p, h2_p, eidx_p, w_p, rank_p, counts_p = _post_mixer(
        o_r.reshape(Np, W), o_a.reshape(Np, W), xp, mod_p[2], mod_p[4], mod_p[3], mod_p[5], n2, wo, wrh, wrl, rbias,
        zero_counts, wsg, wsu, wsd, tm, L // tm)
    base_s, h2_s, eidx_s, w_s, rank_s, counts = _post_mixer(
        o_r_s, o_a_s.reshape(Bs, W), xs, mod_s[2], mod_s[4], mod_s[3], mod_s[5], n2, wo, wrh, wrl, rbias,
        counts_p, wsg, wsu, wsd, Bs, 1)

    T = EXPERT_TILE
    n_tiles = ((Np + Bs) * TOP_K + N_EXPERTS * T + T - 1) // T
    pad_off, plan = _tile_plan(counts.reshape(N_EXPERTS).astype(jnp.int32), n_tiles)
    pad_off = pad_off.astype(F32).reshape(N_EXPERTS, 1)
    dest_p = _dest_rows(eidx_p, rank_p, pad_off)
    dest_s = _dest_rows(eidx_s, rank_s, pad_off)
    grouped = jnp.zeros((n_tiles * T * ROW_PIECES, LANES), jnp.uint32)
    grouped = _dispatch(dest_p, h2_p, grouped, tm)
    grouped = _dispatch(dest_s, h2_s, grouped, Bs)
    ys = _expert_matmul(grouped, plan, w_exp_gate, w_exp_up, w_exp_down)
    tc = COMBINE_TILE
    y_p = _combine(dest_p, base_p, w_p.T, mod_p[5], ys, tc, L // tc)
    y_s = _combine(dest_s, base_s, w_s.T, mod_s[5], ys, Bs, 1)

    return (y_p.reshape(B, L, D), y_s.reshape(Bs, 1, D), state_p, cache_kp, cache_vp, state_s, cache_ks, cache_vs)


def kernel(x_prompt, x_sample, c_prompt, c_sample, state_ret, cache_win_k, cache_win_v, norm1, w_ada, b_ada, w_in,
           gn_ret, q_norm, k_norm, w_out, norm2, w_router, router_bias, w_exp_gate, w_exp_up, w_exp_down,
           w_sh_gate, w_sh_up, w_sh_down):
    assert w_in.shape[0] == 1, "single-layer step"
    assert x_sample.shape[1] == 1, "one new token per sample sequence"
    outs = _layer(x_prompt, x_sample, c_prompt, c_sample, state_ret[0], cache_win_k[0], cache_win_v[0],
                  norm1[0], w_ada[0], b_ada[0], w_in[0], gn_ret[0], q_norm[0], k_norm[0], w_out[0], norm2[0],
                  w_router[0], router_bias[0], w_exp_gate[0], w_exp_up[0], w_exp_down[0],
                  w_sh_gate[0], w_sh_up[0], w_sh_down[0])
    yp, ys, sp, kp, vp, ss, ksm, vsm = outs
    return (yp, ys, sp[None], kp[None], vp[None], ss[None], ksm[None], vsm[None])
```

```python
import functools

import jax
import jax.numpy as jnp
from jax import lax
from jax.experimental import pallas as pl
from jax.experimental.pallas import tpu as pltpu

HEAD_DIM = 64
N_RET_HEADS = 8
N_ATT_HEADS = 8
RET_WIDTH = N_RET_HEADS * HEAD_DIM
ATT_WIDTH = N_ATT_HEADS * HEAD_DIM
RET_CHUNK = 128
DILATED_PATTERNS = ((128, 1), (512, 4), (2048, 16))
MAX_WINDOW = 2048
PAST_LEN = 8192
ROPE_THETA = 10000.0
N_EXPERTS = 256
TOP_K = 8
N_EXPERT_GROUPS = 8
TOPK_GROUPS = 4
ROUTED_SCALE = 2.5
NORM_EPS = 1e-6
GN_EPS = 1e-5
NEG_INF = -1e30

F32 = jnp.float32
BF16 = jnp.bfloat16

VMEM_LIMIT_BYTES = 56 * 1024 * 1024
EXPERT_TILE = 256
TOKEN_TILE = 512
COMBINE_TILE = 256
DMA_QUEUES = 2
ISSUE_UNROLL = 4
RET_STEP_CHUNKS = 4
DEST_TILE = 2048
ATTN_UNROLL = 8


def _cparams(*sem):
    return pltpu.CompilerParams(dimension_semantics=sem, vmem_limit_bytes=VMEM_LIMIT_BYTES)


def _dot(a, b):
    return jnp.dot(a, b, preferred_element_type=F32)


def _dot_nt(a, b):
    return lax.dot_general(a, b, (((1,), (1,)), ((), ())), preferred_element_type=F32)


def _dot_tn(a, b):
    return lax.dot_general(a, b, (((0,), (0,)), ((), ())), preferred_element_type=F32)


def _mod_kernel(c_ref, w_ref, b_ref, o_ref):
    c = c_ref[...]
    a = (c * jax.nn.sigmoid(c)).astype(BF16)
    o_ref[...] = _dot(a, w_ref[...].astype(BF16)) + b_ref[...]


def _modulation(c, w_ada, b_ada):
    R, D = c.shape
    n_out = w_ada.shape[1]
    tn = 1024
    return pl.pallas_call(
        _mod_kernel,
        grid=(n_out // tn,),
        in_specs=[
            pl.BlockSpec((R, D), lambda j: (0, 0)),
            pl.BlockSpec((D, tn), lambda j: (0, j)),
            pl.BlockSpec((1, tn), lambda j: (0, j)),
        ],
        out_specs=pl.BlockSpec((R, tn), lambda j: (0, j)),
        out_shape=jax.ShapeDtypeStruct((R, n_out), F32),
        compiler_params=_cparams("parallel"),
        name="modulation",
    )(c, w_ada, b_ada.reshape(1, n_out))


def _swap_halves(x, first_half):
    n = x.shape[-1]
    return jnp.where(first_half, pltpu.roll(x, n - HEAD_DIM // 2, 1), pltpu.roll(x, HEAD_DIM // 2, 1))


def _head_mean(x, g):
    hi = x.astype(BF16)
    lo = (x - hi.astype(F32)).astype(BF16)
    return _dot(hi, g) + _dot(lo, g)


def _in_proj_kernel(x_ref, sc_ref, sh_ref, cos_ref, sin_ref, n1_ref, qn_ref, kn_ref, g_ref, w_ref,
                    qr_ref, kr_ref, vr_ref, sg_ref, qa_ref, ka_ref, va_ref):
    x = x_ref[...]
    ms = jnp.mean(x * x, axis=-1, keepdims=True)
    h = x * lax.rsqrt(ms + NORM_EPS) * n1_ref[...]
    h = (h * (1.0 + sc_ref[0]) + sh_ref[0]).astype(BF16)
    cos = cos_ref[...]
    sin = sin_ref[...]
    W = RET_WIDTH
    lane = lax.broadcasted_iota(jnp.int32, (1, W), 1)
    first_half = (lane % HEAD_DIM) < (HEAD_DIM // 2)

    def proj(c):
        return _dot(h, w_ref[:, c * W:(c + 1) * W])

    def rot(t):
        return t * cos + _swap_halves(t, first_half) * sin

    def qk_norm(t, gain):
        return t * lax.rsqrt(_head_mean(t * t, g_ref[...]) + NORM_EPS) * gain

    qr_ref[...] = rot(proj(0))
    kr_ref[...] = rot(proj(1)) * (HEAD_DIM ** -0.5)
    vr_ref[...] = proj(2)
    gr = proj(3)
    sg_ref[...] = gr * jax.nn.sigmoid(gr)
    qa_ref[...] = rot(qk_norm(proj(4), qn_ref[...])) * (HEAD_DIM ** -0.5)
    ka_ref[...] = rot(qk_norm(proj(5), kn_ref[...]))
    va_ref[...] = proj(6)


def _in_proj(x, sc, sh, cos, sin, norm1, q_norm_t, k_norm_t, gmat, w_in_bf, tm, rows_per_mod, pos_blocks):
    N, D = x.shape
    W = RET_WIDTH
    R = sc.shape[1]
    mod_spec = pl.BlockSpec((1, R, D), lambda i: (i // rows_per_mod, 0, 0))
    pos_spec = pl.BlockSpec((cos.shape[0] // pos_blocks, W), lambda i: (i % pos_blocks, 0))
    const = lambda shape: pl.BlockSpec(shape, lambda i: (0,) * len(shape))
    out_spec = pl.BlockSpec((tm, W), lambda i: (i, 0))
    return pl.pallas_call(
        _in_proj_kernel,
        grid=(N // tm,),
        in_specs=[
            pl.BlockSpec((tm, D), lambda i: (i, 0)),
            mod_spec, mod_spec, pos_spec, pos_spec,
            const((1, D)), const((1, W)), const((1, W)), const((W, W)), const(w_in_bf.shape),
        ],
        out_specs=[out_spec] * 7,
        out_shape=[jax.ShapeDtypeStruct((N, W), F32)] * 7,
        compiler_params=_cparams("parallel"),
        name="in_proj",
    )(x, sc, sh, cos, sin, norm1, q_norm_t, k_norm_t, gmat, w_in_bf)


def _ret_kernel(q_ref, k_ref, v_ref, sg_ref, dm_ref, qd_ref, kd_ref, sd_ref, bd_ref, gm_ref, gn_ref, o_ref, s_ref):
    @pl.when(pl.program_id(1) == 0)
    def _():
        s_ref[...] = jnp.zeros_like(s_ref)

    C = dm_ref.shape[2]
    Dh = HEAD_DIM
    P2 = 2 * Dh
    n_pairs = N_RET_HEADS // 2
    lane = lax.broadcasted_iota(jnp.int32, (1, P2), 1)
    head0 = lane < Dh
    zero = jnp.zeros((Dh, Dh), F32)
    state = []
    for p in range(n_pairs):
        top = jnp.concatenate([s_ref[0, 2 * p], zero], axis=1)
        bot = jnp.concatenate([zero, s_ref[0, 2 * p + 1]], axis=1)
        state.append(jnp.concatenate([top, bot], axis=0))
    bd = bd_ref[...]
    gm = gm_ref[...]
    for j in range(q_ref.shape[1] // C):
        rows = slice(j * C, (j + 1) * C)
        for p in range(n_pairs):
            sl = slice(P2 * p, P2 * (p + 1))
            q = q_ref[0, rows, sl]
            k = k_ref[0, rows, sl]
            vb = v_ref[0, rows, sl].astype(BF16)
            kdb = (k * kd_ref[:, sl]).astype(BF16)
            q2 = jnp.concatenate([jnp.where(head0, q, 0.0), jnp.where(head0, 0.0, q)], axis=0).astype(BF16)
            sc = _dot_nt(q2, k.astype(BF16)) * dm_ref[p]
            o2 = _dot(sc.astype(BF16), vb)
            S = state[p]
            o = jnp.where(head0, o2[:C], o2[C:]) + _dot(q.astype(BF16), S.astype(BF16)) * qd_ref[:, sl]
            state[p] = S * sd_ref[p] + _dot_tn(kdb, vb) * bd
            o_ref[0, rows, sl] = o
    for p in range(n_pairs):
        s_ref[0, 2 * p] = state[p][:Dh, :Dh]
        s_ref[0, 2 * p + 1] = state[p][Dh:, Dh:]
    o = o_ref[0]
    oc = o - _head_mean(o, gm)
    var = _head_mean(oc * oc, gm)
    o_ref[0] = oc * lax.rsqrt(var + GN_EPS) * gn_ref[...] * sg_ref[0]


def _decay_tables(C):
    H = N_RET_HEADS
    log_g = jnp.log1p(-(2.0 ** (-5.0 - jnp.arange(H, dtype=F32))))
    i = jnp.arange(C)
    diff = i[:, None] - i[None, :]
    dmask = jnp.where(diff[None] >= 0, jnp.exp(log_g[:, None, None] * jnp.maximum(diff, 0)[None]), 0.0).astype(F32)
    q_decay = jnp.exp(log_g[None, :] * (i[:, None] + 1)).astype(F32)
    k_decay = jnp.exp(log_g[None, :] * (C - 1 - i)[:, None]).astype(F32)
    s_decay = jnp.exp(log_g * C).astype(F32)
    return dmask, q_decay, k_decay, s_decay


def _retention_prompt(q, k, v, sg, gn, gm):
    B, L, W = q.shape
    C = RET_CHUNK
    H, Dh = N_RET_HEADS, HEAD_DIM
    dmask, q_decay, k_decay, s_decay = _decay_tables(C)
    qd = jnp.repeat(q_decay, Dh, axis=1)
    kd = jnp.repeat(k_decay, Dh, axis=1)
    P2 = 2 * Dh
    pair_head = jnp.arange(P2) // Dh
    bd = (pair_head[:, None] == pair_head[None, :]).astype(F32)
    sd = s_decay.reshape(H // 2, 2)[:, pair_head][:, :, None] * bd[None]
    rows = C * RET_STEP_CHUNKS
    blk = pl.BlockSpec((1, rows, W), lambda b, c: (b, c, 0))
    const = lambda shape: pl.BlockSpec(shape, lambda b, c: (0,) * len(shape))
    return pl.pallas_call(
        _ret_kernel,
        grid=(B, L // rows),
        in_specs=[blk, blk, blk, blk, const((H // 2, 2 * C, C)), const((C, W)), const((C, W)),
                  const((H // 2, P2, P2)), const((P2, P2)), const((W, W)), const((1, W))],
        out_specs=[blk, pl.BlockSpec((1, H, Dh, Dh), lambda b, c: (b, 0, 0, 0))],
        out_shape=[jax.ShapeDtypeStruct((B, L, W), F32), jax.ShapeDtypeStruct((B, H, Dh, Dh), F32)],
        compiler_params=_cparams("parallel", "arbitrary"),
        name="retention_prompt",
    )(q, k, v, sg, dmask.reshape(H // 2, 2 * C, C), qd, kd, sd, bd, gm, gn)


def _attn_kernel(q_ref, k_ref, v_ref, o_ref, oacc, lacc):
    L = q_ref.shape[1]
    P2 = q_ref.shape[2]
    lane = lax.broadcasted_iota(jnp.int32, (1, P2), 1)
    head0 = lane < HEAD_DIM

    for p, (window, d) in enumerate(DILATED_PATTERNS):
        band = window // d
        nb = L // (d * band)
        qi = lax.broadcasted_iota(jnp.int32, (2 * band, 2 * band), 0) % band
        ki = lax.broadcasted_iota(jnp.int32, (2 * band, 2 * band), 1)
        dist = qi + band - ki
        in_band = (dist >= 0) & (dist <= band)
        bias_any = jnp.where(in_band, 0.0, NEG_INF)
        bias_first = jnp.where(in_band & (ki >= band), 0.0, NEG_INF)

        def block(idx, d=d, band=band, nb=nb, p=p, bias_any=bias_any, bias_first=bias_first):
            r = idx // nb
            n = idx % nb
            qs = r + d * band * n
            ps = jnp.maximum(qs - d * band, r)
            cur = pl.ds(qs, band, stride=d) if d > 1 else pl.ds(qs, band)
            prev = pl.ds(ps, band, stride=d) if d > 1 else pl.ds(ps, band)
            qb = q_ref[0, cur, :]
            kk = jnp.concatenate([k_ref[0, prev, :], k_ref[0, cur, :]], axis=0).astype(BF16)
            vv = jnp.concatenate([v_ref[0, prev, :], v_ref[0, cur, :]], axis=0).astype(BF16)
            q2 = jnp.concatenate([jnp.where(head0, qb, 0.0), jnp.where(head0, 0.0, qb)], axis=0).astype(BF16)
            s = _dot_nt(q2, kk) + jnp.where(n > 0, bias_any, bias_first)
            m = jnp.max(jnp.maximum(s[:, :band], s[:, band:]), axis=-1, keepdims=True)
            e = jnp.exp(s - m)
            l = jnp.sum(e[:, :band] + e[:, band:], axis=-1, keepdims=True)
            o = _dot(e.astype(BF16), vv) * (1.0 / l)
            lse = m + jnp.log(l)
            oacc[p, cur, :] = jnp.where(head0, o[:band], o[band:])
            lacc[p, cur, :] = jnp.where(head0, lse[:band], lse[band:])

        def body(i, carry, block=block):
            for u in range(ATTN_UNROLL):
                block(i * ATTN_UNROLL + u)
            return carry

        lax.fori_loop(0, d * nb // ATTN_UNROLL, body, 0)

    rows = 512

    def combine(i, carry):
        sl = pl.ds(pl.multiple_of(i * rows, rows), rows)
        l0, l1, l2 = lacc[0, sl, :], lacc[1, sl, :], lacc[2, sl, :]
        m = jnp.maximum(jnp.maximum(l0, l1), l2)
        w0, w1, w2 = jnp.exp(l0 - m), jnp.exp(l1 - m), jnp.exp(l2 - m)
        tot = w0 + w1 + w2
        o_ref[0, sl, :] = (w0 * oacc[0, sl, :] + w1 * oacc[1, sl, :] + w2 * oacc[2, sl, :]) / tot
        return carry

    lax.fori_loop(0, L // rows, combine, 0)


def _attention_prompt(q, k, v):
    B, L, W = q.shape
    P2 = 2 * HEAD_DIM
    blk = pl.BlockSpec((1, L, P2), lambda b, hp: (b, 0, hp))
    return pl.pallas_call(
        _attn_kernel,
        grid=(B, W // P2),
        in_specs=[blk, blk, blk],
        out_specs=blk,
        out_shape=jax.ShapeDtypeStruct((B, L, W), F32),
        scratch_shapes=[pltpu.VMEM((len(DILATED_PATTERNS), L, P2), F32)] * 2,
        compiler_params=_cparams("parallel", "parallel"),
        name="attention_prompt",
    )(q, k, v)


def _pack_bf16_pair(a, b):
    ua = pltpu.bitcast(a.astype(BF16).astype(F32), jnp.uint32)
    ub = pltpu.bitcast(b.astype(BF16).astype(F32), jnp.uint32)
    return ua | (ub >> 16)


def _unpack_bf16_pair(p):
    a = pltpu.bitcast(p & jnp.uint32(0xFFFF0000), F32)
    b = pltpu.bitcast(p << 16, F32)
    return a, b


ROUTE_LANES = 128
LANES = 128


ROW_PIECES = 4


def _store_rows(ref, packed):
    n = packed.shape[0]
    for c in range(ROW_PIECES):
        ref[pl.ds(c, n, stride=ROW_PIECES), :] = packed[:, c * LANES:(c + 1) * LANES]


def _load_rows(ref):
    n = ref.shape[0] // ROW_PIECES
    return jnp.concatenate([ref[pl.ds(c, n, stride=ROW_PIECES), :] for c in range(ROW_PIECES)], axis=1)


def _token_rows(t):
    return pl.ds(pl.multiple_of(t * ROW_PIECES, ROW_PIECES), ROW_PIECES)


def _route_chunk(s, bias, carry):
    E, n = s.shape
    G, GS = N_EXPERT_GROUPS, E // N_EXPERT_GROUPS
    NEG = -jnp.inf
    choice = s + bias
    row = lax.broadcasted_iota(jnp.int32, (E, n), 0).astype(F32)
    lrow = row[:GS]
    gs_rows = []
    for g in range(G):
        c = choice[g * GS:(g + 1) * GS]
        m1 = jnp.max(c, axis=0, keepdims=True)
        i1 = jnp.min(jnp.where(c == m1, lrow, float(GS)), axis=0, keepdims=True)
        m2 = jnp.max(jnp.where(lrow == i1, NEG, c), axis=0, keepdims=True)
        gs_rows.append(m1 + m2)
    gs = jnp.concatenate(gs_rows, axis=0)
    grow = lax.broadcasted_iota(jnp.int32, (G, n), 0).astype(F32)
    gsel = jnp.zeros((G, n), F32)
    for _ in range(TOPK_GROUPS):
        gm = jnp.max(gs, axis=0, keepdims=True)
        gi = jnp.min(jnp.where(gs == gm, grow, float(G)), axis=0, keepdims=True)
        hit = grow == gi
        gsel = jnp.where(hit, 1.0, gsel)
        gs = jnp.where(hit, NEG, gs)
    emask = jnp.concatenate([jnp.broadcast_to(gsel[g:g + 1], (GS, n)) for g in range(G)], axis=0)
    masked = jnp.where(emask > 0.5, choice, NEG)
    ids, sks = [], []
    member = jnp.zeros((E, n), F32)
    for _ in range(TOP_K):
        mk = jnp.max(masked, axis=0, keepdims=True)
        ik = jnp.min(jnp.where(masked == mk, row, float(E)), axis=0, keepdims=True)
        sel = row == ik
        sks.append(jnp.sum(jnp.where(sel, s, 0.0), axis=0, keepdims=True))
        masked = jnp.where(sel, NEG, masked)
        member = jnp.where(sel, 1.0, member)
        ids.append(ik)
    eidx = jnp.concatenate(ids, axis=0)
    sk = jnp.concatenate(sks, axis=0)
    w = sk / jnp.sum(sk, axis=0, keepdims=True) * ROUTED_SCALE
    ti = lax.broadcasted_iota(jnp.int32, (n, n), 0)
    tj = lax.broadcasted_iota(jnp.int32, (n, n), 1)
    upper = jnp.where(ti <= tj, 1.0, 0.0).astype(BF16)
    incl = _dot(member.astype(BF16), upper)
    rank_dense = carry + incl - member
    ranks = [jnp.sum(jnp.where(row == ids[k], rank_dense, 0.0), axis=0, keepdims=True) for k in range(TOP_K)]
    return eidx, w, jnp.concatenate(ranks, axis=0), carry + incl[:, n - 1:n]


def _post_kernel(or_ref, oa_ref, x_ref, g1_ref, sc_ref, sh_ref, g2_ref, n2_ref, wo_ref, wrh_ref, wrl_ref, rb_ref,
                 cin_ref, wsg_ref, wsu_ref, wsd_ref, base_ref, h2_ref, eidx_ref, w_ref, rank_ref, cnt_ref):
    @pl.when(pl.program_id(0) == 0)
    def _():
        cnt_ref[...] = cin_ref[...]

    W = RET_WIDTH
    mix = _dot(or_ref[...].astype(BF16), wo_ref[:W, :]) + _dot(oa_ref[...].astype(BF16), wo_ref[W:, :])
    x2 = x_ref[...] + g1_ref[0] * mix
    ms = jnp.mean(x2 * x2, axis=-1, keepdims=True)
    h2 = x2 * lax.rsqrt(ms + NORM_EPS) * n2_ref[...]
    h2 = h2 * (1.0 + sc_ref[0]) + sh_ref[0]
    hb = h2.astype(BF16)
    hl = (h2 - hb.astype(F32)).astype(BF16)
    half = h2.shape[1] // 2
    _store_rows(h2_ref, _pack_bf16_pair(h2[:, :half], h2[:, half:]))
    g = _dot(hb, wsg_ref[...])
    u = _dot(hb, wsu_ref[...])
    shared = _dot((g * jax.nn.sigmoid(g) * u).astype(BF16), wsd_ref[...])
    base_ref[...] = x2 + g2_ref[0] * shared

    logits = _dot_nt(wrh_ref[...], hb) + (_dot_nt(wrh_ref[...], hl) + _dot_nt(wrl_ref[...], hb))
    s = jax.nn.sigmoid(logits)
    carry = cnt_ref[...]
    n = ROUTE_LANES
    for j in range(s.shape[1] // n):
        sl = slice(j * n, (j + 1) * n)
        eidx, w, rank, carry = _route_chunk(s[:, sl], rb_ref[...], carry)
        eidx_ref[:, sl] = eidx.astype(jnp.int32)
        w_ref[:, sl] = w
        rank_ref[:, sl] = rank.astype(jnp.int32)
    cnt_ref[...] = carry


def _post_mixer(o_r, o_a, x, g1, sc2, sh2, g2, norm2, wo, wrh_t, wrl_t, rbias, counts_in, wsg, wsu, wsd, tm,
                rows_per_mod):
    N, D = x.shape
    assert D // 2 == ROW_PIECES * LANES
    W = RET_WIDTH
    E = wrh_t.shape[0]
    R = g1.shape[1]
    mod_spec = pl.BlockSpec((1, R, D), lambda i: (i // rows_per_mod, 0, 0))
    const = lambda a: pl.BlockSpec(a.shape, lambda i: (0,) * a.ndim)
    tok = lambda rows: pl.BlockSpec((rows, tm), lambda i: (0, i))
    return pl.pallas_call(
        _post_kernel,
        grid=(N // tm,),
        in_specs=[
            pl.BlockSpec((tm, W), lambda i: (i, 0)),
            pl.BlockSpec((tm, W), lambda i: (i, 0)),
            pl.BlockSpec((tm, D), lambda i: (i, 0)),
            mod_spec, mod_spec, mod_spec, mod_spec,
            const(norm2), const(wo), const(wrh_t), const(wrl_t), const(rbias), const(counts_in),
            const(wsg), const(wsu), const(wsd),
        ],
        out_specs=[
            pl.BlockSpec((tm, D), lambda i: (i, 0)),
            pl.BlockSpec((tm * ROW_PIECES, LANES), lambda i: (i, 0)),
            tok(TOP_K), tok(TOP_K), tok(TOP_K),
            pl.BlockSpec((E, 1), lambda i: (0, 0)),
        ],
        out_shape=[
            jax.ShapeDtypeStruct((N, D), F32),
            jax.ShapeDtypeStruct((N * ROW_PIECES, LANES), jnp.uint32),
            jax.ShapeDtypeStruct((TOP_K, N), jnp.int32),
            jax.ShapeDtypeStruct((TOP_K, N), F32),
            jax.ShapeDtypeStruct((TOP_K, N), jnp.int32),
            jax.ShapeDtypeStruct((E, 1), F32),
        ],
        compiler_params=_cparams("arbitrary"),
        name="post_mixer",
    )(o_r, o_a, x, g1, sc2, sh2, g2, norm2, wo, wrh_t, wrl_t, rbias, counts_in, wsg, wsu, wsd)


def _expert_kernel(be_ref, first_ref, nused_ref, nxt_ref, slot_ref, x_ref, wg_hbm, wu_hbm, wd_hbm, y_ref,
                   wg_f, wu_f, wd_f, wg_s, wu_s, wd_s, sem):
    i = pl.program_id(0)
    is_first = first_ref[i] == 1
    used = i < nused_ref[0]

    def weight_copies(e, s):
        return (pltpu.make_async_copy(wg_hbm.at[e], wg_f.at[s], sem.at[s, 0]),
                pltpu.make_async_copy(wu_hbm.at[e], wu_f.at[s], sem.at[s, 1]),
                pltpu.make_async_copy(wd_hbm.at[e], wd_f.at[s], sem.at[s, 2]))

    def compute():
        xa, xb = _unpack_bf16_pair(_load_rows(x_ref))
        half = xa.shape[1]
        x = jnp.concatenate([xa.astype(BF16), xb.astype(BF16)], axis=1)
        g = _dot(x, wg_s[...])
        u = _dot(x, wu_s[...])
        hmid = (g * jax.nn.sigmoid(g) * u).astype(BF16)
        _store_rows(y_ref, _pack_bf16_pair(_dot(hmid, wd_s[:, :half]), _dot(hmid, wd_s[:, half:])))

    @pl.when(i == 0)
    def _():
        for c in weight_copies(be_ref[0], 0):
            c.start()

    @pl.when(is_first)
    def _():
        s = slot_ref[i]
        for c in weight_copies(be_ref[i], s):
            c.wait()

        @pl.when(nxt_ref[i] >= 0)
        def _():
            for c in weight_copies(nxt_ref[i], 1 - s):
                c.start()

        wg_s[...] = wg_f[s].astype(BF16)
        wu_s[...] = wu_f[s].astype(BF16)
        wd_s[...] = wd_f[s].astype(BF16)
        compute()

    @pl.when(jnp.logical_and(jnp.logical_not(is_first), used))
    def _():
        compute()

    @pl.when(jnp.logical_not(used))
    def _():
        y_ref[...] = jnp.zeros_like(y_ref)


def _expert_matmul(xs, plan, w_gate, w_up, w_down):
    P = xs.shape[0] // ROW_PIECES
    T = EXPERT_TILE
    D, F = w_gate.shape[1], w_gate.shape[2]
    row = pl.BlockSpec((T * ROW_PIECES, LANES), lambda i, *_: (i, 0))
    row_in = pl.BlockSpec((T * ROW_PIECES, LANES), lambda i, be, fi, nu, *_: (jnp.minimum(i, nu[0] - 1), 0))
    hbm = pl.BlockSpec(memory_space=pl.ANY)
    return pl.pallas_call(
        _expert_kernel,
        grid_spec=pltpu.PrefetchScalarGridSpec(
            num_scalar_prefetch=5,
            grid=(P // T,),
            in_specs=[row_in, hbm, hbm, hbm],
            out_specs=row,
            scratch_shapes=[
                pltpu.VMEM((2, D, F), F32), pltpu.VMEM((2, D, F), F32), pltpu.VMEM((2, F, D), F32),
                pltpu.VMEM((D, F), BF16), pltpu.VMEM((D, F), BF16), pltpu.VMEM((F, D), BF16),
                pltpu.SemaphoreType.DMA((2, 3)),
            ],
        ),
        out_shape=jax.ShapeDtypeStruct(xs.shape, jnp.uint32),
        compiler_params=_cparams("arbitrary"),
        name="routed_experts",
    )(*plan, xs, w_gate, w_up, w_down)


def _dest_kernel(eidx_ref, rank_ref, off_ref, dest_ref):
    E = off_ref.shape[0]
    n = eidx_ref.shape[1]
    row = lax.broadcasted_iota(jnp.int32, (E, n), 0)
    off = off_ref[...]
    rows = []
    for k in range(eidx_ref.shape[0]):
        hit = row == eidx_ref[k:k + 1, :]
        rows.append(jnp.sum(jnp.where(hit, off, 0.0), axis=0, keepdims=True))
    dest_ref[...] = rank_ref[...] + jnp.concatenate(rows, axis=0).astype(jnp.int32)


def _dest_rows(eidx, rank, pad_off):
    K, N = eidx.shape
    tn = min(N, DEST_TILE)
    blk = pl.BlockSpec((K, tn), lambda i: (0, i))
    return pl.pallas_call(
        _dest_kernel,
        grid=(N // tn,),
        in_specs=[blk, blk, pl.BlockSpec(pad_off.shape, lambda i: (0, 0))],
        out_specs=blk,
        out_shape=jax.ShapeDtypeStruct((K, N), jnp.int32),
        compiler_params=_cparams("parallel"),
        name="dest_rows",
    )(eidx, rank, pad_off)


def _row_copy(src, dst, sem):
    return pltpu.make_async_copy(src, dst, sem)


def _dispatch_kernel(dest_ref, h_ref, xs_in_ref, xs_ref, sem):
    del xs_in_ref
    K, tm = dest_ref.shape

    def issue(t, carry):
        for k in range(K):
            _row_copy(h_ref.at[_token_rows(t)], xs_ref.at[_token_rows(dest_ref[k, t])], sem).start(
                priority=k % DMA_QUEUES)
        return carry

    lax.fori_loop(0, tm, issue, 0, unroll=ISSUE_UNROLL)
    for _ in range(K):
        _row_copy(h_ref, xs_ref.at[pl.ds(0, tm * ROW_PIECES)], sem).wait()


def _dispatch(dest, h2p, xs, tm):
    K, N = dest.shape
    return pl.pallas_call(
        _dispatch_kernel,
        grid=(N // tm,),
        in_specs=[
            pl.BlockSpec((K, tm), lambda i: (0, i), memory_space=pltpu.SMEM),
            pl.BlockSpec((tm * ROW_PIECES, LANES), lambda i: (i, 0)),
            pl.BlockSpec(memory_space=pl.ANY),
        ],
        out_specs=pl.BlockSpec(memory_space=pl.ANY),
        out_shape=jax.ShapeDtypeStruct(xs.shape, xs.dtype),
        scratch_shapes=[pltpu.SemaphoreType.DMA],
        input_output_aliases={2: 0},
        compiler_params=_cparams("arbitrary"),
        name="dispatch",
    )(dest, h2p, xs)


def _combine_kernel(dest_ref, base_ref, w_ref, g2_ref, ys_ref, o_ref, gbuf, sem):
    K, tm = dest_ref.shape

    def issue(t, carry):
        for k in range(K):
            _row_copy(ys_ref.at[_token_rows(dest_ref[k, t])], gbuf.at[k, _token_rows(t)], sem).start(
                priority=k % DMA_QUEUES)
        return carry

    lax.fori_loop(0, tm, issue, 0, unroll=ISSUE_UNROLL)
    for k in range(K):
        _row_copy(ys_ref.at[pl.ds(0, tm * ROW_PIECES)], gbuf.at[k], sem).wait()
    half = ROW_PIECES * LANES
    w = w_ref[...]
    acc_a = jnp.zeros((tm, half), F32)
    acc_b = jnp.zeros((tm, half), F32)
    for k in range(K):
        a, b = _unpack_bf16_pair(_load_rows(gbuf.at[k]))
        wk = w[:, k:k + 1]
        acc_a = acc_a + wk * a
        acc_b = acc_b + wk * b
    g2 = g2_ref[0]
    o_ref[:, :half] = base_ref[:, :half] + g2[:, :half] * acc_a
    o_ref[:, half:] = base_ref[:, half:] + g2[:, half:] * acc_b


def _combine(dest, base, w_tok, g2, ys, tm, rows_per_mod):
    K, N = dest.shape
    D = base.shape[1]
    R = g2.shape[1]
    return pl.pallas_call(
        _combine_kernel,
        grid=(N // tm,),
        in_specs=[
            pl.BlockSpec((K, tm), lambda i: (0, i), memory_space=pltpu.SMEM),
            pl.BlockSpec((tm, D), lambda i: (i, 0)),
            pl.BlockSpec((tm, K), lambda i: (i, 0)),
            pl.BlockSpec((1, R, D), lambda i: (i // rows_per_mod, 0, 0)),
            pl.BlockSpec(memory_space=pl.ANY),
        ],
        out_specs=pl.BlockSpec((tm, D), lambda i: (i, 0)),
        out_shape=jax.ShapeDtypeStruct((N, D), F32),
        scratch_shapes=[pltpu.VMEM((K, tm * ROW_PIECES, LANES), jnp.uint32), pltpu.SemaphoreType.DMA],
        compiler_params=_cparams("arbitrary"),
        name="combine",
    )(dest, base, w_tok, g2, ys)


def _tile_plan(counts, n_tiles):
    T = EXPERT_TILE
    E = counts.shape[0]
    padded = (counts + T - 1) // T * T
    pad_end = jnp.cumsum(padded)
    pad_off = pad_end - padded
    idx = jnp.arange(n_tiles, dtype=jnp.int32)
    tile_start = idx * T
    blk_exp = jnp.minimum(jnp.sum((pad_end[None, :] <= tile_start[:, None]).astype(jnp.int32), axis=1), E - 1)
    used = tile_start < pad_end[-1]
    changed = jnp.concatenate([jnp.ones((1,), bool), blk_exp[1:] != blk_exp[:-1]])
    first = jnp.logical_and(used, changed)
    next_first = lax.cummin(jnp.where(first, idx, n_tiles), reverse=True)
    next_first = jnp.concatenate([next_first[1:], jnp.full((1,), n_tiles, jnp.int32)])
    nxt = jnp.where(next_first < n_tiles, blk_exp[jnp.minimum(next_first, n_tiles - 1)], -1)
    slot = (jnp.cumsum(first.astype(jnp.int32)) - 1) % 2
    n_used = (pad_end[-1] // T).reshape(1)
    as_i32 = lambda a: a.astype(jnp.int32)
    return pad_off, (as_i32(blk_exp), as_i32(first), as_i32(n_used), as_i32(nxt), as_i32(slot))


def _ret_sample_kernel(q_ref, k_ref, v_ref, sg_ref, gn_ref, dec_ref, s_ref, o_ref, so_ref):
    q = q_ref[0]
    k = k_ref[0]
    v = v_ref[0]
    gamma = dec_ref[pl.program_id(0)]
    qk = jnp.sum(q * k, axis=0, keepdims=True)
    o = qk * v
    Dh = q.shape[0]
    acc = jnp.zeros_like(v)
    for d in range(Dh):
        S = s_ref[0, d]
        acc = acc + q[d:d + 1, :] * S
        so_ref[0, d] = S * gamma + k[d:d + 1, :] * v
    o = o + acc * gamma
    mu = jnp.mean(o, axis=0, keepdims=True)
    oc = o - mu
    var = jnp.mean(oc * oc, axis=0, keepdims=True)
    o_ref[0] = oc * lax.rsqrt(var + GN_EPS) * gn_ref[0] * sg_ref[0]


def _retention_sample(qT, kT, vT, sgT, gn_col, gamma, S):
    H, Dh, Bt = qT.shape
    vec = pl.BlockSpec((1, Dh, Bt), lambda h: (h, 0, 0))
    st = pl.BlockSpec((1, Dh, Dh, Bt), lambda h: (h, 0, 0, 0))
    return pl.pallas_call(
        _ret_sample_kernel,
        grid=(H,),
        in_specs=[vec, vec, vec, vec, pl.BlockSpec((1, Dh, 1), lambda h: (h, 0, 0)),
                  pl.BlockSpec(memory_space=pltpu.SMEM), st],
        out_specs=[vec, st],
        out_shape=[jax.ShapeDtypeStruct((H, Dh, Bt), F32), jax.ShapeDtypeStruct((H, Dh, Dh, Bt), F32)],
        compiler_params=_cparams("parallel"),
        name="retention_sample",
    )(qT, kT, vT, sgT, gn_col, gamma, S)


def _attn_sample_kernel(q_ref, kn_ref, vn_ref, knT_ref, vnT_ref, kb_ref, vb_ref, o_ref, ko_ref, vo_ref):
    b = pl.program_id(0)
    H, Dh, Wn = kb_ref.shape[1], kb_ref.shape[2], kb_ref.shape[3]
    HD = H * Dh
    q = q_ref[0]
    kn = kn_ref[0]
    vn = vn_ref[0]
    row = lax.broadcasted_iota(jnp.int32, (H, HD), 0)
    col = lax.broadcasted_iota(jnp.int32, (H, HD), 1)
    own = (col // Dh) == row
    q_bd = jnp.where(own, q, 0.0)
    s_new = jnp.sum(q_bd * kn, axis=1, keepdims=True)

    bl = lax.broadcasted_iota(jnp.int32, knT_ref.shape, 1)
    k_col = jnp.sum(jnp.where(bl == b, knT_ref[...], 0.0), axis=1, keepdims=True)
    v_col = jnp.sum(jnp.where(bl == b, vnT_ref[...], 0.0), axis=1, keepdims=True)

    t = lax.broadcasted_iota(jnp.int32, (1, Wn), 1)
    last = t == (Wn - 1)
    q_bf = q_bd.astype(BF16)
    outs = []
    for h in range(H):
        K = kb_ref[0, h]
        V = vb_ref[0, h]
        hs = slice(h * Dh, (h + 1) * Dh)
        ko_ref[0, h] = jnp.where(last, k_col[hs], pltpu.roll(K, Wn - 1, 1))
        vo_ref[0, h] = jnp.where(last, v_col[hs], pltpu.roll(V, Wn - 1, 1))
        s = _dot(q_bf[:, hs], K.astype(BF16))[h:h + 1]
        sn = s_new[h:h + 1]
        Vb = V.astype(BF16)
        vn_h = vn[:, hs]
        o_p, lse_p = [], []
        for window, d in DILATED_PATTERNS:
            valid = (((Wn - t) % d) == 0) & (t >= Wn - window)
            sm = jnp.where(valid, s, NEG_INF)
            m = jnp.maximum(jnp.max(sm, axis=1, keepdims=True), sn)
            e = jnp.exp(sm - m)
            en = jnp.exp(sn - m)
            l = jnp.sum(e, axis=1, keepdims=True) + en
            pe = jnp.broadcast_to((e / l).astype(BF16), (8, Wn))
            o_p.append(_dot_nt(pe, Vb)[0:1] + (en / l) * vn_h)
            lse_p.append(m + jnp.log(l))
        mm = jnp.maximum(jnp.maximum(lse_p[0], lse_p[1]), lse_p[2])
        ws = [jnp.exp(x - mm) for x in lse_p]
        tot = ws[0] + ws[1] + ws[2]
        outs.append((ws[0] * o_p[0] + ws[1] * o_p[1] + ws[2] * o_p[2]) / tot)
    o_ref[0] = jnp.concatenate(outs, axis=1)


def _attention_sample(q, kn, vn, knT, vnT, k_buf, v_buf):
    Bt, H, Dh, Wn = k_buf.shape
    HD = H * Dh
    vec = pl.BlockSpec((1, 1, HD), lambda b: (b, 0, 0))
    tr = pl.BlockSpec((HD, Bt), lambda b: (0, 0))
    cache = pl.BlockSpec((1, H, Dh, Wn), lambda b: (b, 0, 0, 0))
    return pl.pallas_call(
        _attn_sample_kernel,
        grid=(Bt,),
        in_specs=[vec, vec, vec, tr, tr, cache, cache],
        out_specs=[vec, cache, cache],
        out_shape=[jax.ShapeDtypeStruct((Bt, 1, HD), F32),
                   jax.ShapeDtypeStruct(k_buf.shape, F32), jax.ShapeDtypeStruct(v_buf.shape, F32)],
        compiler_params=_cparams("parallel"),
        name="attention_sample",
    )(q, kn, vn, knT, vnT, k_buf, v_buf)


def _rope_tables(pos):
    half = HEAD_DIM // 2
    inv = ROPE_THETA ** (-jnp.arange(half, dtype=F32) / half)
    ang = pos.astype(F32)[:, None] * inv[None, :]
    cos, sin = jnp.cos(ang), jnp.sin(ang)
    cos_t = jnp.tile(cos, (1, 2 * N_RET_HEADS))
    sin_t = jnp.tile(jnp.concatenate([-sin, sin], axis=1), (1, N_RET_HEADS))
    return cos_t, sin_t


def _layer(x_prompt, x_sample, c_prompt, c_sample, state_ret, cache_k, cache_v,
           norm1, w_ada, b_ada, w_in, gn_ret, q_norm, k_norm, w_out, norm2,
           w_router, router_bias, w_exp_gate, w_exp_up, w_exp_down, w_sh_gate, w_sh_up, w_sh_down):
    B, L, D = x_prompt.shape
    Bs = x_sample.shape[0]
    H, Dh, W = N_RET_HEADS, HEAD_DIM, RET_WIDTH
    Np = B * L

    pad = (-(B + Bs)) % 8
    c_all = jnp.concatenate([c_prompt, c_sample, jnp.zeros((pad, D), F32)], axis=0)
    mod = _modulation(c_all, w_ada, b_ada)
    mod_p = [m.reshape(B, 1, D) for m in jnp.split(mod[:B], 6, axis=-1)]
    mod_s = [m.reshape(1, Bs, D) for m in jnp.split(mod[B:B + Bs], 6, axis=-1)]

    w_in_bf = w_in.astype(BF16)
    gi = jnp.arange(W) // Dh
    gmat = jnp.where(gi[:, None] == gi[None, :], 1.0 / Dh, 0.0).astype(BF16)
    qn_t = jnp.tile(q_norm, N_ATT_HEADS).reshape(1, W)
    kn_t = jnp.tile(k_norm, N_ATT_HEADS).reshape(1, W)
    n1 = norm1.reshape(1, D)
    n2 = norm2.reshape(1, D)
    gn = gn_ret.reshape(1, W)

    tm = TOKEN_TILE
    cos_p, sin_p = _rope_tables(jnp.arange(L, dtype=jnp.int32))
    cos_s, sin_s = _rope_tables(PAST_LEN + jnp.arange(1, dtype=jnp.int32))

    xp = x_prompt.reshape(Np, D)
    xs = x_sample.reshape(Bs, D)
    proj_p = _in_proj(xp, mod_p[1], mod_p[0], cos_p, sin_p, n1, qn_t, kn_t, gmat, w_in_bf, tm, L // tm, L // tm)
    proj_s = _in_proj(xs, mod_s[1], mod_s[0], cos_s, sin_s, n1, qn_t, kn_t, gmat, w_in_bf, Bs, 1, 1)
    qr, kr, vr, sg, qa, ka, va = [t.reshape(B, L, W) for t in proj_p]
    qr_s, kr_s, vr_s, sg_s, qa_s, ka_s, va_s = proj_s

    o_r, state_p = _retention_prompt(qr, kr, vr, sg, gn, gmat)
    o_a = _attention_prompt(qa, ka, va)
    keep = min(MAX_WINDOW, L)
    cache_kp = ka[:, L - keep:].reshape(B, keep, N_ATT_HEADS, Dh)
    cache_vp = va[:, L - keep:].reshape(B, keep, N_ATT_HEADS, Dh)

    log_g = jnp.log1p(-(2.0 ** (-5.0 - jnp.arange(H, dtype=F32))))
    gamma = jnp.exp(log_g).astype(F32)
    tr = lambda t: t.T.reshape(H, Dh, Bs)
    S_t = jnp.transpose(state_ret, (1, 2, 3, 0))
    orT, S_new_t = _retention_sample(tr(qr_s), tr(kr_s), tr(vr_s), tr(sg_s), gn_ret.reshape(H, Dh, 1), gamma, S_t)
    o_r_s = orT.reshape(W, Bs).T
    state_s = jnp.transpose(S_new_t, (3, 0, 1, 2))

    kb_t = jnp.transpose(cache_k, (0, 2, 3, 1))
    vb_t = jnp.transpose(cache_v, (0, 2, 3, 1))
    o_a_s, ko_t, vo_t = _attention_sample(qa_s.reshape(Bs, 1, W), ka_s.reshape(Bs, 1, W), va_s.reshape(Bs, 1, W),
                                          ka_s.T, va_s.T, kb_t, vb_t)
    cache_ks = jnp.transpose(ko_t, (0, 3, 1, 2))
    cache_vs = jnp.transpose(vo_t, (0, 3, 1, 2))

    wo = w_out.astype(BF16)
    wr_t = w_router.T
    wrh = wr_t.astype(BF16)
    wrl = (wr_t - wrh.astype(F32)).astype(BF16)
    rbias = router_bias.astype(F32).reshape(N_EXPERTS, 1)
    wsg, wsu, wsd = w_sh_gate.astype(BF16), w_sh_up.astype(BF16), w_sh_down.astype(BF16)
    zero_counts = jnp.zeros((N_EXPERTS, 1), F32)
    base_p, h2_p, eidx_p, w_p, rank_p, counts_p = _post_mixer(
        o_r.reshape(Np, W), o_a.reshape(Np, W), xp, mod_p[2], mod_p[4], mod_p[3], mod_p[5], n2, wo, wrh, wrl, rbias,
        zero_counts, wsg, wsu, wsd, tm, L // tm)
    base_s, h2_s, eidx_s, w_s, rank_s, counts = _post_mixer(
        o_r_s, o_a_s.reshape(Bs, W), xs, mod_s[2], mod_s[4], mod_s[3], mod_s[5], n2, wo, wrh, wrl, rbias,
        counts_p, wsg, wsu, wsd, Bs, 1)

    T = EXPERT_TILE
    n_tiles = ((Np + Bs) * TOP_K + N_EXPERTS * T + T - 1) // T
    pad_off, plan = _tile_plan(counts.reshape(N_EXPERTS).astype(jnp.int32), n_tiles)
    pad_off = pad_off.astype(F32).reshape(N_EXPERTS, 1)
    dest_p = _dest_rows(eidx_p, rank_p, pad_off)
    dest_s = _dest_rows(eidx_s, rank_s, pad_off)
    grouped = jnp.zeros((n_tiles * T * ROW_PIECES, LANES), jnp.uint32)
    grouped = _dispatch(dest_p, h2_p, grouped, tm)
    grouped = _dispatch(dest_s, h2_s, grouped, Bs)
    ys = _expert_matmul(grouped, plan, w_exp_gate, w_exp_up, w_exp_down)
    tc = COMBINE_TILE
    y_p = _combine(dest_p, base_p, w_p.T, mod_p[5], ys, tc, L // tc)
    y_s = _combine(dest_s, base_s, w_s.T, mod_s[5], ys, Bs, 1)

    return (y_p.reshape(B, L, D), y_s.reshape(Bs, 1, D), state_p, cache_kp, cache_vp, state_s, cache_ks, cache_vs)


def kernel(x_prompt, x_sample, c_prompt, c_sample, state_ret, cache_win_k, cache_win_v, norm1, w_ada, b_ada, w_in,
           gn_ret, q_norm, k_norm, w_out, norm2, w_router, router_bias, w_exp_gate, w_exp_up, w_exp_down,
           w_sh_gate, w_sh_up, w_sh_down):
    assert w_in.shape[0] == 1, "single-layer step"
    assert x_sample.shape[1] == 1, "one new token per sample sequence"
    outs = _layer(x_prompt, x_sample, c_prompt, c_sample, state_ret[0], cache_win_k[0], cache_win_v[0],
                  norm1[0], w_ada[0], b_ada[0], w_in[0], gn_ret[0], q_norm[0], k_norm[0], w_out[0], norm2[0],
                  w_router[0], router_bias[0], w_exp_gate[0], w_exp_up[0], w_exp_down[0],
                  w_sh_gate[0], w_sh_up[0], w_sh_down[0])
    yp, ys, sp, kp, vp, ss, ksm, vsm = outs
    return (yp, ys, sp[None], kp[None], vp[None], ss[None], ksm[None], vsm[None])
```

```python
import functools

import jax
import jax.numpy as jnp
from jax import lax
from jax.experimental import pallas as pl
from jax.experimental.pallas import tpu as pltpu

HEAD_DIM = 64
N_RET_HEADS = 8
N_ATT_HEADS = 8
RET_WIDTH = N_RET_HEADS * HEAD_DIM
ATT_WIDTH = N_ATT_HEADS * HEAD_DIM
RET_CHUNK = 128
DILATED_PATTERNS = ((128, 1), (512, 4), (2048, 16))
MAX_WINDOW = 2048
PAST_LEN = 8192
ROPE_THETA = 10000.0
N_EXPERTS = 256
TOP_K = 8
N_EXPERT_GROUPS = 8
TOPK_GROUPS = 4
ROUTED_SCALE = 2.5
NORM_EPS = 1e-6
GN_EPS = 1e-5
NEG_INF = -1e30

F32 = jnp.float32
BF16 = jnp.bfloat16

VMEM_LIMIT_BYTES = 56 * 1024 * 1024
EXPERT_TILE = 256
TOKEN_TILE = 512
COMBINE_TILE = 256
DMA_QUEUES = 2
ISSUE_UNROLL = 4
RET_STEP_CHUNKS = 4
EXPERT_ROW_SLOTS = 3
DEST_TILE = 2048
ATTN_UNROLL = 8


def _cparams(*sem):
    return pltpu.CompilerParams(dimension_semantics=sem, vmem_limit_bytes=VMEM_LIMIT_BYTES)


def _dot(a, b):
    return jnp.dot(a, b, preferred_element_type=F32)


def _dot_nt(a, b):
    return lax.dot_general(a, b, (((1,), (1,)), ((), ())), preferred_element_type=F32)


def _dot_tn(a, b):
    return lax.dot_general(a, b, (((0,), (0,)), ((), ())), preferred_element_type=F32)


def _mod_kernel(c_ref, w_ref, b_ref, o_ref):
    c = c_ref[...]
    a = (c * jax.nn.sigmoid(c)).astype(BF16)
    o_ref[...] = _dot(a, w_ref[...].astype(BF16)) + b_ref[...]


def _modulation(c, w_ada, b_ada):
    R, D = c.shape
    n_out = w_ada.shape[1]
    tn = 1024
    return pl.pallas_call(
        _mod_kernel,
        grid=(n_out // tn,),
        in_specs=[
            pl.BlockSpec((R, D), lambda j: (0, 0)),
            pl.BlockSpec((D, tn), lambda j: (0, j)),
            pl.BlockSpec((1, tn), lambda j: (0, j)),
        ],
        out_specs=pl.BlockSpec((R, tn), lambda j: (0, j)),
        out_shape=jax.ShapeDtypeStruct((R, n_out), F32),
        compiler_params=_cparams("parallel"),
        name="modulation",
    )(c, w_ada, b_ada.reshape(1, n_out))


def _swap_halves(x, first_half):
    n = x.shape[-1]
    return jnp.where(first_half, pltpu.roll(x, n - HEAD_DIM // 2, 1), pltpu.roll(x, HEAD_DIM // 2, 1))


def _head_mean(x, g):
    hi = x.astype(BF16)
    lo = (x - hi.astype(F32)).astype(BF16)
    return _dot(hi, g) + _dot(lo, g)


def _in_proj_kernel(x_ref, sc_ref, sh_ref, cos_ref, sin_ref, n1_ref, qn_ref, kn_ref, g_ref, w_ref,
                    qr_ref, kr_ref, vr_ref, sg_ref, qa_ref, ka_ref, va_ref):
    x = x_ref[...]
    ms = jnp.mean(x * x, axis=-1, keepdims=True)
    h = x * lax.rsqrt(ms + NORM_EPS) * n1_ref[...]
    h = (h * (1.0 + sc_ref[0]) + sh_ref[0]).astype(BF16)
    cos = cos_ref[...]
    sin = sin_ref[...]
    W = RET_WIDTH
    lane = lax.broadcasted_iota(jnp.int32, (1, W), 1)
    first_half = (lane % HEAD_DIM) < (HEAD_DIM // 2)

    def proj(c):
        return _dot(h, w_ref[:, c * W:(c + 1) * W])

    def rot(t):
        return t * cos + _swap_halves(t, first_half) * sin

    def qk_norm(t, gain):
        return t * lax.rsqrt(_head_mean(t * t, g_ref[...]) + NORM_EPS) * gain

    qr_ref[...] = rot(proj(0))
    kr_ref[...] = rot(proj(1)) * (HEAD_DIM ** -0.5)
    vr_ref[...] = proj(2)
    gr = proj(3)
    sg_ref[...] = gr * jax.nn.sigmoid(gr)
    qa_ref[...] = rot(qk_norm(proj(4), qn_ref[...])) * (HEAD_DIM ** -0.5)
    ka_ref[...] = rot(qk_norm(proj(5), kn_ref[...]))
    va_ref[...] = proj(6)


def _in_proj(x, sc, sh, cos, sin, norm1, q_norm_t, k_norm_t, gmat, w_in_bf, tm, rows_per_mod, pos_blocks):
    N, D = x.shape
    W = RET_WIDTH
    R = sc.shape[1]
    mod_spec = pl.BlockSpec((1, R, D), lambda i: (i // rows_per_mod, 0, 0))
    pos_spec = pl.BlockSpec((cos.shape[0] // pos_blocks, W), lambda i: (i % pos_blocks, 0))
    const = lambda shape: pl.BlockSpec(shape, lambda i: (0,) * len(shape))
    out_spec = pl.BlockSpec((tm, W), lambda i: (i, 0))
    return pl.pallas_call(
        _in_proj_kernel,
        grid=(N // tm,),
        in_specs=[
            pl.BlockSpec((tm, D), lambda i: (i, 0)),
            mod_spec, mod_spec, pos_spec, pos_spec,
            const((1, D)), const((1, W)), const((1, W)), const((W, W)), const(w_in_bf.shape),
        ],
        out_specs=[out_spec] * 7,
        out_shape=[jax.ShapeDtypeStruct((N, W), F32)] * 7,
        compiler_params=_cparams("parallel"),
        name="in_proj",
    )(x, sc, sh, cos, sin, norm1, q_norm_t, k_norm_t, gmat, w_in_bf)


def _ret_kernel(q_ref, k_ref, v_ref, sg_ref, dm_ref, qd_ref, kd_ref, sd_ref, bd_ref, gm_ref, gn_ref, o_ref, s_ref):
    @pl.when(pl.program_id(1) == 0)
    def _():
        s_ref[...] = jnp.zeros_like(s_ref)

    C = dm_ref.shape[2]
    Dh = HEAD_DIM
    P2 = 2 * Dh
    n_pairs = N_RET_HEADS // 2
    lane = lax.broadcasted_iota(jnp.int32, (1, P2), 1)
    head0 = lane < Dh
    zero = jnp.zeros((Dh, Dh), F32)
    state = []
    for p in range(n_pairs):
        top = jnp.concatenate([s_ref[0, 2 * p], zero], axis=1)
        bot = jnp.concatenate([zero, s_ref[0, 2 * p + 1]], axis=1)
        state.append(jnp.concatenate([top, bot], axis=0))
    bd = bd_ref[...]
    gm = gm_ref[...]
    for j in range(q_ref.shape[1] // C):
        rows = slice(j * C, (j + 1) * C)
        for p in range(n_pairs):
            sl = slice(P2 * p, P2 * (p + 1))
            q = q_ref[0, rows, sl]
            k = k_ref[0, rows, sl]
            vb = v_ref[0, rows, sl].astype(BF16)
            kdb = (k * kd_ref[:, sl]).astype(BF16)
            q2 = jnp.concatenate([jnp.where(head0, q, 0.0), jnp.where(head0, 0.0, q)], axis=0).astype(BF16)
            sc = _dot_nt(q2, k.astype(BF16)) * dm_ref[p]
            o2 = _dot(sc.astype(BF16), vb)
            S = state[p]
            o = jnp.where(head0, o2[:C], o2[C:]) + _dot(q.astype(BF16), S.astype(BF16)) * qd_ref[:, sl]
            state[p] = S * sd_ref[p] + _dot_tn(kdb, vb) * bd
            o_ref[0, rows, sl] = o
    for p in range(n_pairs):
        s_ref[0, 2 * p] = state[p][:Dh, :Dh]
        s_ref[0, 2 * p + 1] = state[p][Dh:, Dh:]
    o = o_ref[0]
    oc = o - _head_mean(o, gm)
    var = _head_mean(oc * oc, gm)
    o_ref[0] = oc * lax.rsqrt(var + GN_EPS) * gn_ref[...] * sg_ref[0]


def _decay_tables(C):
    H = N_RET_HEADS
    log_g = jnp.log1p(-(2.0 ** (-5.0 - jnp.arange(H, dtype=F32))))
    i = jnp.arange(C)
    diff = i[:, None] - i[None, :]
    dmask = jnp.where(diff[None] >= 0, jnp.exp(log_g[:, None, None] * jnp.maximum(diff, 0)[None]), 0.0).astype(F32)
    q_decay = jnp.exp(log_g[None, :] * (i[:, None] + 1)).astype(F32)
    k_decay = jnp.exp(log_g[None, :] * (C - 1 - i)[:, None]).astype(F32)
    s_decay = jnp.exp(log_g * C).astype(F32)
    return dmask, q_decay, k_decay, s_decay


def _retention_prompt(q, k, v, sg, gn, gm):
    B, L, W = q.shape
    C = RET_CHUNK
    H, Dh = N_RET_HEADS, HEAD_DIM
    dmask, q_decay, k_decay, s_decay = _decay_tables(C)
    qd = jnp.repeat(q_decay, Dh, axis=1)
    kd = jnp.repeat(k_decay, Dh, axis=1)
    P2 = 2 * Dh
    pair_head = jnp.arange(P2) // Dh
    bd = (pair_head[:, None] == pair_head[None, :]).astype(F32)
    sd = s_decay.reshape(H // 2, 2)[:, pair_head][:, :, None] * bd[None]
    rows = C * RET_STEP_CHUNKS
    blk = pl.BlockSpec((1, rows, W), lambda b, c: (b, c, 0))
    const = lambda shape: pl.BlockSpec(shape, lambda b, c: (0,) * len(shape))
    return pl.pallas_call(
        _ret_kernel,
        grid=(B, L // rows),
        in_specs=[blk, blk, blk, blk, const((H // 2, 2 * C, C)), const((C, W)), const((C, W)),
                  const((H // 2, P2, P2)), const((P2, P2)), const((W, W)), const((1, W))],
        out_specs=[blk, pl.BlockSpec((1, H, Dh, Dh), lambda b, c: (b, 0, 0, 0))],
        out_shape=[jax.ShapeDtypeStruct((B, L, W), F32), jax.ShapeDtypeStruct((B, H, Dh, Dh), F32)],
        compiler_params=_cparams("parallel", "arbitrary"),
        name="retention_prompt",
    )(q, k, v, sg, dmask.reshape(H // 2, 2 * C, C), qd, kd, sd, bd, gm, gn)


def _attn_kernel(q_ref, k_ref, v_ref, o_ref, oacc, lacc):
    L = q_ref.shape[1]
    P2 = q_ref.shape[2]
    lane = lax.broadcasted_iota(jnp.int32, (1, P2), 1)
    head0 = lane < HEAD_DIM

    for p, (window, d) in enumerate(DILATED_PATTERNS):
        band = window // d
        nb = L // (d * band)
        qi = lax.broadcasted_iota(jnp.int32, (2 * band, 2 * band), 0) % band
        ki = lax.broadcasted_iota(jnp.int32, (2 * band, 2 * band), 1)
        dist = qi + band - ki
        in_band = (dist >= 0) & (dist <= band)
        bias_any = jnp.where(in_band, 0.0, NEG_INF)
        bias_first = jnp.where(in_band & (ki >= band), 0.0, NEG_INF)

        def block(idx, d=d, band=band, nb=nb, p=p, bias_any=bias_any, bias_first=bias_first):
            r = idx // nb
            n = idx % nb
            qs = r + d * band * n
            ps = jnp.maximum(qs - d * band, r)
            cur = pl.ds(qs, band, stride=d) if d > 1 else pl.ds(qs, band)
            prev = pl.ds(ps, band, stride=d) if d > 1 else pl.ds(ps, band)
            qb = q_ref[0, cur, :]
            kk = jnp.concatenate([k_ref[0, prev, :], k_ref[0, cur, :]], axis=0).astype(BF16)
            vv = jnp.concatenate([v_ref[0, prev, :], v_ref[0, cur, :]], axis=0).astype(BF16)
            q2 = jnp.concatenate([jnp.where(head0, qb, 0.0), jnp.where(head0, 0.0, qb)], axis=0).astype(BF16)
            s = _dot_nt(q2, kk) + jnp.where(n > 0, bias_any, bias_first)
            m = jnp.max(jnp.maximum(s[:, :band], s[:, band:]), axis=-1, keepdims=True)
            e = jnp.exp(s - m)
            l = jnp.sum(e[:, :band] + e[:, band:], axis=-1, keepdims=True)
            o = _dot(e.astype(BF16), vv) * (1.0 / l)
            lse = m + jnp.log(l)
            oacc[p, cur, :] = jnp.where(head0, o[:band], o[band:])
            lacc[p, cur, :] = jnp.where(head0, lse[:band], lse[band:])

        def body(i, carry, block=block):
            for u in range(ATTN_UNROLL):
                block(i * ATTN_UNROLL + u)
            return carry

        lax.fori_loop(0, d * nb // ATTN_UNROLL, body, 0)

    rows = 512

    def combine(i, carry):
        sl = pl.ds(pl.multiple_of(i * rows, rows), rows)
        l0, l1, l2 = lacc[0, sl, :], lacc[1, sl, :], lacc[2, sl, :]
        m = jnp.maximum(jnp.maximum(l0, l1), l2)
        w0, w1, w2 = jnp.exp(l0 - m), jnp.exp(l1 - m), jnp.exp(l2 - m)
        tot = w0 + w1 + w2
        o_ref[0, sl, :] = (w0 * oacc[0, sl, :] + w1 * oacc[1, sl, :] + w2 * oacc[2, sl, :]) / tot
        return carry

    lax.fori_loop(0, L // rows, combine, 0)


def _attention_prompt(q, k, v):
    B, L, W = q.shape
    P2 = 2 * HEAD_DIM
    blk = pl.BlockSpec((1, L, P2), lambda b, hp: (b, 0, hp))
    return pl.pallas_call(
        _attn_kernel,
        grid=(B, W // P2),
        in_specs=[blk, blk, blk],
        out_specs=blk,
        out_shape=jax.ShapeDtypeStruct((B, L, W), F32),
        scratch_shapes=[pltpu.VMEM((len(DILATED_PATTERNS), L, P2), F32)] * 2,
        compiler_params=_cparams("parallel", "parallel"),
        name="attention_prompt",
    )(q, k, v)


def _pack_bf16_pair(a, b):
    ua = pltpu.bitcast(a.astype(BF16).astype(F32), jnp.uint32)
    ub = pltpu.bitcast(b.astype(BF16).astype(F32), jnp.uint32)
    return ua | (ub >> 16)


def _unpack_bf16_pair(p):
    a = pltpu.bitcast(p & jnp.uint32(0xFFFF0000), F32)
    b = pltpu.bitcast(p << 16, F32)
    return a, b


ROUTE_LANES = 128
LANES = 128


ROW_PIECES = 4


def _store_rows(ref, packed):
    n = packed.shape[0]
    for c in range(ROW_PIECES):
        ref[pl.ds(c, n, stride=ROW_PIECES), :] = packed[:, c * LANES:(c + 1) * LANES]


def _load_rows(ref):
    n = ref.shape[0] // ROW_PIECES
    return jnp.concatenate([ref[pl.ds(c, n, stride=ROW_PIECES), :] for c in range(ROW_PIECES)], axis=1)


def _token_rows(t):
    return pl.ds(pl.multiple_of(t * ROW_PIECES, ROW_PIECES), ROW_PIECES)


def _route_chunk(s, bias, carry):
    E, n = s.shape
    G, GS = N_EXPERT_GROUPS, E // N_EXPERT_GROUPS
    NEG = -jnp.inf
    choice = s + bias
    row = lax.broadcasted_iota(jnp.int32, (E, n), 0).astype(F32)
    lrow = row[:GS]
    gs_rows = []
    for g in range(G):
        c = choice[g * GS:(g + 1) * GS]
        m1 = jnp.max(c, axis=0, keepdims=True)
        i1 = jnp.min(jnp.where(c == m1, lrow, float(GS)), axis=0, keepdims=True)
        m2 = jnp.max(jnp.where(lrow == i1, NEG, c), axis=0, keepdims=True)
        gs_rows.append(m1 + m2)
    gs = jnp.concatenate(gs_rows, axis=0)
    grow = lax.broadcasted_iota(jnp.int32, (G, n), 0).astype(F32)
    gsel = jnp.zeros((G, n), F32)
    for _ in range(TOPK_GROUPS):
        gm = jnp.max(gs, axis=0, keepdims=True)
        gi = jnp.min(jnp.where(gs == gm, grow, float(G)), axis=0, keepdims=True)
        hit = grow == gi
        gsel = jnp.where(hit, 1.0, gsel)
        gs = jnp.where(hit, NEG, gs)
    emask = jnp.concatenate([jnp.broadcast_to(gsel[g:g + 1], (GS, n)) for g in range(G)], axis=0)
    masked = jnp.where(emask > 0.5, choice, NEG)
    ids, sks = [], []
    member = jnp.zeros((E, n), F32)
    for _ in range(TOP_K):
        mk = jnp.max(masked, axis=0, keepdims=True)
        ik = jnp.min(jnp.where(masked == mk, row, float(E)), axis=0, keepdims=True)
        sel = row == ik
        sks.append(jnp.sum(jnp.where(sel, s, 0.0), axis=0, keepdims=True))
        masked = jnp.where(sel, NEG, masked)
        member = jnp.where(sel, 1.0, member)
        ids.append(ik)
    eidx = jnp.concatenate(ids, axis=0)
    sk = jnp.concatenate(sks, axis=0)
    w = sk / jnp.sum(sk, axis=0, keepdims=True) * ROUTED_SCALE
    ti = lax.broadcasted_iota(jnp.int32, (n, n), 0)
    tj = lax.broadcasted_iota(jnp.int32, (n, n), 1)
    upper = jnp.where(ti <= tj, 1.0, 0.0).astype(BF16)
    incl = _dot(member.astype(BF16), upper)
    rank_dense = carry + incl - member
    ranks = [jnp.sum(jnp.where(row == ids[k], rank_dense, 0.0), axis=0, keepdims=True) for k in range(TOP_K)]
    return eidx, w, jnp.concatenate(ranks, axis=0), carry + incl[:, n - 1:n]


def _post_kernel(or_ref, oa_ref, x_ref, g1_ref, sc_ref, sh_ref, g2_ref, n2_ref, wo_ref, wrh_ref, wrl_ref, rb_ref,
                 cin_ref, wsg_ref, wsu_ref, wsd_ref, base_ref, h2_ref, eidx_ref, w_ref, rank_ref, cnt_ref):
    @pl.when(pl.program_id(0) == 0)
    def _():
        cnt_ref[...] = cin_ref[...]

    W = RET_WIDTH
    mix = _dot(or_ref[...].astype(BF16), wo_ref[:W, :]) + _dot(oa_ref[...].astype(BF16), wo_ref[W:, :])
    x2 = x_ref[...] + g1_ref[0] * mix
    ms = jnp.mean(x2 * x2, axis=-1, keepdims=True)
    h2 = x2 * lax.rsqrt(ms + NORM_EPS) * n2_ref[...]
    h2 = h2 * (1.0 + sc_ref[0]) + sh_ref[0]
    hb = h2.astype(BF16)
    hl = (h2 - hb.astype(F32)).astype(BF16)
    half = h2.shape[1] // 2
    _store_rows(h2_ref, _pack_bf16_pair(h2[:, :half], h2[:, half:]))
    g = _dot(hb, wsg_ref[...])
    u = _dot(hb, wsu_ref[...])
    shared = _dot((g * jax.nn.sigmoid(g) * u).astype(BF16), wsd_ref[...])
    base_ref[...] = x2 + g2_ref[0] * shared

    logits = _dot_nt(wrh_ref[...], hb) + (_dot_nt(wrh_ref[...], hl) + _dot_nt(wrl_ref[...], hb))
    s = jax.nn.sigmoid(logits)
    carry = cnt_ref[...]
    n = ROUTE_LANES
    for j in range(s.shape[1] // n):
        sl = slice(j * n, (j + 1) * n)
        eidx, w, rank, carry = _route_chunk(s[:, sl], rb_ref[...], carry)
        eidx_ref[:, sl] = eidx.astype(jnp.int32)
        w_ref[:, sl] = w
        rank_ref[:, sl] = rank.astype(jnp.int32)
    cnt_ref[...] = carry


def _post_mixer(o_r, o_a, x, g1, sc2, sh2, g2, norm2, wo, wrh_t, wrl_t, rbias, counts_in, wsg, wsu, wsd, tm,
                rows_per_mod):
    N, D = x.shape
    assert D // 2 == ROW_PIECES * LANES
    W = RET_WIDTH
    E = wrh_t.shape[0]
    R = g1.shape[1]
    mod_spec = pl.BlockSpec((1, R, D), lambda i: (i // rows_per_mod, 0, 0))
    const = lambda a: pl.BlockSpec(a.shape, lambda i: (0,) * a.ndim)
    tok = lambda rows: pl.BlockSpec((rows, tm), lambda i: (0, i))
    return pl.pallas_call(
        _post_kernel,
        grid=(N // tm,),
        in_specs=[
            pl.BlockSpec((tm, W), lambda i: (i, 0)),
            pl.BlockSpec((tm, W), lambda i: (i, 0)),
            pl.BlockSpec((tm, D), lambda i: (i, 0)),
            mod_spec, mod_spec, mod_spec, mod_spec,
            const(norm2), const(wo), const(wrh_t), const(wrl_t), const(rbias), const(counts_in),
            const(wsg), const(wsu), const(wsd),
        ],
        out_specs=[
            pl.BlockSpec((tm, D), lambda i: (i, 0)),
            pl.BlockSpec((tm * ROW_PIECES, LANES), lambda i: (i, 0)),
            tok(TOP_K), tok(TOP_K), tok(TOP_K),
            pl.BlockSpec((E, 1), lambda i: (0, 0)),
        ],
        out_shape=[
            jax.ShapeDtypeStruct((N, D), F32),
            jax.ShapeDtypeStruct((N * ROW_PIECES, LANES), jnp.uint32),
            jax.ShapeDtypeStruct((TOP_K, N), jnp.int32),
            jax.ShapeDtypeStruct((TOP_K, N), F32),
            jax.ShapeDtypeStruct((TOP_K, N), jnp.int32),
            jax.ShapeDtypeStruct((E, 1), F32),
        ],
        compiler_params=_cparams("arbitrary"),
        name="post_mixer",
    )(o_r, o_a, x, g1, sc2, sh2, g2, norm2, wo, wrh_t, wrl_t, rbias, counts_in, wsg, wsu, wsd)


def _expert_kernel(be_ref, first_ref, nused_ref, nxt_ref, slot_ref, x_hbm, wg_hbm, wu_hbm, wd_hbm, y_ref,
                   xbuf, wg_f, wu_f, wd_f, wg_s, wu_s, wd_s, xsem, sem):
    i = pl.program_id(0)
    is_first = first_ref[i] == 1
    n_used = nused_ref[0]
    used = i < n_used
    tile_rows = xbuf.shape[1]
    x_slots = xbuf.shape[0]

    def weight_copies(e, s):
        return (pltpu.make_async_copy(wg_hbm.at[e], wg_f.at[s], sem.at[s, 0]),
                pltpu.make_async_copy(wu_hbm.at[e], wu_f.at[s], sem.at[s, 1]),
                pltpu.make_async_copy(wd_hbm.at[e], wd_f.at[s], sem.at[s, 2]))

    def x_copy(tile):
        s = tile % x_slots
        rows = pl.ds(pl.multiple_of(tile * tile_rows, tile_rows), tile_rows)
        return pltpu.make_async_copy(x_hbm.at[rows], xbuf.at[s], xsem.at[s])

    @pl.when(i == 0)
    def _():
        for t in range(x_slots - 1):
            @pl.when(t < n_used)
            def _():
                x_copy(t).start()

    @pl.when(used)
    def _():
        x_copy(i).wait()

    @pl.when(i + (x_slots - 1) < n_used)
    def _():
        x_copy(i + (x_slots - 1)).start()

    def compute():
        xa, xb = _unpack_bf16_pair(_load_rows(xbuf.at[i % x_slots]))
        half = xa.shape[1]
        x = jnp.concatenate([xa.astype(BF16), xb.astype(BF16)], axis=1)
        g = _dot(x, wg_s[...])
        u = _dot(x, wu_s[...])
        hmid = (g * jax.nn.sigmoid(g) * u).astype(BF16)
        _store_rows(y_ref, _pack_bf16_pair(_dot(hmid, wd_s[:, :half]), _dot(hmid, wd_s[:, half:])))

    @pl.when(i == 0)
    def _():
        for c in weight_copies(be_ref[0], 0):
            c.start()

    @pl.when(is_first)
    def _():
        s = slot_ref[i]
        for c in weight_copies(be_ref[i], s):
            c.wait()

        @pl.when(nxt_ref[i] >= 0)
        def _():
            for c in weight_copies(nxt_ref[i], 1 - s):
                c.start()

        wg_s[...] = wg_f[s].astype(BF16)
        wu_s[...] = wu_f[s].astype(BF16)
        wd_s[...] = wd_f[s].astype(BF16)
        compute()

    @pl.when(jnp.logical_and(jnp.logical_not(is_first), used))
    def _():
        compute()

    @pl.when(jnp.logical_not(used))
    def _():
        y_ref[...] = jnp.zeros_like(y_ref)


def _expert_matmul(xs, plan, w_gate, w_up, w_down):
    P = xs.shape[0] // ROW_PIECES
    T = EXPERT_TILE
    D, F = w_gate.shape[1], w_gate.shape[2]
    row = pl.BlockSpec((T * ROW_PIECES, LANES), lambda i, *_: (i, 0))
    hbm = pl.BlockSpec(memory_space=pl.ANY)
    return pl.pallas_call(
        _expert_kernel,
        grid_spec=pltpu.PrefetchScalarGridSpec(
            num_scalar_prefetch=5,
            grid=(P // T,),
            in_specs=[hbm, hbm, hbm, hbm],
            out_specs=row,
            scratch_shapes=[
                pltpu.VMEM((EXPERT_ROW_SLOTS, T * ROW_PIECES, LANES), jnp.uint32),
                pltpu.VMEM((2, D, F), F32), pltpu.VMEM((2, D, F), F32), pltpu.VMEM((2, F, D), F32),
                pltpu.VMEM((D, F), BF16), pltpu.VMEM((D, F), BF16), pltpu.VMEM((F, D), BF16),
                pltpu.SemaphoreType.DMA((EXPERT_ROW_SLOTS,)),
                pltpu.SemaphoreType.DMA((2, 3)),
            ],
        ),
        out_shape=jax.ShapeDtypeStruct(xs.shape, jnp.uint32),
        compiler_params=_cparams("arbitrary"),
        name="routed_experts",
    )(*plan, xs, w_gate, w_up, w_down)


def _dest_kernel(eidx_ref, rank_ref, off_ref, dest_ref):
    E = off_ref.shape[0]
    n = eidx_ref.shape[1]
    row = lax.broadcasted_iota(jnp.int32, (E, n), 0)
    off = off_ref[...]
    rows = []
    for k in range(eidx_ref.shape[0]):
        hit = row == eidx_ref[k:k + 1, :]
        rows.append(jnp.sum(jnp.where(hit, off, 0.0), axis=0, keepdims=True))
    dest_ref[...] = rank_ref[...] + jnp.concatenate(rows, axis=0).astype(jnp.int32)


def _dest_rows(eidx, rank, pad_off):
    K, N = eidx.shape
    tn = min(N, DEST_TILE)
    blk = pl.BlockSpec((K, tn), lambda i: (0, i))
    return pl.pallas_call(
        _dest_kernel,
        grid=(N // tn,),
        in_specs=[blk, blk, pl.BlockSpec(pad_off.shape, lambda i: (0, 0))],
        out_specs=blk,
        out_shape=jax.ShapeDtypeStruct((K, N), jnp.int32),
        compiler_params=_cparams("parallel"),
        name="dest_rows",
    )(eidx, rank, pad_off)


def _row_copy(src, dst, sem):
    return pltpu.make_async_copy(src, dst, sem)


def _dispatch_kernel(dest_ref, h_ref, xs_in_ref, xs_ref, sem):
    del xs_in_ref
    K, tm = dest_ref.shape

    def issue(t, carry):
        for k in range(K):
            _row_copy(h_ref.at[_token_rows(t)], xs_ref.at[_token_rows(dest_ref[k, t])], sem).start(
                priority=k % DMA_QUEUES)
        return carry

    lax.fori_loop(0, tm, issue, 0, unroll=ISSUE_UNROLL)
    for _ in range(K):
        _row_copy(h_ref, xs_ref.at[pl.ds(0, tm * ROW_PIECES)], sem).wait()


def _dispatch(dest, h2p, xs, tm):
    K, N = dest.shape
    return pl.pallas_call(
        _dispatch_kernel,
        grid=(N // tm,),
        in_specs=[
            pl.BlockSpec((K, tm), lambda i: (0, i), memory_space=pltpu.SMEM),
            pl.BlockSpec((tm * ROW_PIECES, LANES), lambda i: (i, 0)),
            pl.BlockSpec(memory_space=pl.ANY),
        ],
        out_specs=pl.BlockSpec(memory_space=pl.ANY),
        out_shape=jax.ShapeDtypeStruct(xs.shape, xs.dtype),
        scratch_shapes=[pltpu.SemaphoreType.DMA],
        input_output_aliases={2: 0},
        compiler_params=_cparams("arbitrary"),
        name="dispatch",
    )(dest, h2p, xs)


def _combine_kernel(dest_ref, base_ref, w_ref, g2_ref, ys_ref, o_ref, gbuf, sem):
    K, tm = dest_ref.shape

    def issue(t, carry):
        for k in range(K):
            _row_copy(ys_ref.at[_token_rows(dest_ref[k, t])], gbuf.at[k, _token_rows(t)], sem).start(
                priority=k % DMA_QUEUES)
        return carry

    lax.fori_loop(0, tm, issue, 0, unroll=ISSUE_UNROLL)
    for k in range(K):
        _row_copy(ys_ref.at[pl.ds(0, tm * ROW_PIECES)], gbuf.at[k], sem).wait()
    half = ROW_PIECES * LANES
    w = w_ref[...]
    acc_a = jnp.zeros((tm, half), F32)
    acc_b = jnp.zeros((tm, half), F32)
    for k in range(K):
        a, b = _unpack_bf16_pair(_load_rows(gbuf.at[k]))
        wk = w[:, k:k + 1]
        acc_a = acc_a + wk * a
        acc_b = acc_b + wk * b
    g2 = g2_ref[0]
    o_ref[:, :half] = base_ref[:, :half] + g2[:, :half] * acc_a
    o_ref[:, half:] = base_ref[:, half:] + g2[:, half:] * acc_b


def _combine(dest, base, w_tok, g2, ys, tm, rows_per_mod):
    K, N = dest.shape
    D = base.shape[1]
    R = g2.shape[1]
    return pl.pallas_call(
        _combine_kernel,
        grid=(N // tm,),
        in_specs=[
            pl.BlockSpec((K, tm), lambda i: (0, i), memory_space=pltpu.SMEM),
            pl.BlockSpec((tm, D), lambda i: (i, 0)),
            pl.BlockSpec((tm, K), lambda i: (i, 0)),
            pl.BlockSpec((1, R, D), lambda i: (i // rows_per_mod, 0, 0)),
            pl.BlockSpec(memory_space=pl.ANY),
        ],
        out_specs=pl.BlockSpec((tm, D), lambda i: (i, 0)),
        out_shape=jax.ShapeDtypeStruct((N, D), F32),
        scratch_shapes=[pltpu.VMEM((K, tm * ROW_PIECES, LANES), jnp.uint32), pltpu.SemaphoreType.DMA],
        compiler_params=_cparams("arbitrary"),
        name="combine",
    )(dest, base, w_tok, g2, ys)


def _tile_plan(counts, n_tiles):
    T = EXPERT_TILE
    E = counts.shape[0]
    padded = (counts + T - 1) // T * T
    pad_end = jnp.cumsum(padded)
    pad_off = pad_end - padded
    idx = jnp.arange(n_tiles, dtype=jnp.int32)
    tile_start = idx * T
    blk_exp = jnp.minimum(jnp.sum((pad_end[None, :] <= tile_start[:, None]).astype(jnp.int32), axis=1), E - 1)
    used = tile_start < pad_end[-1]
    changed = jnp.concatenate([jnp.ones((1,), bool), blk_exp[1:] != blk_exp[:-1]])
    first = jnp.logical_and(used, changed)
    next_first = lax.cummin(jnp.where(first, idx, n_tiles), reverse=True)
    next_first = jnp.concatenate([next_first[1:], jnp.full((1,), n_tiles, jnp.int32)])
    nxt = jnp.where(next_first < n_tiles, blk_exp[jnp.minimum(next_first, n_tiles - 1)], -1)
    slot = (jnp.cumsum(first.astype(jnp.int32)) - 1) % 2
    n_used = (pad_end[-1] // T).reshape(1)
    as_i32 = lambda a: a.astype(jnp.int32)
    return pad_off, (as_i32(blk_exp), as_i32(first), as_i32(n_used), as_i32(nxt), as_i32(slot))


def _ret_sample_kernel(q_ref, k_ref, v_ref, sg_ref, gn_ref, dec_ref, s_ref, o_ref, so_ref):
    q = q_ref[0]
    k = k_ref[0]
    v = v_ref[0]
    gamma = dec_ref[pl.program_id(0)]
    qk = jnp.sum(q * k, axis=0, keepdims=True)
    o = qk * v
    Dh = q.shape[0]
    acc = jnp.zeros_like(v)
    for d in range(Dh):
        S = s_ref[0, d]
        acc = acc + q[d:d + 1, :] * S
        so_ref[0, d] = S * gamma + k[d:d + 1, :] * v
    o = o + acc * gamma
    mu = jnp.mean(o, axis=0, keepdims=True)
    oc = o - mu
    var = jnp.mean(oc * oc, axis=0, keepdims=True)
    o_ref[0] = oc * lax.rsqrt(var + GN_EPS) * gn_ref[0] * sg_ref[0]


def _retention_sample(qT, kT, vT, sgT, gn_col, gamma, S):
    H, Dh, Bt = qT.shape
    vec = pl.BlockSpec((1, Dh, Bt), lambda h: (h, 0, 0))
    st = pl.BlockSpec((1, Dh, Dh, Bt), lambda h: (h, 0, 0, 0))
    return pl.pallas_call(
        _ret_sample_kernel,
        grid=(H,),
        in_specs=[vec, vec, vec, vec, pl.BlockSpec((1, Dh, 1), lambda h: (h, 0, 0)),
                  pl.BlockSpec(memory_space=pltpu.SMEM), st],
        out_specs=[vec, st],
        out_shape=[jax.ShapeDtypeStruct((H, Dh, Bt), F32), jax.ShapeDtypeStruct((H, Dh, Dh, Bt), F32)],
        compiler_params=_cparams("parallel"),
        name="retention_sample",
    )(qT, kT, vT, sgT, gn_col, gamma, S)


def _attn_sample_kernel(q_ref, kn_ref, vn_ref, knT_ref, vnT_ref, kb_ref, vb_ref, o_ref, ko_ref, vo_ref):
    b = pl.program_id(0)
    H, Dh, Wn = kb_ref.shape[1], kb_ref.shape[2], kb_ref.shape[3]
    HD = H * Dh
    q = q_ref[0]
    kn = kn_ref[0]
    vn = vn_ref[0]
    row = lax.broadcasted_iota(jnp.int32, (H, HD), 0)
    col = lax.broadcasted_iota(jnp.int32, (H, HD), 1)
    own = (col // Dh) == row
    q_bd = jnp.where(own, q, 0.0)
    s_new = jnp.sum(q_bd * kn, axis=1, keepdims=True)

    bl = lax.broadcasted_iota(jnp.int32, knT_ref.shape, 1)
    k_col = jnp.sum(jnp.where(bl == b, knT_ref[...], 0.0), axis=1, keepdims=True)
    v_col = jnp.sum(jnp.where(bl == b, vnT_ref[...], 0.0), axis=1, keepdims=True)

    t = lax.broadcasted_iota(jnp.int32, (1, Wn), 1)
    last = t == (Wn - 1)
    prow = lax.broadcasted_iota(jnp.int32, (8, Wn), 0)
    q_bf = q_bd.astype(BF16)
    outs = []
    for h in range(H):
        K = kb_ref[0, h]
        V = vb_ref[0, h]
        hs = slice(h * Dh, (h + 1) * Dh)
        ko_ref[0, h] = jnp.where(last, k_col[hs], pltpu.roll(K, Wn - 1, 1))
        vo_ref[0, h] = jnp.where(last, v_col[hs], pltpu.roll(V, Wn - 1, 1))
        s = _dot(q_bf[:, hs], K.astype(BF16))[h:h + 1]
        sn = s_new[h:h + 1]
        Vb = V.astype(BF16)
        vn_h = vn[:, hs]
        lse_p, self_p = [], []
        probs = jnp.zeros((8, Wn), F32)
        for p, (window, d) in enumerate(DILATED_PATTERNS):
            valid = (((Wn - t) % d) == 0) & (t >= Wn - window)
            sm = jnp.where(valid, s, NEG_INF)
            m = jnp.maximum(jnp.max(sm, axis=1, keepdims=True), sn)
            e = jnp.exp(sm - m)
            en = jnp.exp(sn - m)
            l = jnp.sum(e, axis=1, keepdims=True) + en
            probs = jnp.where(prow == p, e / l, probs)
            self_p.append(en / l)
            lse_p.append(m + jnp.log(l))
        pv = _dot_nt(probs.astype(BF16), Vb)
        o_p = [pv[p:p + 1] + self_p[p] * vn_h for p in range(len(DILATED_PATTERNS))]
        mm = jnp.maximum(jnp.maximum(lse_p[0], lse_p[1]), lse_p[2])
        ws = [jnp.exp(x - mm) for x in lse_p]
        tot = ws[0] + ws[1] + ws[2]
        outs.append((ws[0] * o_p[0] + ws[1] * o_p[1] + ws[2] * o_p[2]) / tot)
    o_ref[0] = jnp.concatenate(outs, axis=1)


def _attention_sample(q, kn, vn, knT, vnT, k_buf, v_buf):
    Bt, H, Dh, Wn = k_buf.shape
    HD = H * Dh
    vec = pl.BlockSpec((1, 1, HD), lambda b: (b, 0, 0))
    tr = pl.BlockSpec((HD, Bt), lambda b: (0, 0))
    cache = pl.BlockSpec((1, H, Dh, Wn), lambda b: (b, 0, 0, 0))
    return pl.pallas_call(
        _attn_sample_kernel,
        grid=(Bt,),
        in_specs=[vec, vec, vec, tr, tr, cache, cache],
        out_specs=[vec, cache, cache],
        out_shape=[jax.ShapeDtypeStruct((Bt, 1, HD), F32),
                   jax.ShapeDtypeStruct(k_buf.shape, F32), jax.ShapeDtypeStruct(v_buf.shape, F32)],
        compiler_params=_cparams("parallel"),
        name="attention_sample",
    )(q, kn, vn, knT, vnT, k_buf, v_buf)


def _rope_tables(pos):
    half = HEAD_DIM // 2
    inv = ROPE_THETA ** (-jnp.arange(half, dtype=F32) / half)
    ang = pos.astype(F32)[:, None] * inv[None, :]
    cos, sin = jnp.cos(ang), jnp.sin(ang)
    cos_t = jnp.tile(cos, (1, 2 * N_RET_HEADS))
    sin_t = jnp.tile(jnp.concatenate([-sin, sin], axis=1), (1, N_RET_HEADS))
    return cos_t, sin_t


def _layer(x_prompt, x_sample, c_prompt, c_sample, state_ret, cache_k, cache_v,
           norm1, w_ada, b_ada, w_in, gn_ret, q_norm, k_norm, w_out, norm2,
           w_router, router_bias, w_exp_gate, w_exp_up, w_exp_down, w_sh_gate, w_sh_up, w_sh_down):
    B, L, D = x_prompt.shape
    Bs = x_sample.shape[0]
    H, Dh, W = N_RET_HEADS, HEAD_DIM, RET_WIDTH
    Np = B * L

    pad = (-(B + Bs)) % 8
    c_all = jnp.concatenate([c_prompt, c_sample, jnp.zeros((pad, D), F32)], axis=0)
    mod = _modulation(c_all, w_ada, b_ada)
    mod_p = [m.reshape(B, 1, D) for m in jnp.split(mod[:B], 6, axis=-1)]
    mod_s = [m.reshape(1, Bs, D) for m in jnp.split(mod[B:B + Bs], 6, axis=-1)]

    w_in_bf = w_in.astype(BF16)
    gi = jnp.arange(W) // Dh
    gmat = jnp.where(gi[:, None] == gi[None, :], 1.0 / Dh, 0.0).astype(BF16)
    qn_t = jnp.tile(q_norm, N_ATT_HEADS).reshape(1, W)
    kn_t = jnp.tile(k_norm, N_ATT_HEADS).reshape(1, W)
    n1 = norm1.reshape(1, D)
    n2 = norm2.reshape(1, D)
    gn = gn_ret.reshape(1, W)

    tm = TOKEN_TILE
    cos_p, sin_p = _rope_tables(jnp.arange(L, dtype=jnp.int32))
    cos_s, sin_s = _rope_tables(PAST_LEN + jnp.arange(1, dtype=jnp.int32))

    xp = x_prompt.reshape(Np, D)
    xs = x_sample.reshape(Bs, D)
    proj_p = _in_proj(xp, mod_p[1], mod_p[0], cos_p, sin_p, n1, qn_t, kn_t, gmat, w_in_bf, tm, L // tm, L // tm)
    proj_s = _in_proj(xs, mod_s[1], mod_s[0], cos_s, sin_s, n1, qn_t, kn_t, gmat, w_in_bf, Bs, 1, 1)
    qr, kr, vr, sg, qa, ka, va = [t.reshape(B, L, W) for t in proj_p]
    qr_s, kr_s, vr_s, sg_s, qa_s, ka_s, va_s = proj_s

    o_r, state_p = _retention_prompt(qr, kr, vr, sg, gn, gmat)
    o_a = _attention_prompt(qa, ka, va)
    keep = min(MAX_WINDOW, L)
    cache_kp = ka[:, L - keep:].reshape(B, keep, N_ATT_HEADS, Dh)
    cache_vp = va[:, L - keep:].reshape(B, keep, N_ATT_HEADS, Dh)

    log_g = jnp.log1p(-(2.0 ** (-5.0 - jnp.arange(H, dtype=F32))))
    gamma = jnp.exp(log_g).astype(F32)
    tr = lambda t: t.T.reshape(H, Dh, Bs)
    S_t = jnp.transpose(state_ret, (1, 2, 3, 0))
    orT, S_new_t = _retention_sample(tr(qr_s), tr(kr_s), tr(vr_s), tr(sg_s), gn_ret.reshape(H, Dh, 1), gamma, S_t)
    o_r_s = orT.reshape(W, Bs).T
    state_s = jnp.transpose(S_new_t, (3, 0, 1, 2))

    kb_t = jnp.transpose(cache_k, (0, 2, 3, 1))
    vb_t = jnp.transpose(cache_v, (0, 2, 3, 1))
    o_a_s, ko_t, vo_t = _attention_sample(qa_s.reshape(Bs, 1, W), ka_s.reshape(Bs, 1, W), va_s.reshape(Bs, 1, W),
                                          ka_s.T, va_s.T, kb_t, vb_t)
    cache_ks = jnp.transpose(ko_t, (0, 3, 1, 2))
    cache_vs = jnp.transpose(vo_t, (0, 3, 1, 2))

    wo = w_out.astype(BF16)
    wr_t = w_router.T
    wrh = wr_t.astype(BF16)
    wrl = (wr_t - wrh.astype(F32)).astype(BF16)
    rbias = router_bias.astype(F32).reshape(N_EXPERTS, 1)
    wsg, wsu, wsd = w_sh_gate.astype(BF16), w_sh_up.astype(BF16), w_sh_down.astype(BF16)
    zero_counts = jnp.zeros((N_EXPERTS, 1), F32)
    base_p, h2_p, eidx_p, w_p, rank_p, counts_p = _post_mixer(
        o_r.reshape(Np, W), o_a.reshape(Np, W), xp, mod_p[2], mod_p[4], mod_p[3], mod_p[5], n2, wo, wrh, wrl, rbias,
        zero_counts, wsg, wsu, wsd, tm, L // tm)
    base_s, h2_s, eidx_s, w_s, rank_s, counts = _post_mixer(
        o_r_s, o_a_s.reshape(Bs, W), xs, mod_s[2], mod_s[4], mod_s[3], mod_s[5], n2, wo, wrh, wrl, rbias,
        counts_p, wsg, wsu, wsd, Bs, 1)

    T = EXPERT_TILE
    n_tiles = ((Np + Bs) * TOP_K + N_EXPERTS * T + T - 1) // T
    pad_off, plan = _tile_plan(counts.reshape(N_EXPERTS).astype(jnp.int32), n_tiles)
    pad_off = pad_off.astype(F32).reshape(N_EXPERTS, 1)
    dest_p = _dest_rows(eidx_p, rank_p, pad_off)
    dest_s = _dest_rows(eidx_s, rank_s, pad_off)
    grouped = jnp.zeros((n_tiles * T * ROW_PIECES, LANES), jnp.uint32)
    grouped = _dispatch(dest_p, h2_p, grouped, tm)
    grouped = _dispatch(dest_s, h2_s, grouped, Bs)
    ys = _expert_matmul(grouped, plan, w_exp_gate, w_exp_up, w_exp_down)
    tc = COMBINE_TILE
    y_p = _combine(dest_p, base_p, w_p.T, mod_p[5], ys, tc, L // tc)
    y_s = _combine(dest_s, base_s, w_s.T, mod_s[5], ys, Bs, 1)

    return (y_p.reshape(B, L, D), y_s.reshape(Bs, 1, D), state_p, cache_kp, cache_vp, state_s, cache_ks, cache_vs)


def kernel(x_prompt, x_sample, c_prompt, c_sample, state_ret, cache_win_k, cache_win_v, norm1, w_ada, b_ada, w_in,
           gn_ret, q_norm, k_norm, w_out, norm2, w_router, router_bias, w_exp_gate, w_exp_up, w_exp_down,
           w_sh_gate, w_sh_up, w_sh_down):
    assert w_in.shape[0] == 1, "single-layer step"
    assert x_sample.shape[1] == 1, "one new token per sample sequence"
    outs = _layer(x_prompt, x_sample, c_prompt, c_sample, state_ret[0], cache_win_k[0], cache_win_v[0],
                  norm1[0], w_ada[0], b_ada[0], w_in[0], gn_ret[0], q_norm[0], k_norm[0], w_out[0], norm2[0],
                  w_router[0], router_bias[0], w_exp_gate[0], w_exp_up[0], w_exp_down[0],
                  w_sh_gate[0], w_sh_up[0], w_sh_down[0])
    yp, ys, sp, kp, vp, ss, ksm, vsm = outs
    return (yp, ys, sp[None], kp[None], vp[None], ss[None], ksm[None], vsm[None])
```

```python
import functools

import jax
import jax.numpy as jnp
from jax import lax
from jax.experimental import pallas as pl
from jax.experimental.pallas import tpu as pltpu
from jax.experimental.pallas import tpu_sc as plsc

HEAD_DIM = 64
N_RET_HEADS = 8
N_ATT_HEADS = 8
RET_WIDTH = N_RET_HEADS * HEAD_DIM
ATT_WIDTH = N_ATT_HEADS * HEAD_DIM
RET_CHUNK = 128
DILATED_PATTERNS = ((128, 1), (512, 4), (2048, 16))
MAX_WINDOW = 2048
PAST_LEN = 8192
ROPE_THETA = 10000.0
N_EXPERTS = 256
TOP_K = 8
N_EXPERT_GROUPS = 8
TOPK_GROUPS = 4
ROUTED_SCALE = 2.5
NORM_EPS = 1e-6
GN_EPS = 1e-5
NEG_INF = -1e30

F32 = jnp.float32
BF16 = jnp.bfloat16

VMEM_LIMIT_BYTES = 56 * 1024 * 1024
EXPERT_TILE = 256
TOKEN_TILE = 512
COMBINE_TILE = 256
GATHER_WINDOW = 128
DMA_QUEUES = 2
ISSUE_UNROLL = 4
RET_STEP_CHUNKS = 4
EXPERT_ROW_SLOTS = 3
DEST_TILE = 2048
ATTN_UNROLL = 8


def _cparams(*sem):
    return pltpu.CompilerParams(dimension_semantics=sem, vmem_limit_bytes=VMEM_LIMIT_BYTES)


def _dot(a, b):
    return jnp.dot(a, b, preferred_element_type=F32)


def _dot_nt(a, b):
    return lax.dot_general(a, b, (((1,), (1,)), ((), ())), preferred_element_type=F32)


def _dot_tn(a, b):
    return lax.dot_general(a, b, (((0,), (0,)), ((), ())), preferred_element_type=F32)


def _mod_kernel(c_ref, w_ref, b_ref, o_ref):
    c = c_ref[...]
    a = (c * jax.nn.sigmoid(c)).astype(BF16)
    o_ref[...] = _dot(a, w_ref[...].astype(BF16)) + b_ref[...]


def _modulation(c, w_ada, b_ada):
    R, D = c.shape
    n_out = w_ada.shape[1]
    tn = 1024
    return pl.pallas_call(
        _mod_kernel,
        grid=(n_out // tn,),
        in_specs=[
            pl.BlockSpec((R, D), lambda j: (0, 0)),
            pl.BlockSpec((D, tn), lambda j: (0, j)),
            pl.BlockSpec((1, tn), lambda j: (0, j)),
        ],
        out_specs=pl.BlockSpec((R, tn), lambda j: (0, j)),
        out_shape=jax.ShapeDtypeStruct((R, n_out), F32),
        compiler_params=_cparams("parallel"),
        name="modulation",
    )(c, w_ada, b_ada.reshape(1, n_out))


def _swap_halves(x, first_half):
    n = x.shape[-1]
    return jnp.where(first_half, pltpu.roll(x, n - HEAD_DIM // 2, 1), pltpu.roll(x, HEAD_DIM // 2, 1))


def _head_mean(x, g):
    hi = x.astype(BF16)
    lo = (x - hi.astype(F32)).astype(BF16)
    return _dot(hi, g) + _dot(lo, g)


def _in_proj_kernel(x_ref, sc_ref, sh_ref, cos_ref, sin_ref, n1_ref, qn_ref, kn_ref, g_ref, w_ref,
                    qr_ref, kr_ref, vr_ref, sg_ref, qa_ref, ka_ref, va_ref):
    x = x_ref[...]
    ms = jnp.mean(x * x, axis=-1, keepdims=True)
    h = x * lax.rsqrt(ms + NORM_EPS) * n1_ref[...]
    h = (h * (1.0 + sc_ref[0]) + sh_ref[0]).astype(BF16)
    cos = cos_ref[...]
    sin = sin_ref[...]
    W = RET_WIDTH
    lane = lax.broadcasted_iota(jnp.int32, (1, W), 1)
    first_half = (lane % HEAD_DIM) < (HEAD_DIM // 2)

    def proj(c):
        return _dot(h, w_ref[:, c * W:(c + 1) * W])

    def rot(t):
        return t * cos + _swap_halves(t, first_half) * sin

    def qk_norm(t, gain):
        return t * lax.rsqrt(_head_mean(t * t, g_ref[...]) + NORM_EPS) * gain

    qr_ref[...] = rot(proj(0))
    kr_ref[...] = rot(proj(1)) * (HEAD_DIM ** -0.5)
    vr_ref[...] = proj(2)
    gr = proj(3)
    sg_ref[...] = gr * jax.nn.sigmoid(gr)
    qa_ref[...] = rot(qk_norm(proj(4), qn_ref[...])) * (HEAD_DIM ** -0.5)
    ka_ref[...] = rot(qk_norm(proj(5), kn_ref[...]))
    va_ref[...] = proj(6)


def _in_proj(x, sc, sh, cos, sin, norm1, q_norm_t, k_norm_t, gmat, w_in_bf, tm, rows_per_mod, pos_blocks):
    N, D = x.shape
    W = RET_WIDTH
    R = sc.shape[1]
    mod_spec = pl.BlockSpec((1, R, D), lambda i: (i // rows_per_mod, 0, 0))
    pos_spec = pl.BlockSpec((cos.shape[0] // pos_blocks, W), lambda i: (i % pos_blocks, 0))
    const = lambda shape: pl.BlockSpec(shape, lambda i: (0,) * len(shape))
    out_spec = pl.BlockSpec((tm, W), lambda i: (i, 0))
    return pl.pallas_call(
        _in_proj_kernel,
        grid=(N // tm,),
        in_specs=[
            pl.BlockSpec((tm, D), lambda i: (i, 0)),
            mod_spec, mod_spec, pos_spec, pos_spec,
            const((1, D)), const((1, W)), const((1, W)), const((W, W)), const(w_in_bf.shape),
        ],
        out_specs=[out_spec] * 7,
        out_shape=[jax.ShapeDtypeStruct((N, W), F32)] * 7,
        compiler_params=_cparams("parallel"),
        name="in_proj",
    )(x, sc, sh, cos, sin, norm1, q_norm_t, k_norm_t, gmat, w_in_bf)


def _ret_kernel(q_ref, k_ref, v_ref, sg_ref, dm_ref, qd_ref, kd_ref, sd_ref, bd_ref, gm_ref, gn_ref, o_ref, s_ref):
    @pl.when(pl.program_id(1) == 0)
    def _():
        s_ref[...] = jnp.zeros_like(s_ref)

    C = dm_ref.shape[2]
    Dh = HEAD_DIM
    P2 = 2 * Dh
    n_pairs = N_RET_HEADS // 2
    lane = lax.broadcasted_iota(jnp.int32, (1, P2), 1)
    head0 = lane < Dh
    zero = jnp.zeros((Dh, Dh), F32)
    state = []
    for p in range(n_pairs):
        top = jnp.concatenate([s_ref[0, 2 * p], zero], axis=1)
        bot = jnp.concatenate([zero, s_ref[0, 2 * p + 1]], axis=1)
        state.append(jnp.concatenate([top, bot], axis=0))
    bd = bd_ref[...]
    gm = gm_ref[...]
    for j in range(q_ref.shape[1] // C):
        rows = slice(j * C, (j + 1) * C)
        for p in range(n_pairs):
            sl = slice(P2 * p, P2 * (p + 1))
            q = q_ref[0, rows, sl]
            k = k_ref[0, rows, sl]
            vb = v_ref[0, rows, sl].astype(BF16)
            kdb = (k * kd_ref[:, sl]).astype(BF16)
            q2 = jnp.concatenate([jnp.where(head0, q, 0.0), jnp.where(head0, 0.0, q)], axis=0).astype(BF16)
            sc = _dot_nt(q2, k.astype(BF16)) * dm_ref[p]
            o2 = _dot(sc.astype(BF16), vb)
            S = state[p]
            o = jnp.where(head0, o2[:C], o2[C:]) + _dot(q.astype(BF16), S.astype(BF16)) * qd_ref[:, sl]
            state[p] = S * sd_ref[p] + _dot_tn(kdb, vb) * bd
            o_ref[0, rows, sl] = o
    for p in range(n_pairs):
        s_ref[0, 2 * p] = state[p][:Dh, :Dh]
        s_ref[0, 2 * p + 1] = state[p][Dh:, Dh:]
    o = o_ref[0]
    oc = o - _head_mean(o, gm)
    var = _head_mean(oc * oc, gm)
    o_ref[0] = oc * lax.rsqrt(var + GN_EPS) * gn_ref[...] * sg_ref[0]


def _decay_tables(C):
    H = N_RET_HEADS
    log_g = jnp.log1p(-(2.0 ** (-5.0 - jnp.arange(H, dtype=F32))))
    i = jnp.arange(C)
    diff = i[:, None] - i[None, :]
    dmask = jnp.where(diff[None] >= 0, jnp.exp(log_g[:, None, None] * jnp.maximum(diff, 0)[None]), 0.0).astype(F32)
    q_decay = jnp.exp(log_g[None, :] * (i[:, None] + 1)).astype(F32)
    k_decay = jnp.exp(log_g[None, :] * (C - 1 - i)[:, None]).astype(F32)
    s_decay = jnp.exp(log_g * C).astype(F32)
    return dmask, q_decay, k_decay, s_decay


def _retention_prompt(q, k, v, sg, gn, gm):
    B, L, W = q.shape
    C = RET_CHUNK
    H, Dh = N_RET_HEADS, HEAD_DIM
    dmask, q_decay, k_decay, s_decay = _decay_tables(C)
    qd = jnp.repeat(q_decay, Dh, axis=1)
    kd = jnp.repeat(k_decay, Dh, axis=1)
    P2 = 2 * Dh
    pair_head = jnp.arange(P2) // Dh
    bd = (pair_head[:, None] == pair_head[None, :]).astype(F32)
    sd = s_decay.reshape(H // 2, 2)[:, pair_head][:, :, None] * bd[None]
    rows = C * RET_STEP_CHUNKS
    blk = pl.BlockSpec((1, rows, W), lambda b, c: (b, c, 0))
    const = lambda shape: pl.BlockSpec(shape, lambda b, c: (0,) * len(shape))
    return pl.pallas_call(
        _ret_kernel,
        grid=(B, L // rows),
        in_specs=[blk, blk, blk, blk, const((H // 2, 2 * C, C)), const((C, W)), const((C, W)),
                  const((H // 2, P2, P2)), const((P2, P2)), const((W, W)), const((1, W))],
        out_specs=[blk, pl.BlockSpec((1, H, Dh, Dh), lambda b, c: (b, 0, 0, 0))],
        out_shape=[jax.ShapeDtypeStruct((B, L, W), F32), jax.ShapeDtypeStruct((B, H, Dh, Dh), F32)],
        compiler_params=_cparams("parallel", "arbitrary"),
        name="retention_prompt",
    )(q, k, v, sg, dmask.reshape(H // 2, 2 * C, C), qd, kd, sd, bd, gm, gn)


def _attn_kernel(q_ref, k_ref, v_ref, o_ref, oacc, lacc):
    L = q_ref.shape[1]
    P2 = q_ref.shape[2]
    lane = lax.broadcasted_iota(jnp.int32, (1, P2), 1)
    head0 = lane < HEAD_DIM

    for p, (window, d) in enumerate(DILATED_PATTERNS):
        band = window // d
        nb = L // (d * band)
        qi = lax.broadcasted_iota(jnp.int32, (2 * band, 2 * band), 0) % band
        ki = lax.broadcasted_iota(jnp.int32, (2 * band, 2 * band), 1)
        dist = qi + band - ki
        in_band = (dist >= 0) & (dist <= band)
        bias_any = jnp.where(in_band, 0.0, NEG_INF)
        bias_first = jnp.where(in_band & (ki >= band), 0.0, NEG_INF)

        def block(idx, d=d, band=band, nb=nb, p=p, bias_any=bias_any, bias_first=bias_first):
            r = idx // nb
            n = idx % nb
            qs = r + d * band * n
            ps = jnp.maximum(qs - d * band, r)
            cur = pl.ds(qs, band, stride=d) if d > 1 else pl.ds(qs, band)
            prev = pl.ds(ps, band, stride=d) if d > 1 else pl.ds(ps, band)
            qb = q_ref[0, cur, :]
            kk = jnp.concatenate([k_ref[0, prev, :], k_ref[0, cur, :]], axis=0).astype(BF16)
            vv = jnp.concatenate([v_ref[0, prev, :], v_ref[0, cur, :]], axis=0).astype(BF16)
            q2 = jnp.concatenate([jnp.where(head0, qb, 0.0), jnp.where(head0, 0.0, qb)], axis=0).astype(BF16)
            s = _dot_nt(q2, kk) + jnp.where(n > 0, bias_any, bias_first)
            m = jnp.max(jnp.maximum(s[:, :band], s[:, band:]), axis=-1, keepdims=True)
            e = jnp.exp(s - m)
            l = jnp.sum(e[:, :band] + e[:, band:], axis=-1, keepdims=True)
            o = _dot(e.astype(BF16), vv) * (1.0 / l)
            lse = m + jnp.log(l)
            oacc[p, cur, :] = jnp.where(head0, o[:band], o[band:])
            lacc[p, cur, :] = jnp.where(head0, lse[:band], lse[band:])

        def body(i, carry, block=block):
            for u in range(ATTN_UNROLL):
                block(i * ATTN_UNROLL + u)
            return carry

        lax.fori_loop(0, d * nb // ATTN_UNROLL, body, 0)

    rows = 512

    def combine(i, carry):
        sl = pl.ds(pl.multiple_of(i * rows, rows), rows)
        l0, l1, l2 = lacc[0, sl, :], lacc[1, sl, :], lacc[2, sl, :]
        m = jnp.maximum(jnp.maximum(l0, l1), l2)
        w0, w1, w2 = jnp.exp(l0 - m), jnp.exp(l1 - m), jnp.exp(l2 - m)
        tot = w0 + w1 + w2
        o_ref[0, sl, :] = (w0 * oacc[0, sl, :] + w1 * oacc[1, sl, :] + w2 * oacc[2, sl, :]) / tot
        return carry

    lax.fori_loop(0, L // rows, combine, 0)


def _attention_prompt(q, k, v):
    B, L, W = q.shape
    P2 = 2 * HEAD_DIM
    blk = pl.BlockSpec((1, L, P2), lambda b, hp: (b, 0, hp))
    return pl.pallas_call(
        _attn_kernel,
        grid=(B, W // P2),
        in_specs=[blk, blk, blk],
        out_specs=blk,
        out_shape=jax.ShapeDtypeStruct((B, L, W), F32),
        scratch_shapes=[pltpu.VMEM((len(DILATED_PATTERNS), L, P2), F32)] * 2,
        compiler_params=_cparams("parallel", "parallel"),
        name="attention_prompt",
    )(q, k, v)


def _pack_bf16_pair(a, b):
    ua = pltpu.bitcast(a.astype(BF16).astype(F32), jnp.uint32)
    ub = pltpu.bitcast(b.astype(BF16).astype(F32), jnp.uint32)
    return ua | (ub >> 16)


def _unpack_bf16_pair(p):
    a = pltpu.bitcast(p & jnp.uint32(0xFFFF0000), F32)
    b = pltpu.bitcast(p << 16, F32)
    return a, b


ROUTE_LANES = 128
LANES = 128


ROW_PIECES = 4


def _store_rows(ref, packed):
    n = packed.shape[0]
    for c in range(ROW_PIECES):
        ref[pl.ds(c, n, stride=ROW_PIECES), :] = packed[:, c * LANES:(c + 1) * LANES]


def _load_rows(ref):
    n = ref.shape[0] // ROW_PIECES
    return jnp.concatenate([ref[pl.ds(c, n, stride=ROW_PIECES), :] for c in range(ROW_PIECES)], axis=1)


def _token_rows(t):
    return pl.ds(pl.multiple_of(t * ROW_PIECES, ROW_PIECES), ROW_PIECES)


def _route_chunk(s, bias, carry):
    E, n = s.shape
    G, GS = N_EXPERT_GROUPS, E // N_EXPERT_GROUPS
    NEG = -jnp.inf
    choice = s + bias
    row = lax.broadcasted_iota(jnp.int32, (E, n), 0).astype(F32)
    lrow = row[:GS]
    gs_rows = []
    for g in range(G):
        c = choice[g * GS:(g + 1) * GS]
        m1 = jnp.max(c, axis=0, keepdims=True)
        i1 = jnp.min(jnp.where(c == m1, lrow, float(GS)), axis=0, keepdims=True)
        m2 = jnp.max(jnp.where(lrow == i1, NEG, c), axis=0, keepdims=True)
        gs_rows.append(m1 + m2)
    gs = jnp.concatenate(gs_rows, axis=0)
    grow = lax.broadcasted_iota(jnp.int32, (G, n), 0).astype(F32)
    gsel = jnp.zeros((G, n), F32)
    for _ in range(TOPK_GROUPS):
        gm = jnp.max(gs, axis=0, keepdims=True)
        gi = jnp.min(jnp.where(gs == gm, grow, float(G)), axis=0, keepdims=True)
        hit = grow == gi
        gsel = jnp.where(hit, 1.0, gsel)
        gs = jnp.where(hit, NEG, gs)
    emask = jnp.concatenate([jnp.broadcast_to(gsel[g:g + 1], (GS, n)) for g in range(G)], axis=0)
    masked = jnp.where(emask > 0.5, choice, NEG)
    ids, sks = [], []
    member = jnp.zeros((E, n), F32)
    for _ in range(TOP_K):
        mk = jnp.max(masked, axis=0, keepdims=True)
        ik = jnp.min(jnp.where(masked == mk, row, float(E)), axis=0, keepdims=True)
        sel = row == ik
        sks.append(jnp.sum(jnp.where(sel, s, 0.0), axis=0, keepdims=True))
        masked = jnp.where(sel, NEG, masked)
        member = jnp.where(sel, 1.0, member)
        ids.append(ik)
    eidx = jnp.concatenate(ids, axis=0)
    sk = jnp.concatenate(sks, axis=0)
    w = sk / jnp.sum(sk, axis=0, keepdims=True) * ROUTED_SCALE
    ti = lax.broadcasted_iota(jnp.int32, (n, n), 0)
    tj = lax.broadcasted_iota(jnp.int32, (n, n), 1)
    upper = jnp.where(ti <= tj, 1.0, 0.0).astype(BF16)
    incl = _dot(member.astype(BF16), upper)
    rank_dense = carry + incl - member
    ranks = [jnp.sum(jnp.where(row == ids[k], rank_dense, 0.0), axis=0, keepdims=True) for k in range(TOP_K)]
    return eidx, w, jnp.concatenate(ranks, axis=0), carry + incl[:, n - 1:n]


def _post_kernel(or_ref, oa_ref, x_ref, g1_ref, sc_ref, sh_ref, g2_ref, n2_ref, wo_ref, wrh_ref, wrl_ref, rb_ref,
                 cin_ref, wsg_ref, wsu_ref, wsd_ref, base_ref, h2_ref, eidx_ref, w_ref, rank_ref, cnt_ref):
    @pl.when(pl.program_id(0) == 0)
    def _():
        cnt_ref[...] = cin_ref[...]

    W = RET_WIDTH
    mix = _dot(or_ref[...].astype(BF16), wo_ref[:W, :]) + _dot(oa_ref[...].astype(BF16), wo_ref[W:, :])
    x2 = x_ref[...] + g1_ref[0] * mix
    ms = jnp.mean(x2 * x2, axis=-1, keepdims=True)
    h2 = x2 * lax.rsqrt(ms + NORM_EPS) * n2_ref[...]
    h2 = h2 * (1.0 + sc_ref[0]) + sh_ref[0]
    hb = h2.astype(BF16)
    hl = (h2 - hb.astype(F32)).astype(BF16)
    half = h2.shape[1] // 2
    _store_rows(h2_ref, _pack_bf16_pair(h2[:, :half], h2[:, half:]))
    g = _dot(hb, wsg_ref[...])
    u = _dot(hb, wsu_ref[...])
    shared = _dot((g * jax.nn.sigmoid(g) * u).astype(BF16), wsd_ref[...])
    base_ref[...] = x2 + g2_ref[0] * shared

    logits = _dot_nt(wrh_ref[...], hb) + (_dot_nt(wrh_ref[...], hl) + _dot_nt(wrl_ref[...], hb))
    s = jax.nn.sigmoid(logits)
    carry = cnt_ref[...]
    n = ROUTE_LANES
    for j in range(s.shape[1] // n):
        sl = slice(j * n, (j + 1) * n)
        eidx, w, rank, carry = _route_chunk(s[:, sl], rb_ref[...], carry)
        eidx_ref[:, sl] = eidx.astype(jnp.int32)
        w_ref[:, sl] = w
        rank_ref[:, sl] = rank.astype(jnp.int32)
    cnt_ref[...] = carry


def _post_mixer(o_r, o_a, x, g1, sc2, sh2, g2, norm2, wo, wrh_t, wrl_t, rbias, counts_in, wsg, wsu, wsd, tm,
                rows_per_mod):
    N, D = x.shape
    assert D // 2 == ROW_PIECES * LANES
    W = RET_WIDTH
    E = wrh_t.shape[0]
    R = g1.shape[1]
    mod_spec = pl.BlockSpec((1, R, D), lambda i: (i // rows_per_mod, 0, 0))
    const = lambda a: pl.BlockSpec(a.shape, lambda i: (0,) * a.ndim)
    tok = lambda rows: pl.BlockSpec((rows, tm), lambda i: (0, i))
    return pl.pallas_call(
        _post_kernel,
        grid=(N // tm,),
        in_specs=[
            pl.BlockSpec((tm, W), lambda i: (i, 0)),
            pl.BlockSpec((tm, W), lambda i: (i, 0)),
            pl.BlockSpec((tm, D), lambda i: (i, 0)),
            mod_spec, mod_spec, mod_spec, mod_spec,
            const(norm2), const(wo), const(wrh_t), const(wrl_t), const(rbias), const(counts_in),
            const(wsg), const(wsu), const(wsd),
        ],
        out_specs=[
            pl.BlockSpec((tm, D), lambda i: (i, 0)),
            pl.BlockSpec((tm * ROW_PIECES, LANES), lambda i: (i, 0)),
            tok(TOP_K), tok(TOP_K), tok(TOP_K),
            pl.BlockSpec((E, 1), lambda i: (0, 0)),
        ],
        out_shape=[
            jax.ShapeDtypeStruct((N, D), F32),
            jax.ShapeDtypeStruct((N * ROW_PIECES, LANES), jnp.uint32),
            jax.ShapeDtypeStruct((TOP_K, N), jnp.int32),
            jax.ShapeDtypeStruct((TOP_K, N), F32),
            jax.ShapeDtypeStruct((TOP_K, N), jnp.int32),
            jax.ShapeDtypeStruct((E, 1), F32),
        ],
        compiler_params=_cparams("arbitrary"),
        name="post_mixer",
    )(o_r, o_a, x, g1, sc2, sh2, g2, norm2, wo, wrh_t, wrl_t, rbias, counts_in, wsg, wsu, wsd)


def _expert_kernel(be_ref, first_ref, nused_ref, nxt_ref, slot_ref, x_hbm, wg_hbm, wu_hbm, wd_hbm, y_ref,
                   xbuf, wg_f, wu_f, wd_f, wg_s, wu_s, wd_s, xsem, sem):
    i = pl.program_id(0)
    is_first = first_ref[i] == 1
    n_used = nused_ref[0]
    used = i < n_used
    tile_rows = xbuf.shape[1]
    x_slots = xbuf.shape[0]

    def weight_copies(e, s):
        return (pltpu.make_async_copy(wg_hbm.at[e], wg_f.at[s], sem.at[s, 0]),
                pltpu.make_async_copy(wu_hbm.at[e], wu_f.at[s], sem.at[s, 1]),
                pltpu.make_async_copy(wd_hbm.at[e], wd_f.at[s], sem.at[s, 2]))

    def x_copy(tile):
        s = tile % x_slots
        rows = pl.ds(pl.multiple_of(tile * tile_rows, tile_rows), tile_rows)
        return pltpu.make_async_copy(x_hbm.at[rows], xbuf.at[s], xsem.at[s])

    @pl.when(i == 0)
    def _():
        for t in range(x_slots - 1):
            @pl.when(t < n_used)
            def _():
                x_copy(t).start()

    @pl.when(used)
    def _():
        x_copy(i).wait()

    @pl.when(i + (x_slots - 1) < n_used)
    def _():
        x_copy(i + (x_slots - 1)).start()

    def compute():
        xa, xb = _unpack_bf16_pair(_load_rows(xbuf.at[i % x_slots]))
        half = xa.shape[1]
        x = jnp.concatenate([xa.astype(BF16), xb.astype(BF16)], axis=1)
        g = _dot(x, wg_s[...])
        u = _dot(x, wu_s[...])
        hmid = (g * jax.nn.sigmoid(g) * u).astype(BF16)
        _store_rows(y_ref, _pack_bf16_pair(_dot(hmid, wd_s[:, :half]), _dot(hmid, wd_s[:, half:])))

    @pl.when(i == 0)
    def _():
        for c in weight_copies(be_ref[0], 0):
            c.start()

    @pl.when(is_first)
    def _():
        s = slot_ref[i]
        for c in weight_copies(be_ref[i], s):
            c.wait()

        @pl.when(nxt_ref[i] >= 0)
        def _():
            for c in weight_copies(nxt_ref[i], 1 - s):
                c.start()

        wg_s[...] = wg_f[s].astype(BF16)
        wu_s[...] = wu_f[s].astype(BF16)
        wd_s[...] = wd_f[s].astype(BF16)
        compute()

    @pl.when(jnp.logical_and(jnp.logical_not(is_first), used))
    def _():
        compute()

    @pl.when(jnp.logical_not(used))
    def _():
        y_ref[...] = jnp.zeros_like(y_ref)


def _expert_matmul(xs, plan, w_gate, w_up, w_down):
    P = xs.shape[0] // ROW_PIECES
    T = EXPERT_TILE
    D, F = w_gate.shape[1], w_gate.shape[2]
    row = pl.BlockSpec((T * ROW_PIECES, LANES), lambda i, *_: (i, 0))
    hbm = pl.BlockSpec(memory_space=pl.ANY)
    return pl.pallas_call(
        _expert_kernel,
        grid_spec=pltpu.PrefetchScalarGridSpec(
            num_scalar_prefetch=5,
            grid=(P // T,),
            in_specs=[hbm, hbm, hbm, hbm],
            out_specs=row,
            scratch_shapes=[
                pltpu.VMEM((EXPERT_ROW_SLOTS, T * ROW_PIECES, LANES), jnp.uint32),
                pltpu.VMEM((2, D, F), F32), pltpu.VMEM((2, D, F), F32), pltpu.VMEM((2, F, D), F32),
                pltpu.VMEM((D, F), BF16), pltpu.VMEM((D, F), BF16), pltpu.VMEM((F, D), BF16),
                pltpu.SemaphoreType.DMA((EXPERT_ROW_SLOTS,)),
                pltpu.SemaphoreType.DMA((2, 3)),
            ],
        ),
        out_shape=jax.ShapeDtypeStruct(xs.shape, jnp.uint32),
        compiler_params=_cparams("arbitrary"),
        name="routed_experts",
    )(*plan, xs, w_gate, w_up, w_down)


def _dest_kernel(eidx_ref, rank_ref, off_ref, dest_ref):
    E = off_ref.shape[0]
    n = eidx_ref.shape[1]
    row = lax.broadcasted_iota(jnp.int32, (E, n), 0)
    off = off_ref[...]
    rows = []
    for k in range(eidx_ref.shape[0]):
        hit = row == eidx_ref[k:k + 1, :]
        rows.append(jnp.sum(jnp.where(hit, off, 0.0), axis=0, keepdims=True))
    dest_ref[...] = rank_ref[...] + jnp.concatenate(rows, axis=0).astype(jnp.int32)


def _dest_rows(eidx, rank, pad_off):
    K, N = eidx.shape
    tn = min(N, DEST_TILE)
    blk = pl.BlockSpec((K, tn), lambda i: (0, i))
    return pl.pallas_call(
        _dest_kernel,
        grid=(N // tn,),
        in_specs=[blk, blk, pl.BlockSpec(pad_off.shape, lambda i: (0, 0))],
        out_specs=blk,
        out_shape=jax.ShapeDtypeStruct((K, N), jnp.int32),
        compiler_params=_cparams("parallel"),
        name="dest_rows",
    )(eidx, rank, pad_off)


def _row_copy(src, dst, sem):
    return pltpu.make_async_copy(src, dst, sem)


def _dispatch_kernel(dest_ref, h_ref, xs_in_ref, xs_ref, sem):
    del xs_in_ref
    K, tm = dest_ref.shape

    def issue(t, carry):
        for k in range(K):
            _row_copy(h_ref.at[_token_rows(t)], xs_ref.at[_token_rows(dest_ref[k, t])], sem).start(
                priority=k % DMA_QUEUES)
        return carry

    lax.fori_loop(0, tm, issue, 0, unroll=ISSUE_UNROLL)
    for _ in range(K):
        _row_copy(h_ref, xs_ref.at[pl.ds(0, tm * ROW_PIECES)], sem).wait()


def _dispatch(dest, h2p, xs, tm):
    K, N = dest.shape
    return pl.pallas_call(
        _dispatch_kernel,
        grid=(N // tm,),
        in_specs=[
            pl.BlockSpec((K, tm), lambda i: (0, i), memory_space=pltpu.SMEM),
            pl.BlockSpec((tm * ROW_PIECES, LANES), lambda i: (i, 0)),
            pl.BlockSpec(memory_space=pl.ANY),
        ],
        out_specs=pl.BlockSpec(memory_space=pl.ANY),
        out_shape=jax.ShapeDtypeStruct(xs.shape, xs.dtype),
        scratch_shapes=[pltpu.SemaphoreType.DMA],
        input_output_aliases={2: 0},
        compiler_params=_cparams("arbitrary"),
        name="dispatch",
    )(dest, h2p, xs)


def _gather_rows(src, idx):
    M = idx.shape[1]
    mesh = plsc.VectorSubcoreMesh(core_axis_name="core", subcore_axis_name="subcore")

    @pl.kernel(out_type=jax.ShapeDtypeStruct((M, src.shape[1]), src.dtype), mesh=mesh, scratch_types=[])
    def gather_kernel(src_hbm, idx_hbm, out_hbm):
        def body(idx_vmem, out_vmem):
            pltpu.sync_copy(src_hbm.at[idx_vmem.at[0]], out_vmem)

        pltpu.emit_pipeline(
            body,
            grid=(M // GATHER_WINDOW,),
            in_specs=[pl.BlockSpec((1, GATHER_WINDOW), index_map=lambda i: (0, i))],
            out_specs=[pl.BlockSpec((GATHER_WINDOW, src.shape[1]), index_map=lambda i: (i, 0))],
            core_axis_name=("core", "subcore"),
            dimension_semantics=(pltpu.PARALLEL,),
        )(idx_hbm, out_hbm)

    return gather_kernel(src, idx)


def _combine_kernel(base_ref, w_ref, g2_ref, rows_ref, o_ref):
    tm = base_ref.shape[0]
    per_token = TOP_K * ROW_PIECES
    half = ROW_PIECES * LANES
    w = w_ref[...]
    acc_a = jnp.zeros((tm, half), F32)
    acc_b = jnp.zeros((tm, half), F32)
    for k in range(TOP_K):
        packed = jnp.concatenate(
            [rows_ref[pl.ds(k * ROW_PIECES + c, tm, stride=per_token), :] for c in range(ROW_PIECES)], axis=1)
        a, b = _unpack_bf16_pair(packed)
        wk = w[:, k:k + 1]
        acc_a = acc_a + wk * a
        acc_b = acc_b + wk * b
    g2 = g2_ref[0]
    o_ref[:, :half] = base_ref[:, :half] + g2[:, :half] * acc_a
    o_ref[:, half:] = base_ref[:, half:] + g2[:, half:] * acc_b


def _combine(base, w_tok, g2, gathered, first_block, tm, rows_per_mod):
    N, D = base.shape
    K = w_tok.shape[1]
    R = g2.shape[1]
    return pl.pallas_call(
        _combine_kernel,
        grid=(N // tm,),
        in_specs=[
            pl.BlockSpec((tm, D), lambda i: (i, 0)),
            pl.BlockSpec((tm, K), lambda i: (i, 0)),
            pl.BlockSpec((1, R, D), lambda i: (i // rows_per_mod, 0, 0)),
            pl.BlockSpec((tm * K * ROW_PIECES, LANES), lambda i: (first_block + i, 0)),
        ],
        out_specs=pl.BlockSpec((tm, D), lambda i: (i, 0)),
        out_shape=jax.ShapeDtypeStruct((N, D), F32),
        compiler_params=_cparams("parallel"),
        name="combine",
    )(base, w_tok, g2, gathered)


def _tile_plan(counts, n_tiles):
    T = EXPERT_TILE
    E = counts.shape[0]
    padded = (counts + T - 1) // T * T
    pad_end = jnp.cumsum(padded)
    pad_off = pad_end - padded
    idx = jnp.arange(n_tiles, dtype=jnp.int32)
    tile_start = idx * T
    blk_exp = jnp.minimum(jnp.sum((pad_end[None, :] <= tile_start[:, None]).astype(jnp.int32), axis=1), E - 1)
    used = tile_start < pad_end[-1]
    changed = jnp.concatenate([jnp.ones((1,), bool), blk_exp[1:] != blk_exp[:-1]])
    first = jnp.logical_and(used, changed)
    next_first = lax.cummin(jnp.where(first, idx, n_tiles), reverse=True)
    next_first = jnp.concatenate([next_first[1:], jnp.full((1,), n_tiles, jnp.int32)])
    nxt = jnp.where(next_first < n_tiles, blk_exp[jnp.minimum(next_first, n_tiles - 1)], -1)
    slot = (jnp.cumsum(first.astype(jnp.int32)) - 1) % 2
    n_used = (pad_end[-1] // T).reshape(1)
    as_i32 = lambda a: a.astype(jnp.int32)
    return pad_off, (as_i32(blk_exp), as_i32(first), as_i32(n_used), as_i32(nxt), as_i32(slot))


def _ret_sample_kernel(q_ref, k_ref, v_ref, sg_ref, gn_ref, dec_ref, s_ref, o_ref, so_ref):
    q = q_ref[0]
    k = k_ref[0]
    v = v_ref[0]
    gamma = dec_ref[pl.program_id(0)]
    qk = jnp.sum(q * k, axis=0, keepdims=True)
    o = qk * v
    Dh = q.shape[0]
    acc = jnp.zeros_like(v)
    for d in range(Dh):
        S = s_ref[0, d]
        acc = acc + q[d:d + 1, :] * S
        so_ref[0, d] = S * gamma + k[d:d + 1, :] * v
    o = o + acc * gamma
    mu = jnp.mean(o, axis=0, keepdims=True)
    oc = o - mu
    var = jnp.mean(oc * oc, axis=0, keepdims=True)
    o_ref[0] = oc * lax.rsqrt(var + GN_EPS) * gn_ref[0] * sg_ref[0]


def _retention_sample(qT, kT, vT, sgT, gn_col, gamma, S):
    H, Dh, Bt = qT.shape
    vec = pl.BlockSpec((1, Dh, Bt), lambda h: (h, 0, 0))
    st = pl.BlockSpec((1, Dh, Dh, Bt), lambda h: (h, 0, 0, 0))
    return pl.pallas_call(
        _ret_sample_kernel,
        grid=(H,),
        in_specs=[vec, vec, vec, vec, pl.BlockSpec((1, Dh, 1), lambda h: (h, 0, 0)),
                  pl.BlockSpec(memory_space=pltpu.SMEM), st],
        out_specs=[vec, st],
        out_shape=[jax.ShapeDtypeStruct((H, Dh, Bt), F32), jax.ShapeDtypeStruct((H, Dh, Dh, Bt), F32)],
        compiler_params=_cparams("parallel"),
        name="retention_sample",
    )(qT, kT, vT, sgT, gn_col, gamma, S)


def _attn_sample_kernel(q_ref, kn_ref, vn_ref, knT_ref, vnT_ref, kb_ref, vb_ref, o_ref, ko_ref, vo_ref):
    b = pl.program_id(0)
    H, Dh, Wn = kb_ref.shape[1], kb_ref.shape[2], kb_ref.shape[3]
    HD = H * Dh
    q = q_ref[0]
    kn = kn_ref[0]
    vn = vn_ref[0]
    row = lax.broadcasted_iota(jnp.int32, (H, HD), 0)
    col = lax.broadcasted_iota(jnp.int32, (H, HD), 1)
    own = (col // Dh) == row
    q_bd = jnp.where(own, q, 0.0)
    s_new = jnp.sum(q_bd * kn, axis=1, keepdims=True)

    bl = lax.broadcasted_iota(jnp.int32, knT_ref.shape, 1)
    k_col = jnp.sum(jnp.where(bl == b, knT_ref[...], 0.0), axis=1, keepdims=True)
    v_col = jnp.sum(jnp.where(bl == b, vnT_ref[...], 0.0), axis=1, keepdims=True)

    t = lax.broadcasted_iota(jnp.int32, (1, Wn), 1)
    last = t == (Wn - 1)
    prow = lax.broadcasted_iota(jnp.int32, (8, Wn), 0)
    q_bf = q_bd.astype(BF16)
    outs = []
    for h in range(H):
        K = kb_ref[0, h]
        V = vb_ref[0, h]
        hs = slice(h * Dh, (h + 1) * Dh)
        ko_ref[0, h] = jnp.where(last, k_col[hs], pltpu.roll(K, Wn - 1, 1))
        vo_ref[0, h] = jnp.where(last, v_col[hs], pltpu.roll(V, Wn - 1, 1))
        s = _dot(q_bf[:, hs], K.astype(BF16))[h:h + 1]
        sn = s_new[h:h + 1]
        Vb = V.astype(BF16)
        vn_h = vn[:, hs]
        lse_p, self_p = [], []
        probs = jnp.zeros((8, Wn), F32)
        for p, (window, d) in enumerate(DILATED_PATTERNS):
            valid = (((Wn - t) % d) == 0) & (t >= Wn - window)
            sm = jnp.where(valid, s, NEG_INF)
            m = jnp.maximum(jnp.max(sm, axis=1, keepdims=True), sn)
            e = jnp.exp(sm - m)
            en = jnp.exp(sn - m)
            l = jnp.sum(e, axis=1, keepdims=True) + en
            probs = jnp.where(prow == p, e / l, probs)
            self_p.append(en / l)
            lse_p.append(m + jnp.log(l))
        pv = _dot_nt(probs.astype(BF16), Vb)
        o_p = [pv[p:p + 1] + self_p[p] * vn_h for p in range(len(DILATED_PATTERNS))]
        mm = jnp.maximum(jnp.maximum(lse_p[0], lse_p[1]), lse_p[2])
        ws = [jnp.exp(x - mm) for x in lse_p]
        tot = ws[0] + ws[1] + ws[2]
        outs.append((ws[0] * o_p[0] + ws[1] * o_p[1] + ws[2] * o_p[2]) / tot)
    o_ref[0] = jnp.concatenate(outs, axis=1)


def _attention_sample(q, kn, vn, knT, vnT, k_buf, v_buf):
    Bt, H, Dh, Wn = k_buf.shape
    HD = H * Dh
    vec = pl.BlockSpec((1, 1, HD), lambda b: (b, 0, 0))
    tr = pl.BlockSpec((HD, Bt), lambda b: (0, 0))
    cache = pl.BlockSpec((1, H, Dh, Wn), lambda b: (b, 0, 0, 0))
    return pl.pallas_call(
        _attn_sample_kernel,
        grid=(Bt,),
        in_specs=[vec, vec, vec, tr, tr, cache, cache],
        out_specs=[vec, cache, cache],
        out_shape=[jax.ShapeDtypeStruct((Bt, 1, HD), F32),
                   jax.ShapeDtypeStruct(k_buf.shape, F32), jax.ShapeDtypeStruct(v_buf.shape, F32)],
        compiler_params=_cparams("parallel"),
        name="attention_sample",
    )(q, kn, vn, knT, vnT, k_buf, v_buf)


def _rope_tables(pos):
    half = HEAD_DIM // 2
    inv = ROPE_THETA ** (-jnp.arange(half, dtype=F32) / half)
    ang = pos.astype(F32)[:, None] * inv[None, :]
    cos, sin = jnp.cos(ang), jnp.sin(ang)
    cos_t = jnp.tile(cos, (1, 2 * N_RET_HEADS))
    sin_t = jnp.tile(jnp.concatenate([-sin, sin], axis=1), (1, N_RET_HEADS))
    return cos_t, sin_t


def _layer(x_prompt, x_sample, c_prompt, c_sample, state_ret, cache_k, cache_v,
           norm1, w_ada, b_ada, w_in, gn_ret, q_norm, k_norm, w_out, norm2,
           w_router, router_bias, w_exp_gate, w_exp_up, w_exp_down, w_sh_gate, w_sh_up, w_sh_down):
    B, L, D = x_prompt.shape
    Bs = x_sample.shape[0]
    H, Dh, W = N_RET_HEADS, HEAD_DIM, RET_WIDTH
    Np = B * L

    pad = (-(B + Bs)) % 8
    c_all = jnp.concatenate([c_prompt, c_sample, jnp.zeros((pad, D), F32)], axis=0)
    mod = _modulation(c_all, w_ada, b_ada)
    mod_p = [m.reshape(B, 1, D) for m in jnp.split(mod[:B], 6, axis=-1)]
    mod_s = [m.reshape(1, Bs, D) for m in jnp.split(mod[B:B + Bs], 6, axis=-1)]

    w_in_bf = w_in.astype(BF16)
    gi = jnp.arange(W) // Dh
    gmat = jnp.where(gi[:, None] == gi[None, :], 1.0 / Dh, 0.0).astype(BF16)
    qn_t = jnp.tile(q_norm, N_ATT_HEADS).reshape(1, W)
    kn_t = jnp.tile(k_norm, N_ATT_HEADS).reshape(1, W)
    n1 = norm1.reshape(1, D)
    n2 = norm2.reshape(1, D)
    gn = gn_ret.reshape(1, W)

    tm = TOKEN_TILE
    cos_p, sin_p = _rope_tables(jnp.arange(L, dtype=jnp.int32))
    cos_s, sin_s = _rope_tables(PAST_LEN + jnp.arange(1, dtype=jnp.int32))

    xp = x_prompt.reshape(Np, D)
    xs = x_sample.reshape(Bs, D)
    proj_p = _in_proj(xp, mod_p[1], mod_p[0], cos_p, sin_p, n1, qn_t, kn_t, gmat, w_in_bf, tm, L // tm, L // tm)
    proj_s = _in_proj(xs, mod_s[1], mod_s[0], cos_s, sin_s, n1, qn_t, kn_t, gmat, w_in_bf, Bs, 1, 1)
    qr, kr, vr, sg, qa, ka, va = [t.reshape(B, L, W) for t in proj_p]
    qr_s, kr_s, vr_s, sg_s, qa_s, ka_s, va_s = proj_s

    o_r, state_p = _retention_prompt(qr, kr, vr, sg, gn, gmat)
    o_a = _attention_prompt(qa, ka, va)
    keep = min(MAX_WINDOW, L)
    cache_kp = ka[:, L - keep:].reshape(B, keep, N_ATT_HEADS, Dh)
    cache_vp = va[:, L - keep:].reshape(B, keep, N_ATT_HEADS, Dh)

    log_g = jnp.log1p(-(2.0 ** (-5.0 - jnp.arange(H, dtype=F32))))
    gamma = jnp.exp(log_g).astype(F32)
    tr = lambda t: t.T.reshape(H, Dh, Bs)
    S_t = jnp.transpose(state_ret, (1, 2, 3, 0))
    orT, S_new_t = _retention_sample(tr(qr_s), tr(kr_s), tr(vr_s), tr(sg_s), gn_ret.reshape(H, Dh, 1), gamma, S_t)
    o_r_s = orT.reshape(W, Bs).T
    state_s = jnp.transpose(S_new_t, (3, 0, 1, 2))

    kb_t = jnp.transpose(cache_k, (0, 2, 3, 1))
    vb_t = jnp.transpose(cache_v, (0, 2, 3, 1))
    o_a_s, ko_t, vo_t = _attention_sample(qa_s.reshape(Bs, 1, W), ka_s.reshape(Bs, 1, W), va_s.reshape(Bs, 1, W),
                                          ka_s.T, va_s.T, kb_t, vb_t)
    cache_ks = jnp.transpose(ko_t, (0, 3, 1, 2))
    cache_vs = jnp.transpose(vo_t, (0, 3, 1, 2))

    wo = w_out.astype(BF16)
    wr_t = w_router.T
    wrh = wr_t.astype(BF16)
    wrl = (wr_t - wrh.astype(F32)).astype(BF16)
    rbias = router_bias.astype(F32).reshape(N_EXPERTS, 1)
    wsg, wsu, wsd = w_sh_gate.astype(BF16), w_sh_up.astype(BF16), w_sh_down.astype(BF16)
    zero_counts = jnp.zeros((N_EXPERTS, 1), F32)
    base_p, h2_p, eidx_p, w_p, rank_p, counts_p = _post_mixer(
        o_r.reshape(Np, W), o_a.reshape(Np, W), xp, mod_p[2], mod_p[4], mod_p[3], mod_p[5], n2, wo, wrh, wrl, rbias,
        zero_counts, wsg, wsu, wsd, tm, L // tm)
    base_s, h2_s, eidx_s, w_s, rank_s, counts = _post_mixer(
        o_r_s, o_a_s.reshape(Bs, W), xs, mod_s[2], mod_s[4], mod_s[3], mod_s[5], n2, wo, wrh, wrl, rbias,
        counts_p, wsg, wsu, wsd, Bs, 1)

    T = EXPERT_TILE
    n_tiles = ((Np + Bs) * TOP_K + N_EXPERTS * T + T - 1) // T
    pad_off, plan = _tile_plan(counts.reshape(N_EXPERTS).astype(jnp.int32), n_tiles)
    pad_off = pad_off.astype(F32).reshape(N_EXPERTS, 1)
    dest_p = _dest_rows(eidx_p, rank_p, pad_off)
    dest_s = _dest_rows(eidx_s, rank_s, pad_off)
    grouped = jnp.zeros((n_tiles * T * ROW_PIECES, LANES), jnp.uint32)
    grouped = _dispatch(dest_p, h2_p, grouped, tm)
    grouped = _dispatch(dest_s, h2_s, grouped, Bs)
    ys = _expert_matmul(grouped, plan, w_exp_gate, w_exp_up, w_exp_down)
    dest_all = jnp.concatenate([dest_p, dest_s], axis=1).T
    piece = jnp.arange(ROW_PIECES, dtype=jnp.int32)
    row_idx = (dest_all[:, :, None] * ROW_PIECES + piece[None, None, :]).reshape(1, -1)
    gathered = _gather_rows(ys, row_idx)
    tc = COMBINE_TILE
    y_p = _combine(base_p, w_p.T, mod_p[5], gathered, 0, tc, L // tc)
    y_s = _combine(base_s, w_s.T, mod_s[5], gathered, Np // Bs, Bs, 1)

    return (y_p.reshape(B, L, D), y_s.reshape(Bs, 1, D), state_p, cache_kp, cache_vp, state_s, cache_ks, cache_vs)


def kernel(x_prompt, x_sample, c_prompt, c_sample, state_ret, cache_win_k, cache_win_v, norm1, w_ada, b_ada, w_in,
           gn_ret, q_norm, k_norm, w_out, norm2, w_router, router_bias, w_exp_gate, w_exp_up, w_exp_down,
           w_sh_gate, w_sh_up, w_sh_down):
    assert w_in.shape[0] == 1, "single-layer step"
    assert x_sample.shape[1] == 1, "one new token per sample sequence"
    outs = _layer(x_prompt, x_sample, c_prompt, c_sample, state_ret[0], cache_win_k[0], cache_win_v[0],
                  norm1[0], w_ada[0], b_ada[0], w_in[0], gn_ret[0], q_norm[0], k_norm[0], w_out[0], norm2[0],
                  w_router[0], router_bias[0], w_exp_gate[0], w_exp_up[0], w_exp_down[0],
                  w_sh_gate[0], w_sh_up[0], w_sh_down[0])
    yp, ys, sp, kp, vp, ss, ksm, vsm = outs
    return (yp, ys, sp[None], kp[None], vp[None], ss[None], ksm[None], vsm[None])
```

```python
import functools

import jax
import jax.numpy as jnp
from jax import lax
from jax.experimental import pallas as pl
from jax.experimental.pallas import tpu as pltpu
from jax.experimental.pallas import tpu_sc as plsc

HEAD_DIM = 64
N_RET_HEADS = 8
N_ATT_HEADS = 8
RET_WIDTH = N_RET_HEADS * HEAD_DIM
ATT_WIDTH = N_ATT_HEADS * HEAD_DIM
RET_CHUNK = 128
DILATED_PATTERNS = ((128, 1), (512, 4), (2048, 16))
MAX_WINDOW = 2048
PAST_LEN = 8192
ROPE_THETA = 10000.0
N_EXPERTS = 256
TOP_K = 8
N_EXPERT_GROUPS = 8
TOPK_GROUPS = 4
ROUTED_SCALE = 2.5
NORM_EPS = 1e-6
GN_EPS = 1e-5
NEG_INF = -1e30

F32 = jnp.float32
BF16 = jnp.bfloat16

VMEM_LIMIT_BYTES = 56 * 1024 * 1024
EXPERT_TILE = 256
TOKEN_TILE = 512
COMBINE_TILE = 256
GATHER_WINDOW = 128
DMA_QUEUES = 2
ISSUE_UNROLL = 4
RET_STEP_CHUNKS = 4
EXPERT_ROW_SLOTS = 3
DEST_TILE = 2048
ATTN_UNROLL = 8


def _cparams(*sem):
    return pltpu.CompilerParams(dimension_semantics=sem, vmem_limit_bytes=VMEM_LIMIT_BYTES)


def _dot(a, b):
    return jnp.dot(a, b, preferred_element_type=F32)


def _dot_nt(a, b):
    return lax.dot_general(a, b, (((1,), (1,)), ((), ())), preferred_element_type=F32)


def _dot_tn(a, b):
    return lax.dot_general(a, b, (((0,), (0,)), ((), ())), preferred_element_type=F32)


def _mod_kernel(c_ref, w_ref, b_ref, o_ref):
    c = c_ref[...]
    a = (c * jax.nn.sigmoid(c)).astype(BF16)
    o_ref[...] = _dot(a, w_ref[...].astype(BF16)) + b_ref[...]


def _modulation(c, w_ada, b_ada):
    R, D = c.shape
    n_out = w_ada.shape[1]
    tn = 1024
    return pl.pallas_call(
        _mod_kernel,
        grid=(n_out // tn,),
        in_specs=[
            pl.BlockSpec((R, D), lambda j: (0, 0)),
            pl.BlockSpec((D, tn), lambda j: (0, j)),
            pl.BlockSpec((1, tn), lambda j: (0, j)),
        ],
        out_specs=pl.BlockSpec((R, tn), lambda j: (0, j)),
        out_shape=jax.ShapeDtypeStruct((R, n_out), F32),
        compiler_params=_cparams("parallel"),
        name="modulation",
    )(c, w_ada, b_ada.reshape(1, n_out))


def _swap_halves(x, first_half):
    n = x.shape[-1]
    return jnp.where(first_half, pltpu.roll(x, n - HEAD_DIM // 2, 1), pltpu.roll(x, HEAD_DIM // 2, 1))


def _head_mean(x, g):
    hi = x.astype(BF16)
    lo = (x - hi.astype(F32)).astype(BF16)
    return _dot(hi, g) + _dot(lo, g)


def _in_proj_kernel(x_ref, sc_ref, sh_ref, cos_ref, sin_ref, n1_ref, qn_ref, kn_ref, g_ref, w_ref,
                    qr_ref, kr_ref, vr_ref, sg_ref, qa_ref, ka_ref, va_ref):
    x = x_ref[...]
    ms = jnp.mean(x * x, axis=-1, keepdims=True)
    h = x * lax.rsqrt(ms + NORM_EPS) * n1_ref[...]
    h = (h * (1.0 + sc_ref[0]) + sh_ref[0]).astype(BF16)
    cos = cos_ref[...]
    sin = sin_ref[...]
    W = RET_WIDTH
    lane = lax.broadcasted_iota(jnp.int32, (1, W), 1)
    first_half = (lane % HEAD_DIM) < (HEAD_DIM // 2)

    def proj(c):
        return _dot(h, w_ref[:, c * W:(c + 1) * W])

    def rot(t):
        return t * cos + _swap_halves(t, first_half) * sin

    def qk_norm(t, gain):
        return t * lax.rsqrt(_head_mean(t * t, g_ref[...]) + NORM_EPS) * gain

    qr_ref[...] = rot(proj(0))
    kr_ref[...] = rot(proj(1)) * (HEAD_DIM ** -0.5)
    vr_ref[...] = proj(2)
    gr = proj(3)
    sg_ref[...] = gr * jax.nn.sigmoid(gr)
    qa_ref[...] = rot(qk_norm(proj(4), qn_ref[...])) * (HEAD_DIM ** -0.5)
    ka_ref[...] = rot(qk_norm(proj(5), kn_ref[...]))
    va_ref[...] = proj(6)


def _in_proj(x, sc, sh, cos, sin, norm1, q_norm_t, k_norm_t, gmat, w_in_bf, tm, rows_per_mod, pos_blocks):
    N, D = x.shape
    W = RET_WIDTH
    R = sc.shape[1]
    mod_spec = pl.BlockSpec((1, R, D), lambda i: (i // rows_per_mod, 0, 0))
    pos_spec = pl.BlockSpec((cos.shape[0] // pos_blocks, W), lambda i: (i % pos_blocks, 0))
    const = lambda shape: pl.BlockSpec(shape, lambda i: (0,) * len(shape))
    out_spec = pl.BlockSpec((tm, W), lambda i: (i, 0))
    return pl.pallas_call(
        _in_proj_kernel,
        grid=(N // tm,),
        in_specs=[
            pl.BlockSpec((tm, D), lambda i: (i, 0)),
            mod_spec, mod_spec, pos_spec, pos_spec,
            const((1, D)), const((1, W)), const((1, W)), const((W, W)), const(w_in_bf.shape),
        ],
        out_specs=[out_spec] * 7,
        out_shape=[jax.ShapeDtypeStruct((N, W), F32)] * 7,
        compiler_params=_cparams("parallel"),
        name="in_proj",
    )(x, sc, sh, cos, sin, norm1, q_norm_t, k_norm_t, gmat, w_in_bf)


def _ret_kernel(q_ref, k_ref, v_ref, sg_ref, dm_ref, qd_ref, kd_ref, sd_ref, bd_ref, gm_ref, gn_ref, o_ref, s_ref):
    @pl.when(pl.program_id(1) == 0)
    def _():
        s_ref[...] = jnp.zeros_like(s_ref)

    C = dm_ref.shape[2]
    Dh = HEAD_DIM
    P2 = 2 * Dh
    n_pairs = N_RET_HEADS // 2
    lane = lax.broadcasted_iota(jnp.int32, (1, P2), 1)
    head0 = lane < Dh
    zero = jnp.zeros((Dh, Dh), F32)
    state = []
    for p in range(n_pairs):
        top = jnp.concatenate([s_ref[0, 2 * p], zero], axis=1)
        bot = jnp.concatenate([zero, s_ref[0, 2 * p + 1]], axis=1)
        state.append(jnp.concatenate([top, bot], axis=0))
    bd = bd_ref[...]
    gm = gm_ref[...]
    for j in range(q_ref.shape[1] // C):
        rows = slice(j * C, (j + 1) * C)
        for p in range(n_pairs):
            sl = slice(P2 * p, P2 * (p + 1))
            q = q_ref[0, rows, sl]
            k = k_ref[0, rows, sl]
            vb = v_ref[0, rows, sl].astype(BF16)
            kdb = (k * kd_ref[:, sl]).astype(BF16)
            q2 = jnp.concatenate([jnp.where(head0, q, 0.0), jnp.where(head0, 0.0, q)], axis=0).astype(BF16)
            sc = _dot_nt(q2, k.astype(BF16)) * dm_ref[p]
            o2 = _dot(sc.astype(BF16), vb)
            S = state[p]
            o = jnp.where(head0, o2[:C], o2[C:]) + _dot(q.astype(BF16), S.astype(BF16)) * qd_ref[:, sl]
            state[p] = S * sd_ref[p] + _dot_tn(kdb, vb) * bd
            o_ref[0, rows, sl] = o
    for p in range(n_pairs):
        s_ref[0, 2 * p] = state[p][:Dh, :Dh]
        s_ref[0, 2 * p + 1] = state[p][Dh:, Dh:]
    o = o_ref[0]
    oc = o - _head_mean(o, gm)
    var = _head_mean(oc * oc, gm)
    o_ref[0] = oc * lax.rsqrt(var + GN_EPS) * gn_ref[...] * sg_ref[0]


def _decay_tables(C):
    H = N_RET_HEADS
    log_g = jnp.log1p(-(2.0 ** (-5.0 - jnp.arange(H, dtype=F32))))
    i = jnp.arange(C)
    diff = i[:, None] - i[None, :]
    dmask = jnp.where(diff[None] >= 0, jnp.exp(log_g[:, None, None] * jnp.maximum(diff, 0)[None]), 0.0).astype(F32)
    q_decay = jnp.exp(log_g[None, :] * (i[:, None] + 1)).astype(F32)
    k_decay = jnp.exp(log_g[None, :] * (C - 1 - i)[:, None]).astype(F32)
    s_decay = jnp.exp(log_g * C).astype(F32)
    return dmask, q_decay, k_decay, s_decay


def _retention_prompt(q, k, v, sg, gn, gm):
    B, L, W = q.shape
    C = RET_CHUNK
    H, Dh = N_RET_HEADS, HEAD_DIM
    dmask, q_decay, k_decay, s_decay = _decay_tables(C)
    qd = jnp.repeat(q_decay, Dh, axis=1)
    kd = jnp.repeat(k_decay, Dh, axis=1)
    P2 = 2 * Dh
    pair_head = jnp.arange(P2) // Dh
    bd = (pair_head[:, None] == pair_head[None, :]).astype(F32)
    sd = s_decay.reshape(H // 2, 2)[:, pair_head][:, :, None] * bd[None]
    rows = C * RET_STEP_CHUNKS
    blk = pl.BlockSpec((1, rows, W), lambda b, c: (b, c, 0))
    const = lambda shape: pl.BlockSpec(shape, lambda b, c: (0,) * len(shape))
    return pl.pallas_call(
        _ret_kernel,
        grid=(B, L // rows),
        in_specs=[blk, blk, blk, blk, const((H // 2, 2 * C, C)), const((C, W)), const((C, W)),
                  const((H // 2, P2, P2)), const((P2, P2)), const((W, W)), const((1, W))],
        out_specs=[blk, pl.BlockSpec((1, H, Dh, Dh), lambda b, c: (b, 0, 0, 0))],
        out_shape=[jax.ShapeDtypeStruct((B, L, W), F32), jax.ShapeDtypeStruct((B, H, Dh, Dh), F32)],
        compiler_params=_cparams("parallel", "arbitrary"),
        name="retention_prompt",
    )(q, k, v, sg, dmask.reshape(H // 2, 2 * C, C), qd, kd, sd, bd, gm, gn)


def _attn_kernel(q_ref, k_ref, v_ref, o_ref, oacc, lacc):
    L = q_ref.shape[1]
    P2 = q_ref.shape[2]
    lane = lax.broadcasted_iota(jnp.int32, (1, P2), 1)
    head0 = lane < HEAD_DIM

    for p, (window, d) in enumerate(DILATED_PATTERNS):
        band = window // d
        nb = L // (d * band)
        qi = lax.broadcasted_iota(jnp.int32, (2 * band, 2 * band), 0) % band
        ki = lax.broadcasted_iota(jnp.int32, (2 * band, 2 * band), 1)
        dist = qi + band - ki
        in_band = (dist >= 0) & (dist <= band)
        bias_any = jnp.where(in_band, 0.0, NEG_INF)
        bias_first = jnp.where(in_band & (ki >= band), 0.0, NEG_INF)

        def block(idx, d=d, band=band, nb=nb, p=p, bias_any=bias_any, bias_first=bias_first):
            r = idx // nb
            n = idx % nb
            qs = r + d * band * n
            ps = jnp.maximum(qs - d * band, r)
            cur = pl.ds(qs, band, stride=d) if d > 1 else pl.ds(qs, band)
            prev = pl.ds(ps, band, stride=d) if d > 1 else pl.ds(ps, band)
            qb = q_ref[0, cur, :]
            kk = jnp.concatenate([k_ref[0, prev, :], k_ref[0, cur, :]], axis=0).astype(BF16)
            vv = jnp.concatenate([v_ref[0, prev, :], v_ref[0, cur, :]], axis=0).astype(BF16)
            q2 = jnp.concatenate([jnp.where(head0, qb, 0.0), jnp.where(head0, 0.0, qb)], axis=0).astype(BF16)
            s = _dot_nt(q2, kk) + jnp.where(n > 0, bias_any, bias_first)
            m = jnp.max(jnp.maximum(s[:, :band], s[:, band:]), axis=-1, keepdims=True)
            e = jnp.exp(s - m)
            l = jnp.sum(e[:, :band] + e[:, band:], axis=-1, keepdims=True)
            o = _dot(e.astype(BF16), vv) * (1.0 / l)
            lse = m + jnp.log(l)
            oacc[p, cur, :] = jnp.where(head0, o[:band], o[band:])
            lacc[p, cur, :] = jnp.where(head0, lse[:band], lse[band:])

        def body(i, carry, block=block):
            for u in range(ATTN_UNROLL):
                block(i * ATTN_UNROLL + u)
            return carry

        lax.fori_loop(0, d * nb // ATTN_UNROLL, body, 0)

    rows = 512

    def combine(i, carry):
        sl = pl.ds(pl.multiple_of(i * rows, rows), rows)
        l0, l1, l2 = lacc[0, sl, :], lacc[1, sl, :], lacc[2, sl, :]
        m = jnp.maximum(jnp.maximum(l0, l1), l2)
        w0, w1, w2 = jnp.exp(l0 - m), jnp.exp(l1 - m), jnp.exp(l2 - m)
        tot = w0 + w1 + w2
        o_ref[0, sl, :] = (w0 * oacc[0, sl, :] + w1 * oacc[1, sl, :] + w2 * oacc[2, sl, :]) / tot
        return carry

    lax.fori_loop(0, L // rows, combine, 0)


def _attention_prompt(q, k, v):
    B, L, W = q.shape
    P2 = 2 * HEAD_DIM
    blk = pl.BlockSpec((1, L, P2), lambda b, hp: (b, 0, hp))
    return pl.pallas_call(
        _attn_kernel,
        grid=(B, W // P2),
        in_specs=[blk, blk, blk],
        out_specs=blk,
        out_shape=jax.ShapeDtypeStruct((B, L, W), F32),
        scratch_shapes=[pltpu.VMEM((len(DILATED_PATTERNS), L, P2), F32)] * 2,
        compiler_params=_cparams("parallel", "parallel"),
        name="attention_prompt",
    )(q, k, v)


def _pack_bf16_pair(a, b):
    ua = pltpu.bitcast(a.astype(BF16).astype(F32), jnp.uint32)
    ub = pltpu.bitcast(b.astype(BF16).astype(F32), jnp.uint32)
    return ua | (ub >> 16)


def _unpack_bf16_pair(p):
    a = pltpu.bitcast(p & jnp.uint32(0xFFFF0000), F32)
    b = pltpu.bitcast(p << 16, F32)
    return a, b


ROUTE_LANES = 128
LANES = 128


ROW_PIECES = 4


def _store_rows(ref, packed):
    n = packed.shape[0]
    for c in range(ROW_PIECES):
        ref[pl.ds(c, n, stride=ROW_PIECES), :] = packed[:, c * LANES:(c + 1) * LANES]


def _load_rows(ref):
    n = ref.shape[0] // ROW_PIECES
    return jnp.concatenate([ref[pl.ds(c, n, stride=ROW_PIECES), :] for c in range(ROW_PIECES)], axis=1)


def _token_rows(t):
    return pl.ds(pl.multiple_of(t * ROW_PIECES, ROW_PIECES), ROW_PIECES)


def _route_chunk(s, bias, carry):
    E, n = s.shape
    G, GS = N_EXPERT_GROUPS, E // N_EXPERT_GROUPS
    NEG = -jnp.inf
    choice = s + bias
    row = lax.broadcasted_iota(jnp.int32, (E, n), 0).astype(F32)
    lrow = row[:GS]
    gs_rows = []
    for g in range(G):
        c = choice[g * GS:(g + 1) * GS]
        m1 = jnp.max(c, axis=0, keepdims=True)
        i1 = jnp.min(jnp.where(c == m1, lrow, float(GS)), axis=0, keepdims=True)
        m2 = jnp.max(jnp.where(lrow == i1, NEG, c), axis=0, keepdims=True)
        gs_rows.append(m1 + m2)
    gs = jnp.concatenate(gs_rows, axis=0)
    grow = lax.broadcasted_iota(jnp.int32, (G, n), 0).astype(F32)
    gsel = jnp.zeros((G, n), F32)
    for _ in range(TOPK_GROUPS):
        gm = jnp.max(gs, axis=0, keepdims=True)
        gi = jnp.min(jnp.where(gs == gm, grow, float(G)), axis=0, keepdims=True)
        hit = grow == gi
        gsel = jnp.where(hit, 1.0, gsel)
        gs = jnp.where(hit, NEG, gs)
    emask = jnp.concatenate([jnp.broadcast_to(gsel[g:g + 1], (GS, n)) for g in range(G)], axis=0)
    masked = jnp.where(emask > 0.5, choice, NEG)
    ids, sks = [], []
    member = jnp.zeros((E, n), F32)
    for _ in range(TOP_K):
        mk = jnp.max(masked, axis=0, keepdims=True)
        ik = jnp.min(jnp.where(masked == mk, row, float(E)), axis=0, keepdims=True)
        sel = row == ik
        sks.append(jnp.sum(jnp.where(sel, s, 0.0), axis=0, keepdims=True))
        masked = jnp.where(sel, NEG, masked)
        member = jnp.where(sel, 1.0, member)
        ids.append(ik)
    eidx = jnp.concatenate(ids, axis=0)
    sk = jnp.concatenate(sks, axis=0)
    w = sk / jnp.sum(sk, axis=0, keepdims=True) * ROUTED_SCALE
    ti = lax.broadcasted_iota(jnp.int32, (n, n), 0)
    tj = lax.broadcasted_iota(jnp.int32, (n, n), 1)
    upper = jnp.where(ti <= tj, 1.0, 0.0).astype(BF16)
    incl = _dot(member.astype(BF16), upper)
    rank_dense = carry + incl - member
    ranks = [jnp.sum(jnp.where(row == ids[k], rank_dense, 0.0), axis=0, keepdims=True) for k in range(TOP_K)]
    return eidx, w, jnp.concatenate(ranks, axis=0), carry + incl[:, n - 1:n]


def _post_kernel(or_ref, oa_ref, x_ref, g1_ref, sc_ref, sh_ref, g2_ref, n2_ref, wo_ref, wrh_ref, wrl_ref, rb_ref,
                 cin_ref, wsg_ref, wsu_ref, wsd_ref, base_ref, h2_ref, eidx_ref, w_ref, rank_ref, cnt_ref):
    @pl.when(pl.program_id(0) == 0)
    def _():
        cnt_ref[...] = cin_ref[...]

    W = RET_WIDTH
    mix = _dot(or_ref[...].astype(BF16), wo_ref[:W, :]) + _dot(oa_ref[...].astype(BF16), wo_ref[W:, :])
    x2 = x_ref[...] + g1_ref[0] * mix
    ms = jnp.mean(x2 * x2, axis=-1, keepdims=True)
    h2 = x2 * lax.rsqrt(ms + NORM_EPS) * n2_ref[...]
    h2 = h2 * (1.0 + sc_ref[0]) + sh_ref[0]
    hb = h2.astype(BF16)
    hl = (h2 - hb.astype(F32)).astype(BF16)
    half = h2.shape[1] // 2
    _store_rows(h2_ref, _pack_bf16_pair(h2[:, :half], h2[:, half:]))
    g = _dot(hb, wsg_ref[...])
    u = _dot(hb, wsu_ref[...])
    shared = _dot((g * jax.nn.sigmoid(g) * u).astype(BF16), wsd_ref[...])
    base_ref[...] = x2 + g2_ref[0] * shared

    logits = _dot_nt(wrh_ref[...], hb) + (_dot_nt(wrh_ref[...], hl) + _dot_nt(wrl_ref[...], hb))
    s = jax.nn.sigmoid(logits)
    carry = cnt_ref[...]
    n = ROUTE_LANES
    for j in range(s.shape[1] // n):
        sl = slice(j * n, (j + 1) * n)
        eidx, w, rank, carry = _route_chunk(s[:, sl], rb_ref[...], carry)
        eidx_ref[:, sl] = eidx.astype(jnp.int32)
        w_ref[:, sl] = w
        rank_ref[:, sl] = rank.astype(jnp.int32)
    cnt_ref[...] = carry


def _post_mixer(o_r, o_a, x, g1, sc2, sh2, g2, norm2, wo, wrh_t, wrl_t, rbias, counts_in, wsg, wsu, wsd, tm,
                rows_per_mod):
    N, D = x.shape
    assert D // 2 == ROW_PIECES * LANES
    W = RET_WIDTH
    E = wrh_t.shape[0]
    R = g1.shape[1]
    mod_spec = pl.BlockSpec((1, R, D), lambda i: (i // rows_per_mod, 0, 0))
    const = lambda a: pl.BlockSpec(a.shape, lambda i: (0,) * a.ndim)
    tok = lambda rows: pl.BlockSpec((rows, tm), lambda i: (0, i))
    return pl.pallas_call(
        _post_kernel,
        grid=(N // tm,),
        in_specs=[
            pl.BlockSpec((tm, W), lambda i: (i, 0)),
            pl.BlockSpec((tm, W), lambda i: (i, 0)),
            pl.BlockSpec((tm, D), lambda i: (i, 0)),
            mod_spec, mod_spec, mod_spec, mod_spec,
            const(norm2), const(wo), const(wrh_t), const(wrl_t), const(rbias), const(counts_in),
            const(wsg), const(wsu), const(wsd),
        ],
        out_specs=[
            pl.BlockSpec((tm, D), lambda i: (i, 0)),
            pl.BlockSpec((tm * ROW_PIECES, LANES), lambda i: (i, 0)),
            tok(TOP_K), tok(TOP_K), tok(TOP_K),
            pl.BlockSpec((E, 1), lambda i: (0, 0)),
        ],
        out_shape=[
            jax.ShapeDtypeStruct((N, D), F32),
            jax.ShapeDtypeStruct((N * ROW_PIECES, LANES), jnp.uint32),
            jax.ShapeDtypeStruct((TOP_K, N), jnp.int32),
            jax.ShapeDtypeStruct((TOP_K, N), F32),
            jax.ShapeDtypeStruct((TOP_K, N), jnp.int32),
            jax.ShapeDtypeStruct((E, 1), F32),
        ],
        compiler_params=_cparams("arbitrary"),
        name="post_mixer",
    )(o_r, o_a, x, g1, sc2, sh2, g2, norm2, wo, wrh_t, wrl_t, rbias, counts_in, wsg, wsu, wsd)


def _expert_kernel(be_ref, first_ref, nused_ref, nxt_ref, slot_ref, x_hbm, wg_hbm, wu_hbm, wd_hbm, y_ref,
                   xbuf, wg_f, wu_f, wd_f, wg_s, wu_s, wd_s, xsem, sem):
    i = pl.program_id(0)
    is_first = first_ref[i] == 1
    n_used = nused_ref[0]
    used = i < n_used
    tile_rows = xbuf.shape[1]
    x_slots = xbuf.shape[0]

    def weight_copies(e, s):
        return (pltpu.make_async_copy(wg_hbm.at[e], wg_f.at[s], sem.at[s, 0]),
                pltpu.make_async_copy(wu_hbm.at[e], wu_f.at[s], sem.at[s, 1]),
                pltpu.make_async_copy(wd_hbm.at[e], wd_f.at[s], sem.at[s, 2]))

    def x_copy(tile):
        s = tile % x_slots
        rows = pl.ds(pl.multiple_of(tile * tile_rows, tile_rows), tile_rows)
        return pltpu.make_async_copy(x_hbm.at[rows], xbuf.at[s], xsem.at[s])

    @pl.when(i == 0)
    def _():
        for t in range(x_slots - 1):
            @pl.when(t < n_used)
            def _():
                x_copy(t).start()

    @pl.when(used)
    def _():
        x_copy(i).wait()

    @pl.when(i + (x_slots - 1) < n_used)
    def _():
        x_copy(i + (x_slots - 1)).start()

    def compute():
        xa, xb = _unpack_bf16_pair(_load_rows(xbuf.at[i % x_slots]))
        half = xa.shape[1]
        x = jnp.concatenate([xa.astype(BF16), xb.astype(BF16)], axis=1)
        g = _dot(x, wg_s[...])
        u = _dot(x, wu_s[...])
        hmid = (g * jax.nn.sigmoid(g) * u).astype(BF16)
        _store_rows(y_ref, _pack_bf16_pair(_dot(hmid, wd_s[:, :half]), _dot(hmid, wd_s[:, half:])))

    @pl.when(i == 0)
    def _():
        for c in weight_copies(be_ref[0], 0):
            c.start()

    @pl.when(is_first)
    def _():
        s = slot_ref[i]
        for c in weight_copies(be_ref[i], s):
            c.wait()

        @pl.when(nxt_ref[i] >= 0)
        def _():
            for c in weight_copies(nxt_ref[i], 1 - s):
                c.start()

        wg_s[...] = wg_f[s].astype(BF16)
        wu_s[...] = wu_f[s].astype(BF16)
        wd_s[...] = wd_f[s].astype(BF16)
        compute()

    @pl.when(jnp.logical_and(jnp.logical_not(is_first), used))
    def _():
        compute()

    @pl.when(jnp.logical_not(used))
    def _():
        y_ref[...] = jnp.zeros_like(y_ref)


def _expert_matmul(xs, plan, w_gate, w_up, w_down):
    P = xs.shape[0] // ROW_PIECES
    T = EXPERT_TILE
    D, F = w_gate.shape[1], w_gate.shape[2]
    row = pl.BlockSpec((T * ROW_PIECES, LANES), lambda i, *_: (i, 0))
    hbm = pl.BlockSpec(memory_space=pl.ANY)
    return pl.pallas_call(
        _expert_kernel,
        grid_spec=pltpu.PrefetchScalarGridSpec(
            num_scalar_prefetch=5,
            grid=(P // T,),
            in_specs=[hbm, hbm, hbm, hbm],
            out_specs=row,
            scratch_shapes=[
                pltpu.VMEM((EXPERT_ROW_SLOTS, T * ROW_PIECES, LANES), jnp.uint32),
                pltpu.VMEM((2, D, F), F32), pltpu.VMEM((2, D, F), F32), pltpu.VMEM((2, F, D), F32),
                pltpu.VMEM((D, F), BF16), pltpu.VMEM((D, F), BF16), pltpu.VMEM((F, D), BF16),
                pltpu.SemaphoreType.DMA((EXPERT_ROW_SLOTS,)),
                pltpu.SemaphoreType.DMA((2, 3)),
            ],
        ),
        out_shape=jax.ShapeDtypeStruct(xs.shape, jnp.uint32),
        compiler_params=_cparams("arbitrary"),
        name="routed_experts",
    )(*plan, xs, w_gate, w_up, w_down)


def _dest_kernel(eidx_ref, rank_ref, off_ref, dest_ref):
    E = off_ref.shape[0]
    n = eidx_ref.shape[1]
    row = lax.broadcasted_iota(jnp.int32, (E, n), 0)
    off = off_ref[...]
    rows = []
    for k in range(eidx_ref.shape[0]):
        hit = row == eidx_ref[k:k + 1, :]
        rows.append(jnp.sum(jnp.where(hit, off, 0.0), axis=0, keepdims=True))
    dest_ref[...] = rank_ref[...] + jnp.concatenate(rows, axis=0).astype(jnp.int32)


def _dest_rows(eidx, rank, pad_off):
    K, N = eidx.shape
    tn = min(N, DEST_TILE)
    blk = pl.BlockSpec((K, tn), lambda i: (0, i))
    return pl.pallas_call(
        _dest_kernel,
        grid=(N // tn,),
        in_specs=[blk, blk, pl.BlockSpec(pad_off.shape, lambda i: (0, 0))],
        out_specs=blk,
        out_shape=jax.ShapeDtypeStruct((K, N), jnp.int32),
        compiler_params=_cparams("parallel"),
        name="dest_rows",
    )(eidx, rank, pad_off)


def _row_copy(src, dst, sem):
    return pltpu.make_async_copy(src, dst, sem)


def _dispatch_kernel(dest_ref, h_ref, xs_in_ref, xs_ref, sem):
    del xs_in_ref
    K, tm = dest_ref.shape

    def issue(t, carry):
        for k in range(K):
            _row_copy(h_ref.at[_token_rows(t)], xs_ref.at[_token_rows(dest_ref[k, t])], sem).start(
                priority=k % DMA_QUEUES)
        return carry

    lax.fori_loop(0, tm, issue, 0, unroll=ISSUE_UNROLL)
    for _ in range(K):
        _row_copy(h_ref, xs_ref.at[pl.ds(0, tm * ROW_PIECES)], sem).wait()


def _dispatch(dest, h2p, xs, tm):
    K, N = dest.shape
    return pl.pallas_call(
        _dispatch_kernel,
        grid=(N // tm,),
        in_specs=[
            pl.BlockSpec((K, tm), lambda i: (0, i), memory_space=pltpu.SMEM),
            pl.BlockSpec((tm * ROW_PIECES, LANES), lambda i: (i, 0)),
            pl.BlockSpec(memory_space=pl.ANY),
        ],
        out_specs=pl.BlockSpec(memory_space=pl.ANY),
        out_shape=jax.ShapeDtypeStruct(xs.shape, xs.dtype),
        scratch_shapes=[pltpu.SemaphoreType.DMA],
        input_output_aliases={2: 0},
        compiler_params=_cparams("arbitrary"),
        name="dispatch",
    )(dest, h2p, xs)


def _gather_rows(src, idx):
    G, n = idx.shape
    nb = n // GATHER_WINDOW
    mesh = plsc.VectorSubcoreMesh(core_axis_name="core", subcore_axis_name="subcore")

    @pl.kernel(out_type=jax.ShapeDtypeStruct((G * n, src.shape[1]), src.dtype), mesh=mesh, scratch_types=[])
    def gather_kernel(src_hbm, idx_hbm, out_hbm):
        def body(idx_vmem, out_vmem):
            pltpu.sync_copy(src_hbm.at[idx_vmem.at[0]], out_vmem)

        pltpu.emit_pipeline(
            body,
            grid=(G, nb),
            in_specs=[pl.BlockSpec((1, GATHER_WINDOW), index_map=lambda g, j: (g, j))],
            out_specs=[pl.BlockSpec((GATHER_WINDOW, src.shape[1]), index_map=lambda g, j: (g * nb + j, 0))],
            core_axis_name=("core", "subcore"),
            dimension_semantics=(pltpu.PARALLEL, pltpu.PARALLEL),
        )(idx_hbm, out_hbm)

    return gather_kernel(src, idx)


def _combine_kernel(base_ref, w_ref, g2_ref, rows_ref, o_ref):
    tm = base_ref.shape[0]
    half = ROW_PIECES * LANES
    w = w_ref[...]
    acc_a = jnp.zeros((tm, half), F32)
    acc_b = jnp.zeros((tm, half), F32)
    for k in range(TOP_K):
        packed = jnp.concatenate([rows_ref[c * TOP_K + k] for c in range(ROW_PIECES)], axis=1)
        a, b = _unpack_bf16_pair(packed)
        wk = w[:, k:k + 1]
        acc_a = acc_a + wk * a
        acc_b = acc_b + wk * b
    g2 = g2_ref[0]
    o_ref[:, :half] = base_ref[:, :half] + g2[:, :half] * acc_a
    o_ref[:, half:] = base_ref[:, half:] + g2[:, half:] * acc_b


def _combine(base, w_tok, g2, gathered, first_block, tm, rows_per_mod):
    N, D = base.shape
    K = w_tok.shape[1]
    R = g2.shape[1]
    planes = gathered.shape[0]
    return pl.pallas_call(
        _combine_kernel,
        grid=(N // tm,),
        in_specs=[
            pl.BlockSpec((tm, D), lambda i: (i, 0)),
            pl.BlockSpec((tm, K), lambda i: (i, 0)),
            pl.BlockSpec((1, R, D), lambda i: (i // rows_per_mod, 0, 0)),
            pl.BlockSpec((planes, tm, LANES), lambda i: (0, first_block + i, 0)),
        ],
        out_specs=pl.BlockSpec((tm, D), lambda i: (i, 0)),
        out_shape=jax.ShapeDtypeStruct((N, D), F32),
        compiler_params=_cparams("parallel"),
        name="combine",
    )(base, w_tok, g2, gathered)


def _tile_plan(counts, n_tiles):
    T = EXPERT_TILE
    E = counts.shape[0]
    padded = (counts + T - 1) // T * T
    pad_end = jnp.cumsum(padded)
    pad_off = pad_end - padded
    idx = jnp.arange(n_tiles, dtype=jnp.int32)
    tile_start = idx * T
    blk_exp = jnp.minimum(jnp.sum((pad_end[None, :] <= tile_start[:, None]).astype(jnp.int32), axis=1), E - 1)
    used = tile_start < pad_end[-1]
    changed = jnp.concatenate([jnp.ones((1,), bool), blk_exp[1:] != blk_exp[:-1]])
    first = jnp.logical_and(used, changed)
    next_first = lax.cummin(jnp.where(first, idx, n_tiles), reverse=True)
    next_first = jnp.concatenate([next_first[1:], jnp.full((1,), n_tiles, jnp.int32)])
    nxt = jnp.where(next_first < n_tiles, blk_exp[jnp.minimum(next_first, n_tiles - 1)], -1)
    slot = (jnp.cumsum(first.astype(jnp.int32)) - 1) % 2
    n_used = (pad_end[-1] // T).reshape(1)
    as_i32 = lambda a: a.astype(jnp.int32)
    return pad_off, (as_i32(blk_exp), as_i32(first), as_i32(n_used), as_i32(nxt), as_i32(slot))


def _ret_sample_kernel(q_ref, k_ref, v_ref, sg_ref, gn_ref, dec_ref, s_ref, o_ref, so_ref):
    q = q_ref[0]
    k = k_ref[0]
    v = v_ref[0]
    gamma = dec_ref[pl.program_id(0)]
    qk = jnp.sum(q * k, axis=0, keepdims=True)
    o = qk * v
    Dh = q.shape[0]
    acc = jnp.zeros_like(v)
    for d in range(Dh):
        S = s_ref[0, d]
        acc = acc + q[d:d + 1, :] * S
        so_ref[0, d] = S * gamma + k[d:d + 1, :] * v
    o = o + acc * gamma
    mu = jnp.mean(o, axis=0, keepdims=True)
    oc = o - mu
    var = jnp.mean(oc * oc, axis=0, keepdims=True)
    o_ref[0] = oc * lax.rsqrt(var + GN_EPS) * gn_ref[0] * sg_ref[0]


def _retention_sample(qT, kT, vT, sgT, gn_col, gamma, S):
    H, Dh, Bt = qT.shape
    vec = pl.BlockSpec((1, Dh, Bt), lambda h: (h, 0, 0))
    st = pl.BlockSpec((1, Dh, Dh, Bt), lambda h: (h, 0, 0, 0))
    return pl.pallas_call(
        _ret_sample_kernel,
        grid=(H,),
        in_specs=[vec, vec, vec, vec, pl.BlockSpec((1, Dh, 1), lambda h: (h, 0, 0)),
                  pl.BlockSpec(memory_space=pltpu.SMEM), st],
        out_specs=[vec, st],
        out_shape=[jax.ShapeDtypeStruct((H, Dh, Bt), F32), jax.ShapeDtypeStruct((H, Dh, Dh, Bt), F32)],
        compiler_params=_cparams("parallel"),
        name="retention_sample",
    )(qT, kT, vT, sgT, gn_col, gamma, S)


def _attn_sample_kernel(q_ref, kn_ref, vn_ref, knT_ref, vnT_ref, kb_ref, vb_ref, o_ref, ko_ref, vo_ref):
    b = pl.program_id(0)
    H, Dh, Wn = kb_ref.shape[1], kb_ref.shape[2], kb_ref.shape[3]
    HD = H * Dh
    q = q_ref[0]
    kn = kn_ref[0]
    vn = vn_ref[0]
    row = lax.broadcasted_iota(jnp.int32, (H, HD), 0)
    col = lax.broadcasted_iota(jnp.int32, (H, HD), 1)
    own = (col // Dh) == row
    q_bd = jnp.where(own, q, 0.0)
    s_new = jnp.sum(q_bd * kn, axis=1, keepdims=True)

    bl = lax.broadcasted_iota(jnp.int32, knT_ref.shape, 1)
    k_col = jnp.sum(jnp.where(bl == b, knT_ref[...], 0.0), axis=1, keepdims=True)
    v_col = jnp.sum(jnp.where(bl == b, vnT_ref[...], 0.0), axis=1, keepdims=True)

    t = lax.broadcasted_iota(jnp.int32, (1, Wn), 1)
    last = t == (Wn - 1)
    prow = lax.broadcasted_iota(jnp.int32, (8, Wn), 0)
    q_bf = q_bd.astype(BF16)
    outs = []
    for h in range(H):
        K = kb_ref[0, h]
        V = vb_ref[0, h]
        hs = slice(h * Dh, (h + 1) * Dh)
        ko_ref[0, h] = jnp.where(last, k_col[hs], pltpu.roll(K, Wn - 1, 1))
        vo_ref[0, h] = jnp.where(last, v_col[hs], pltpu.roll(V, Wn - 1, 1))
        s = _dot(q_bf[:, hs], K.astype(BF16))[h:h + 1]
        sn = s_new[h:h + 1]
        Vb = V.astype(BF16)
        vn_h = vn[:, hs]
        lse_p, self_p = [], []
        probs = jnp.zeros((8, Wn), F32)
        for p, (window, d) in enumerate(DILATED_PATTERNS):
            valid = (((Wn - t) % d) == 0) & (t >= Wn - window)
            sm = jnp.where(valid, s, NEG_INF)
            m = jnp.maximum(jnp.max(sm, axis=1, keepdims=True), sn)
            e = jnp.exp(sm - m)
            en = jnp.exp(sn - m)
            l = jnp.sum(e, axis=1, keepdims=True) + en
            probs = jnp.where(prow == p, e / l, probs)
            self_p.append(en / l)
            lse_p.append(m + jnp.log(l))
        pv = _dot_nt(probs.astype(BF16), Vb)
        o_p = [pv[p:p + 1] + self_p[p] * vn_h for p in range(len(DILATED_PATTERNS))]
        mm = jnp.maximum(jnp.maximum(lse_p[0], lse_p[1]), lse_p[2])
        ws = [jnp.exp(x - mm) for x in lse_p]
        tot = ws[0] + ws[1] + ws[2]
        outs.append((ws[0] * o_p[0] + ws[1] * o_p[1] + ws[2] * o_p[2]) / tot)
    o_ref[0] = jnp.concatenate(outs, axis=1)


def _attention_sample(q, kn, vn, knT, vnT, k_buf, v_buf):
    Bt, H, Dh, Wn = k_buf.shape
    HD = H * Dh
    vec = pl.BlockSpec((1, 1, HD), lambda b: (b, 0, 0))
    tr = pl.BlockSpec((HD, Bt), lambda b: (0, 0))
    cache = pl.BlockSpec((1, H, Dh, Wn), lambda b: (b, 0, 0, 0))
    return pl.pallas_call(
        _attn_sample_kernel,
        grid=(Bt,),
        in_specs=[vec, vec, vec, tr, tr, cache, cache],
        out_specs=[vec, cache, cache],
        out_shape=[jax.ShapeDtypeStruct((Bt, 1, HD), F32),
                   jax.ShapeDtypeStruct(k_buf.shape, F32), jax.ShapeDtypeStruct(v_buf.shape, F32)],
        compiler_params=_cparams("parallel"),
        name="attention_sample",
    )(q, kn, vn, knT, vnT, k_buf, v_buf)


def _rope_tables(pos):
    half = HEAD_DIM // 2
    inv = ROPE_THETA ** (-jnp.arange(half, dtype=F32) / half)
    ang = pos.astype(F32)[:, None] * inv[None, :]
    cos, sin = jnp.cos(ang), jnp.sin(ang)
    cos_t = jnp.tile(cos, (1, 2 * N_RET_HEADS))
    sin_t = jnp.tile(jnp.concatenate([-sin, sin], axis=1), (1, N_RET_HEADS))
    return cos_t, sin_t


def _layer(x_prompt, x_sample, c_prompt, c_sample, state_ret, cache_k, cache_v,
           norm1, w_ada, b_ada, w_in, gn_ret, q_norm, k_norm, w_out, norm2,
           w_router, router_bias, w_exp_gate, w_exp_up, w_exp_down, w_sh_gate, w_sh_up, w_sh_down):
    B, L, D = x_prompt.shape
    Bs = x_sample.shape[0]
    H, Dh, W = N_RET_HEADS, HEAD_DIM, RET_WIDTH
    Np = B * L

    pad = (-(B + Bs)) % 8
    c_all = jnp.concatenate([c_prompt, c_sample, jnp.zeros((pad, D), F32)], axis=0)
    mod = _modulation(c_all, w_ada, b_ada)
    mod_p = [m.reshape(B, 1, D) for m in jnp.split(mod[:B], 6, axis=-1)]
    mod_s = [m.reshape(1, Bs, D) for m in jnp.split(mod[B:B + Bs], 6, axis=-1)]

    w_in_bf = w_in.astype(BF16)
    gi = jnp.arange(W) // Dh
    gmat = jnp.where(gi[:, None] == gi[None, :], 1.0 / Dh, 0.0).astype(BF16)
    qn_t = jnp.tile(q_norm, N_ATT_HEADS).reshape(1, W)
    kn_t = jnp.tile(k_norm, N_ATT_HEADS).reshape(1, W)
    n1 = norm1.reshape(1, D)
    n2 = norm2.reshape(1, D)
    gn = gn_ret.reshape(1, W)

    tm = TOKEN_TILE
    cos_p, sin_p = _rope_tables(jnp.arange(L, dtype=jnp.int32))
    cos_s, sin_s = _rope_tables(PAST_LEN + jnp.arange(1, dtype=jnp.int32))

    xp = x_prompt.reshape(Np, D)
    xs = x_sample.reshape(Bs, D)
    proj_p = _in_proj(xp, mod_p[1], mod_p[0], cos_p, sin_p, n1, qn_t, kn_t, gmat, w_in_bf, tm, L // tm, L // tm)
    proj_s = _in_proj(xs, mod_s[1], mod_s[0], cos_s, sin_s, n1, qn_t, kn_t, gmat, w_in_bf, Bs, 1, 1)
    qr, kr, vr, sg, qa, ka, va = [t.reshape(B, L, W) for t in proj_p]
    qr_s, kr_s, vr_s, sg_s, qa_s, ka_s, va_s = proj_s

    o_r, state_p = _retention_prompt(qr, kr, vr, sg, gn, gmat)
    o_a = _attention_prompt(qa, ka, va)
    keep = min(MAX_WINDOW, L)
    cache_kp = ka[:, L - keep:].reshape(B, keep, N_ATT_HEADS, Dh)
    cache_vp = va[:, L - keep:].reshape(B, keep, N_ATT_HEADS, Dh)

    log_g = jnp.log1p(-(2.0 ** (-5.0 - jnp.arange(H, dtype=F32))))
    gamma = jnp.exp(log_g).astype(F32)
    tr = lambda t: t.T.reshape(H, Dh, Bs)
    S_t = jnp.transpose(state_ret, (1, 2, 3, 0))
    orT, S_new_t = _retention_sample(tr(qr_s), tr(kr_s), tr(vr_s), tr(sg_s), gn_ret.reshape(H, Dh, 1), gamma, S_t)
    o_r_s = orT.reshape(W, Bs).T
    state_s = jnp.transpose(S_new_t, (3, 0, 1, 2))

    kb_t = jnp.transpose(cache_k, (0, 2, 3, 1))
    vb_t = jnp.transpose(cache_v, (0, 2, 3, 1))
    o_a_s, ko_t, vo_t = _attention_sample(qa_s.reshape(Bs, 1, W), ka_s.reshape(Bs, 1, W), va_s.reshape(Bs, 1, W),
                                          ka_s.T, va_s.T, kb_t, vb_t)
    cache_ks = jnp.transpose(ko_t, (0, 3, 1, 2))
    cache_vs = jnp.transpose(vo_t, (0, 3, 1, 2))

    wo = w_out.astype(BF16)
    wr_t = w_router.T
    wrh = wr_t.astype(BF16)
    wrl = (wr_t - wrh.astype(F32)).astype(BF16)
    rbias = router_bias.astype(F32).reshape(N_EXPERTS, 1)
    wsg, wsu, wsd = w_sh_gate.astype(BF16), w_sh_up.astype(BF16), w_sh_down.astype(BF16)
    zero_counts = jnp.zeros((N_EXPERTS, 1), F32)
    base_p, h2_p, eidx_p, w_p, rank_p, counts_p = _post_mixer(
        o_r.reshape(Np, W), o_a.reshape(Np, W), xp, mod_p[2], mod_p[4], mod_p[3], mod_p[5], n2, wo, wrh, wrl, rbias,
        zero_counts, wsg, wsu, wsd, tm, L // tm)
    base_s, h2_s, eidx_s, w_s, rank_s, counts = _post_mixer(
        o_r_s, o_a_s.reshape(Bs, W), xs, mod_s[2], mod_s[4], mod_s[3], mod_s[5], n2, wo, wrh, wrl, rbias,
        counts_p, wsg, wsu, wsd, Bs, 1)

    T = EXPERT_TILE
    n_tiles = ((Np + Bs) * TOP_K + N_EXPERTS * T + T - 1) // T
    pad_off, plan = _tile_plan(counts.reshape(N_EXPERTS).astype(jnp.int32), n_tiles)
    pad_off = pad_off.astype(F32).reshape(N_EXPERTS, 1)
    dest_p = _dest_rows(eidx_p, rank_p, pad_off)
    dest_s = _dest_rows(eidx_s, rank_s, pad_off)
    grouped = jnp.zeros((n_tiles * T * ROW_PIECES, LANES), jnp.uint32)
    grouped = _dispatch(dest_p, h2_p, grouped, tm)
    grouped = _dispatch(dest_s, h2_s, grouped, Bs)
    ys = _expert_matmul(grouped, plan, w_exp_gate, w_exp_up, w_exp_down)
    dest_all = jnp.concatenate([dest_p, dest_s], axis=1) * ROW_PIECES
    row_idx = jnp.concatenate([dest_all + c for c in range(ROW_PIECES)], axis=0)
    gathered = _gather_rows(ys, row_idx).reshape(row_idx.shape + (LANES,))
    tc = COMBINE_TILE
    y_p = _combine(base_p, w_p.T, mod_p[5], gathered, 0, tc, L // tc)
    y_s = _combine(base_s, w_s.T, mod_s[5], gathered, Np // Bs, Bs, 1)

    return (y_p.reshape(B, L, D), y_s.reshape(Bs, 1, D), state_p, cache_kp, cache_vp, state_s, cache_ks, cache_vs)


def kernel(x_prompt, x_sample, c_prompt, c_sample, state_ret, cache_win_k, cache_win_v, norm1, w_ada, b_ada, w_in,
           gn_ret, q_norm, k_norm, w_out, norm2, w_router, router_bias, w_exp_gate, w_exp_up, w_exp_down,
           w_sh_gate, w_sh_up, w_sh_down):
    assert w_in.shape[0] == 1, "single-layer step"
    assert x_sample.shape[1] == 1, "one new token per sample sequence"
    outs = _layer(x_prompt, x_sample, c_prompt, c_sample, state_ret[0], cache_win_k[0], cache_win_v[0],
                  norm1[0], w_ada[0], b_ada[0], w_in[0], gn_ret[0], q_norm[0], k_norm[0], w_out[0], norm2[0],
                  w_router[0], router_bias[0], w_exp_gate[0], w_exp_up[0], w_exp_down[0],
                  w_sh_gate[0], w_sh_up[0], w_sh_down[0])
    yp, ys, sp, kp, vp, ss, ksm, vsm = outs
    return (yp, ys, sp[None], kp[None], vp[None], ss[None], ksm[None], vsm[None])
```

```python
import functools

import jax
import jax.numpy as jnp
from jax import lax
from jax.experimental import pallas as pl
from jax.experimental.pallas import tpu as pltpu
from jax.experimental.pallas import tpu_sc as plsc

HEAD_DIM = 64
N_RET_HEADS = 8
N_ATT_HEADS = 8
RET_WIDTH = N_RET_HEADS * HEAD_DIM
ATT_WIDTH = N_ATT_HEADS * HEAD_DIM
RET_CHUNK = 128
DILATED_PATTERNS = ((128, 1), (512, 4), (2048, 16))
MAX_WINDOW = 2048
PAST_LEN = 8192
ROPE_THETA = 10000.0
N_EXPERTS = 256
TOP_K = 8
N_EXPERT_GROUPS = 8
TOPK_GROUPS = 4
ROUTED_SCALE = 2.5
NORM_EPS = 1e-6
GN_EPS = 1e-5
NEG_INF = -1e30

F32 = jnp.float32
BF16 = jnp.bfloat16

VMEM_LIMIT_BYTES = 56 * 1024 * 1024
EXPERT_TILE = 256
TOKEN_TILE = 512
COMBINE_TILE = 256
GATHER_WINDOW = 128
GATHER_CHUNKS = 4
DMA_QUEUES = 2
ISSUE_UNROLL = 4
RET_STEP_CHUNKS = 4
EXPERT_ROW_SLOTS = 3
DEST_TILE = 2048
ATTN_UNROLL = 8


def _cparams(*sem):
    return pltpu.CompilerParams(dimension_semantics=sem, vmem_limit_bytes=VMEM_LIMIT_BYTES)


def _dot(a, b):
    return jnp.dot(a, b, preferred_element_type=F32)


def _dot_nt(a, b):
    return lax.dot_general(a, b, (((1,), (1,)), ((), ())), preferred_element_type=F32)


def _dot_tn(a, b):
    return lax.dot_general(a, b, (((0,), (0,)), ((), ())), preferred_element_type=F32)


def _mod_kernel(c_ref, w_ref, b_ref, o_ref):
    c = c_ref[...]
    a = (c * jax.nn.sigmoid(c)).astype(BF16)
    o_ref[...] = _dot(a, w_ref[...].astype(BF16)) + b_ref[...]


def _modulation(c, w_ada, b_ada):
    R, D = c.shape
    n_out = w_ada.shape[1]
    tn = 1024
    return pl.pallas_call(
        _mod_kernel,
        grid=(n_out // tn,),
        in_specs=[
            pl.BlockSpec((R, D), lambda j: (0, 0)),
            pl.BlockSpec((D, tn), lambda j: (0, j)),
            pl.BlockSpec((1, tn), lambda j: (0, j)),
        ],
        out_specs=pl.BlockSpec((R, tn), lambda j: (0, j)),
        out_shape=jax.ShapeDtypeStruct((R, n_out), F32),
        compiler_params=_cparams("parallel"),
        name="modulation",
    )(c, w_ada, b_ada.reshape(1, n_out))


def _swap_halves(x, first_half):
    n = x.shape[-1]
    return jnp.where(first_half, pltpu.roll(x, n - HEAD_DIM // 2, 1), pltpu.roll(x, HEAD_DIM // 2, 1))


def _head_mean(x, g):
    hi = x.astype(BF16)
    lo = (x - hi.astype(F32)).astype(BF16)
    return _dot(hi, g) + _dot(lo, g)


def _in_proj_kernel(x_ref, sc_ref, sh_ref, cos_ref, sin_ref, n1_ref, qn_ref, kn_ref, g_ref, w_ref,
                    qr_ref, kr_ref, vr_ref, sg_ref, qa_ref, ka_ref, va_ref):
    x = x_ref[...]
    ms = jnp.mean(x * x, axis=-1, keepdims=True)
    h = x * lax.rsqrt(ms + NORM_EPS) * n1_ref[...]
    h = (h * (1.0 + sc_ref[0]) + sh_ref[0]).astype(BF16)
    cos = cos_ref[...]
    sin = sin_ref[...]
    W = RET_WIDTH
    lane = lax.broadcasted_iota(jnp.int32, (1, W), 1)
    first_half = (lane % HEAD_DIM) < (HEAD_DIM // 2)

    def proj(c):
        return _dot(h, w_ref[:, c * W:(c + 1) * W])

    def rot(t):
        return t * cos + _swap_halves(t, first_half) * sin

    def qk_norm(t, gain):
        return t * lax.rsqrt(_head_mean(t * t, g_ref[...]) + NORM_EPS) * gain

    qr_ref[...] = rot(proj(0))
    kr_ref[...] = rot(proj(1)) * (HEAD_DIM ** -0.5)
    vr_ref[...] = proj(2)
    gr = proj(3)
    sg_ref[...] = gr * jax.nn.sigmoid(gr)
    qa_ref[...] = rot(qk_norm(proj(4), qn_ref[...])) * (HEAD_DIM ** -0.5)
    ka_ref[...] = rot(qk_norm(proj(5), kn_ref[...]))
    va_ref[...] = proj(6)


def _in_proj(x, sc, sh, cos, sin, norm1, q_norm_t, k_norm_t, gmat, w_in_bf, tm, rows_per_mod, pos_blocks):
    N, D = x.shape
    W = RET_WIDTH
    R = sc.shape[1]
    mod_spec = pl.BlockSpec((1, R, D), lambda i: (i // rows_per_mod, 0, 0))
    pos_spec = pl.BlockSpec((cos.shape[0] // pos_blocks, W), lambda i: (i % pos_blocks, 0))
    const = lambda shape: pl.BlockSpec(shape, lambda i: (0,) * len(shape))
    out_spec = pl.BlockSpec((tm, W), lambda i: (i, 0))
    return pl.pallas_call(
        _in_proj_kernel,
        grid=(N // tm,),
        in_specs=[
            pl.BlockSpec((tm, D), lambda i: (i, 0)),
            mod_spec, mod_spec, pos_spec, pos_spec,
            const((1, D)), const((1, W)), const((1, W)), const((W, W)), const(w_in_bf.shape),
        ],
        out_specs=[out_spec] * 7,
        out_shape=[jax.ShapeDtypeStruct((N, W), F32)] * 7,
        compiler_params=_cparams("parallel"),
        name="in_proj",
    )(x, sc, sh, cos, sin, norm1, q_norm_t, k_norm_t, gmat, w_in_bf)


def _ret_kernel(q_ref, k_ref, v_ref, sg_ref, dm_ref, qd_ref, kd_ref, sd_ref, bd_ref, gm_ref, gn_ref, o_ref, s_ref):
    @pl.when(pl.program_id(1) == 0)
    def _():
        s_ref[...] = jnp.zeros_like(s_ref)

    C = dm_ref.shape[2]
    Dh = HEAD_DIM
    P2 = 2 * Dh
    n_pairs = N_RET_HEADS // 2
    lane = lax.broadcasted_iota(jnp.int32, (1, P2), 1)
    head0 = lane < Dh
    zero = jnp.zeros((Dh, Dh), F32)
    state = []
    for p in range(n_pairs):
        top = jnp.concatenate([s_ref[0, 2 * p], zero], axis=1)
        bot = jnp.concatenate([zero, s_ref[0, 2 * p + 1]], axis=1)
        state.append(jnp.concatenate([top, bot], axis=0))
    bd = bd_ref[...]
    gm = gm_ref[...]
    for j in range(q_ref.shape[1] // C):
        rows = slice(j * C, (j + 1) * C)
        for p in range(n_pairs):
            sl = slice(P2 * p, P2 * (p + 1))
            q = q_ref[0, rows, sl]
            k = k_ref[0, rows, sl]
            vb = v_ref[0, rows, sl].astype(BF16)
            kdb = (k * kd_ref[:, sl]).astype(BF16)
            q2 = jnp.concatenate([jnp.where(head0, q, 0.0), jnp.where(head0, 0.0, q)], axis=0).astype(BF16)
            sc = _dot_nt(q2, k.astype(BF16)) * dm_ref[p]
            o2 = _dot(sc.astype(BF16), vb)
            S = state[p]
            o = jnp.where(head0, o2[:C], o2[C:]) + _dot(q.astype(BF16), S.astype(BF16)) * qd_ref[:, sl]
            state[p] = S * sd_ref[p] + _dot_tn(kdb, vb) * bd
            o_ref[0, rows, sl] = o
    for p in range(n_pairs):
        s_ref[0, 2 * p] = state[p][:Dh, :Dh]
        s_ref[0, 2 * p + 1] = state[p][Dh:, Dh:]
    o = o_ref[0]
    oc = o - _head_mean(o, gm)
    var = _head_mean(oc * oc, gm)
    o_ref[0] = oc * lax.rsqrt(var + GN_EPS) * gn_ref[...] * sg_ref[0]


def _decay_tables(C):
    H = N_RET_HEADS
    log_g = jnp.log1p(-(2.0 ** (-5.0 - jnp.arange(H, dtype=F32))))
    i = jnp.arange(C)
    diff = i[:, None] - i[None, :]
    dmask = jnp.where(diff[None] >= 0, jnp.exp(log_g[:, None, None] * jnp.maximum(diff, 0)[None]), 0.0).astype(F32)
    q_decay = jnp.exp(log_g[None, :] * (i[:, None] + 1)).astype(F32)
    k_decay = jnp.exp(log_g[None, :] * (C - 1 - i)[:, None]).astype(F32)
    s_decay = jnp.exp(log_g * C).astype(F32)
    return dmask, q_decay, k_decay, s_decay


def _retention_prompt(q, k, v, sg, gn, gm):
    B, L, W = q.shape
    C = RET_CHUNK
    H, Dh = N_RET_HEADS, HEAD_DIM
    dmask, q_decay, k_decay, s_decay = _decay_tables(C)
    qd = jnp.repeat(q_decay, Dh, axis=1)
    kd = jnp.repeat(k_decay, Dh, axis=1)
    P2 = 2 * Dh
    pair_head = jnp.arange(P2) // Dh
    bd = (pair_head[:, None] == pair_head[None, :]).astype(F32)
    sd = s_decay.reshape(H // 2, 2)[:, pair_head][:, :, None] * bd[None]
    rows = C * RET_STEP_CHUNKS
    blk = pl.BlockSpec((1, rows, W), lambda b, c: (b, c, 0))
    const = lambda shape: pl.BlockSpec(shape, lambda b, c: (0,) * len(shape))
    return pl.pallas_call(
        _ret_kernel,
        grid=(B, L // rows),
        in_specs=[blk, blk, blk, blk, const((H // 2, 2 * C, C)), const((C, W)), const((C, W)),
                  const((H // 2, P2, P2)), const((P2, P2)), const((W, W)), const((1, W))],
        out_specs=[blk, pl.BlockSpec((1, H, Dh, Dh), lambda b, c: (b, 0, 0, 0))],
        out_shape=[jax.ShapeDtypeStruct((B, L, W), F32), jax.ShapeDtypeStruct((B, H, Dh, Dh), F32)],
        compiler_params=_cparams("parallel", "arbitrary"),
        name="retention_prompt",
    )(q, k, v, sg, dmask.reshape(H // 2, 2 * C, C), qd, kd, sd, bd, gm, gn)


def _attn_kernel(q_ref, k_ref, v_ref, o_ref, oacc, lacc):
    L = q_ref.shape[1]
    P2 = q_ref.shape[2]
    lane = lax.broadcasted_iota(jnp.int32, (1, P2), 1)
    head0 = lane < HEAD_DIM

    for p, (window, d) in enumerate(DILATED_PATTERNS):
        band = window // d
        nb = L // (d * band)
        qi = lax.broadcasted_iota(jnp.int32, (2 * band, 2 * band), 0) % band
        ki = lax.broadcasted_iota(jnp.int32, (2 * band, 2 * band), 1)
        dist = qi + band - ki
        in_band = (dist >= 0) & (dist <= band)
        bias_any = jnp.where(in_band, 0.0, NEG_INF)
        bias_first = jnp.where(in_band & (ki >= band), 0.0, NEG_INF)

        def block(idx, d=d, band=band, nb=nb, p=p, bias_any=bias_any, bias_first=bias_first):
            r = idx // nb
            n = idx % nb
            qs = r + d * band * n
            ps = jnp.maximum(qs - d * band, r)
            cur = pl.ds(qs, band, stride=d) if d > 1 else pl.ds(qs, band)
            prev = pl.ds(ps, band, stride=d) if d > 1 else pl.ds(ps, band)
            qb = q_ref[0, cur, :]
            kk = jnp.concatenate([k_ref[0, prev, :], k_ref[0, cur, :]], axis=0).astype(BF16)
            vv = jnp.concatenate([v_ref[0, prev, :], v_ref[0, cur, :]], axis=0).astype(BF16)
            q2 = jnp.concatenate([jnp.where(head0, qb, 0.0), jnp.where(head0, 0.0, qb)], axis=0).astype(BF16)
            s = _dot_nt(q2, kk) + jnp.where(n > 0, bias_any, bias_first)
            m = jnp.max(jnp.maximum(s[:, :band], s[:, band:]), axis=-1, keepdims=True)
            e = jnp.exp(s - m)
            l = jnp.sum(e[:, :band] + e[:, band:], axis=-1, keepdims=True)
            o = _dot(e.astype(BF16), vv) * (1.0 / l)
            lse = m + jnp.log(l)
            oacc[p, cur, :] = jnp.where(head0, o[:band], o[band:])
            lacc[p, cur, :] = jnp.where(head0, lse[:band], lse[band:])

        def body(i, carry, block=block):
            for u in range(ATTN_UNROLL):
                block(i * ATTN_UNROLL + u)
            return carry

        lax.fori_loop(0, d * nb // ATTN_UNROLL, body, 0)

    rows = 512

    def combine(i, carry):
        sl = pl.ds(pl.multiple_of(i * rows, rows), rows)
        l0, l1, l2 = lacc[0, sl, :], lacc[1, sl, :], lacc[2, sl, :]
        m = jnp.maximum(jnp.maximum(l0, l1), l2)
        w0, w1, w2 = jnp.exp(l0 - m), jnp.exp(l1 - m), jnp.exp(l2 - m)
        tot = w0 + w1 + w2
        o_ref[0, sl, :] = (w0 * oacc[0, sl, :] + w1 * oacc[1, sl, :] + w2 * oacc[2, sl, :]) / tot
        return carry

    lax.fori_loop(0, L // rows, combine, 0)


def _attention_prompt(q, k, v):
    B, L, W = q.shape
    P2 = 2 * HEAD_DIM
    blk = pl.BlockSpec((1, L, P2), lambda b, hp: (b, 0, hp))
    return pl.pallas_call(
        _attn_kernel,
        grid=(B, W // P2),
        in_specs=[blk, blk, blk],
        out_specs=blk,
        out_shape=jax.ShapeDtypeStruct((B, L, W), F32),
        scratch_shapes=[pltpu.VMEM((len(DILATED_PATTERNS), L, P2), F32)] * 2,
        compiler_params=_cparams("parallel", "parallel"),
        name="attention_prompt",
    )(q, k, v)


def _pack_bf16_pair(a, b):
    ua = pltpu.bitcast(a.astype(BF16).astype(F32), jnp.uint32)
    ub = pltpu.bitcast(b.astype(BF16).astype(F32), jnp.uint32)
    return ua | (ub >> 16)


def _unpack_bf16_pair(p):
    a = pltpu.bitcast(p & jnp.uint32(0xFFFF0000), F32)
    b = pltpu.bitcast(p << 16, F32)
    return a, b


ROUTE_LANES = 128
LANES = 128


ROW_PIECES = 4


def _store_rows(ref, packed):
    n = packed.shape[0]
    for c in range(ROW_PIECES):
        ref[pl.ds(c, n, stride=ROW_PIECES), :] = packed[:, c * LANES:(c + 1) * LANES]


def _load_rows(ref):
    n = ref.shape[0] // ROW_PIECES
    return jnp.concatenate([ref[pl.ds(c, n, stride=ROW_PIECES), :] for c in range(ROW_PIECES)], axis=1)


def _token_rows(t):
    return pl.ds(pl.multiple_of(t * ROW_PIECES, ROW_PIECES), ROW_PIECES)


def _route_chunk(s, bias, carry):
    E, n = s.shape
    G, GS = N_EXPERT_GROUPS, E // N_EXPERT_GROUPS
    NEG = -jnp.inf
    choice = s + bias
    row = lax.broadcasted_iota(jnp.int32, (E, n), 0).astype(F32)
    lrow = row[:GS]
    gs_rows = []
    for g in range(G):
        c = choice[g * GS:(g + 1) * GS]
        m1 = jnp.max(c, axis=0, keepdims=True)
        i1 = jnp.min(jnp.where(c == m1, lrow, float(GS)), axis=0, keepdims=True)
        m2 = jnp.max(jnp.where(lrow == i1, NEG, c), axis=0, keepdims=True)
        gs_rows.append(m1 + m2)
    gs = jnp.concatenate(gs_rows, axis=0)
    grow = lax.broadcasted_iota(jnp.int32, (G, n), 0).astype(F32)
    gsel = jnp.zeros((G, n), F32)
    for _ in range(TOPK_GROUPS):
        gm = jnp.max(gs, axis=0, keepdims=True)
        gi = jnp.min(jnp.where(gs == gm, grow, float(G)), axis=0, keepdims=True)
        hit = grow == gi
        gsel = jnp.where(hit, 1.0, gsel)
        gs = jnp.where(hit, NEG, gs)
    emask = jnp.concatenate([jnp.broadcast_to(gsel[g:g + 1], (GS, n)) for g in range(G)], axis=0)
    masked = jnp.where(emask > 0.5, choice, NEG)
    ids, sks = [], []
    member = jnp.zeros((E, n), F32)
    for _ in range(TOP_K):
        mk = jnp.max(masked, axis=0, keepdims=True)
        ik = jnp.min(jnp.where(masked == mk, row, float(E)), axis=0, keepdims=True)
        sel = row == ik
        sks.append(jnp.sum(jnp.where(sel, s, 0.0), axis=0, keepdims=True))
        masked = jnp.where(sel, NEG, masked)
        member = jnp.where(sel, 1.0, member)
        ids.append(ik)
    eidx = jnp.concatenate(ids, axis=0)
    sk = jnp.concatenate(sks, axis=0)
    w = sk / jnp.sum(sk, axis=0, keepdims=True) * ROUTED_SCALE
    ti = lax.broadcasted_iota(jnp.int32, (n, n), 0)
    tj = lax.broadcasted_iota(jnp.int32, (n, n), 1)
    upper = jnp.where(ti <= tj, 1.0, 0.0).astype(BF16)
    incl = _dot(member.astype(BF16), upper)
    rank_dense = carry + incl - member
    ranks = [jnp.sum(jnp.where(row == ids[k], rank_dense, 0.0), axis=0, keepdims=True) for k in range(TOP_K)]
    return eidx, w, jnp.concatenate(ranks, axis=0), carry + incl[:, n - 1:n]


def _post_kernel(or_ref, oa_ref, x_ref, g1_ref, sc_ref, sh_ref, g2_ref, n2_ref, wo_ref, wrh_ref, wrl_ref, rb_ref,
                 cin_ref, wsg_ref, wsu_ref, wsd_ref, base_ref, h2_ref, eidx_ref, w_ref, rank_ref, cnt_ref):
    @pl.when(pl.program_id(0) == 0)
    def _():
        cnt_ref[...] = cin_ref[...]

    W = RET_WIDTH
    mix = _dot(or_ref[...].astype(BF16), wo_ref[:W, :]) + _dot(oa_ref[...].astype(BF16), wo_ref[W:, :])
    x2 = x_ref[...] + g1_ref[0] * mix
    ms = jnp.mean(x2 * x2, axis=-1, keepdims=True)
    h2 = x2 * lax.rsqrt(ms + NORM_EPS) * n2_ref[...]
    h2 = h2 * (1.0 + sc_ref[0]) + sh_ref[0]
    hb = h2.astype(BF16)
    hl = (h2 - hb.astype(F32)).astype(BF16)
    half = h2.shape[1] // 2
    _store_rows(h2_ref, _pack_bf16_pair(h2[:, :half], h2[:, half:]))
    g = _dot(hb, wsg_ref[...])
    u = _dot(hb, wsu_ref[...])
    shared = _dot((g * jax.nn.sigmoid(g) * u).astype(BF16), wsd_ref[...])
    base_ref[...] = x2 + g2_ref[0] * shared

    logits = _dot_nt(wrh_ref[...], hb) + (_dot_nt(wrh_ref[...], hl) + _dot_nt(wrl_ref[...], hb))
    s = jax.nn.sigmoid(logits)
    carry = cnt_ref[...]
    n = ROUTE_LANES
    for j in range(s.shape[1] // n):
        sl = slice(j * n, (j + 1) * n)
        eidx, w, rank, carry = _route_chunk(s[:, sl], rb_ref[...], carry)
        eidx_ref[:, sl] = eidx.astype(jnp.int32)
        w_ref[:, sl] = w
        rank_ref[:, sl] = rank.astype(jnp.int32)
    cnt_ref[...] = carry


def _post_mixer(o_r, o_a, x, g1, sc2, sh2, g2, norm2, wo, wrh_t, wrl_t, rbias, counts_in, wsg, wsu, wsd, tm,
                rows_per_mod):
    N, D = x.shape
    assert D // 2 == ROW_PIECES * LANES
    W = RET_WIDTH
    E = wrh_t.shape[0]
    R = g1.shape[1]
    mod_spec = pl.BlockSpec((1, R, D), lambda i: (i // rows_per_mod, 0, 0))
    const = lambda a: pl.BlockSpec(a.shape, lambda i: (0,) * a.ndim)
    tok = lambda rows: pl.BlockSpec((rows, tm), lambda i: (0, i))
    return pl.pallas_call(
        _post_kernel,
        grid=(N // tm,),
        in_specs=[
            pl.BlockSpec((tm, W), lambda i: (i, 0)),
            pl.BlockSpec((tm, W), lambda i: (i, 0)),
            pl.BlockSpec((tm, D), lambda i: (i, 0)),
            mod_spec, mod_spec, mod_spec, mod_spec,
            const(norm2), const(wo), const(wrh_t), const(wrl_t), const(rbias), const(counts_in),
            const(wsg), const(wsu), const(wsd),
        ],
        out_specs=[
            pl.BlockSpec((tm, D), lambda i: (i, 0)),
            pl.BlockSpec((tm * ROW_PIECES, LANES), lambda i: (i, 0)),
            tok(TOP_K), tok(TOP_K), tok(TOP_K),
            pl.BlockSpec((E, 1), lambda i: (0, 0)),
        ],
        out_shape=[
            jax.ShapeDtypeStruct((N, D), F32),
            jax.ShapeDtypeStruct((N * ROW_PIECES, LANES), jnp.uint32),
            jax.ShapeDtypeStruct((TOP_K, N), jnp.int32),
            jax.ShapeDtypeStruct((TOP_K, N), F32),
            jax.ShapeDtypeStruct((TOP_K, N), jnp.int32),
            jax.ShapeDtypeStruct((E, 1), F32),
        ],
        compiler_params=_cparams("arbitrary"),
        name="post_mixer",
    )(o_r, o_a, x, g1, sc2, sh2, g2, norm2, wo, wrh_t, wrl_t, rbias, counts_in, wsg, wsu, wsd)


def _expert_kernel(be_ref, first_ref, nused_ref, nxt_ref, slot_ref, x_hbm, wg_hbm, wu_hbm, wd_hbm, y_ref,
                   xbuf, wg_f, wu_f, wd_f, wg_s, wu_s, wd_s, xsem, sem):
    i = pl.program_id(0)
    is_first = first_ref[i] == 1
    n_used = nused_ref[0]
    used = i < n_used
    tile_rows = xbuf.shape[1]
    x_slots = xbuf.shape[0]

    def weight_copies(e, s):
        return (pltpu.make_async_copy(wg_hbm.at[e], wg_f.at[s], sem.at[s, 0]),
                pltpu.make_async_copy(wu_hbm.at[e], wu_f.at[s], sem.at[s, 1]),
                pltpu.make_async_copy(wd_hbm.at[e], wd_f.at[s], sem.at[s, 2]))

    def x_copy(tile):
        s = tile % x_slots
        rows = pl.ds(pl.multiple_of(tile * tile_rows, tile_rows), tile_rows)
        return pltpu.make_async_copy(x_hbm.at[rows], xbuf.at[s], xsem.at[s])

    @pl.when(i == 0)
    def _():
        for t in range(x_slots - 1):
            @pl.when(t < n_used)
            def _():
                x_copy(t).start()

    @pl.when(used)
    def _():
        x_copy(i).wait()

    @pl.when(i + (x_slots - 1) < n_used)
    def _():
        x_copy(i + (x_slots - 1)).start()

    def compute():
        xa, xb = _unpack_bf16_pair(_load_rows(xbuf.at[i % x_slots]))
        half = xa.shape[1]
        x = jnp.concatenate([xa.astype(BF16), xb.astype(BF16)], axis=1)
        g = _dot(x, wg_s[...])
        u = _dot(x, wu_s[...])
        hmid = (g * jax.nn.sigmoid(g) * u).astype(BF16)
        _store_rows(y_ref, _pack_bf16_pair(_dot(hmid, wd_s[:, :half]), _dot(hmid, wd_s[:, half:])))

    @pl.when(i == 0)
    def _():
        for c in weight_copies(be_ref[0], 0):
            c.start()

    @pl.when(is_first)
    def _():
        s = slot_ref[i]
        for c in weight_copies(be_ref[i], s):
            c.wait()

        @pl.when(nxt_ref[i] >= 0)
        def _():
            for c in weight_copies(nxt_ref[i], 1 - s):
                c.start()

        wg_s[...] = wg_f[s].astype(BF16)
        wu_s[...] = wu_f[s].astype(BF16)
        wd_s[...] = wd_f[s].astype(BF16)
        compute()

    @pl.when(jnp.logical_and(jnp.logical_not(is_first), used))
    def _():
        compute()

    @pl.when(jnp.logical_not(used))
    def _():
        y_ref[...] = jnp.zeros_like(y_ref)


def _expert_matmul(xs, plan, w_gate, w_up, w_down):
    P = xs.shape[0] // ROW_PIECES
    T = EXPERT_TILE
    D, F = w_gate.shape[1], w_gate.shape[2]
    row = pl.BlockSpec((T * ROW_PIECES, LANES), lambda i, *_: (i, 0))
    hbm = pl.BlockSpec(memory_space=pl.ANY)
    return pl.pallas_call(
        _expert_kernel,
        grid_spec=pltpu.PrefetchScalarGridSpec(
            num_scalar_prefetch=5,
            grid=(P // T,),
            in_specs=[hbm, hbm, hbm, hbm],
            out_specs=row,
            scratch_shapes=[
                pltpu.VMEM((EXPERT_ROW_SLOTS, T * ROW_PIECES, LANES), jnp.uint32),
                pltpu.VMEM((2, D, F), F32), pltpu.VMEM((2, D, F), F32), pltpu.VMEM((2, F, D), F32),
                pltpu.VMEM((D, F), BF16), pltpu.VMEM((D, F), BF16), pltpu.VMEM((F, D), BF16),
                pltpu.SemaphoreType.DMA((EXPERT_ROW_SLOTS,)),
                pltpu.SemaphoreType.DMA((2, 3)),
            ],
        ),
        out_shape=jax.ShapeDtypeStruct(xs.shape, jnp.uint32),
        compiler_params=_cparams("arbitrary"),
        name="routed_experts",
    )(*plan, xs, w_gate, w_up, w_down)


def _dest_kernel(eidx_ref, rank_ref, off_ref, dest_ref):
    E = off_ref.shape[0]
    n = eidx_ref.shape[1]
    row = lax.broadcasted_iota(jnp.int32, (E, n), 0)
    off = off_ref[...]
    rows = []
    for k in range(eidx_ref.shape[0]):
        hit = row == eidx_ref[k:k + 1, :]
        rows.append(jnp.sum(jnp.where(hit, off, 0.0), axis=0, keepdims=True))
    dest_ref[...] = rank_ref[...] + jnp.concatenate(rows, axis=0).astype(jnp.int32)


def _dest_rows(eidx, rank, pad_off):
    K, N = eidx.shape
    tn = min(N, DEST_TILE)
    blk = pl.BlockSpec((K, tn), lambda i: (0, i))
    return pl.pallas_call(
        _dest_kernel,
        grid=(N // tn,),
        in_specs=[blk, blk, pl.BlockSpec(pad_off.shape, lambda i: (0, 0))],
        out_specs=blk,
        out_shape=jax.ShapeDtypeStruct((K, N), jnp.int32),
        compiler_params=_cparams("parallel"),
        name="dest_rows",
    )(eidx, rank, pad_off)


def _row_copy(src, dst, sem):
    return pltpu.make_async_copy(src, dst, sem)


def _dispatch_kernel(dest_ref, h_ref, xs_in_ref, xs_ref, sem):
    del xs_in_ref
    K, tm = dest_ref.shape

    def issue(t, carry):
        for k in range(K):
            _row_copy(h_ref.at[_token_rows(t)], xs_ref.at[_token_rows(dest_ref[k, t])], sem).start(
                priority=k % DMA_QUEUES)
        return carry

    lax.fori_loop(0, tm, issue, 0, unroll=ISSUE_UNROLL)
    for _ in range(K):
        _row_copy(h_ref, xs_ref.at[pl.ds(0, tm * ROW_PIECES)], sem).wait()


def _dispatch(dest, h2p, xs, tm):
    K, N = dest.shape
    return pl.pallas_call(
        _dispatch_kernel,
        grid=(N // tm,),
        in_specs=[
            pl.BlockSpec((K, tm), lambda i: (0, i), memory_space=pltpu.SMEM),
            pl.BlockSpec((tm * ROW_PIECES, LANES), lambda i: (i, 0)),
            pl.BlockSpec(memory_space=pl.ANY),
        ],
        out_specs=pl.BlockSpec(memory_space=pl.ANY),
        out_shape=jax.ShapeDtypeStruct(xs.shape, xs.dtype),
        scratch_shapes=[pltpu.SemaphoreType.DMA],
        input_output_aliases={2: 0},
        compiler_params=_cparams("arbitrary"),
        name="dispatch",
    )(dest, h2p, xs)


def _gather_rows(src, idx):
    G, n = idx.shape
    nb = n // GATHER_WINDOW
    mesh = plsc.VectorSubcoreMesh(core_axis_name="core", subcore_axis_name="subcore")

    @pl.kernel(out_type=jax.ShapeDtypeStruct((G * n, src.shape[1]), src.dtype), mesh=mesh, scratch_types=[])
    def gather_kernel(src_hbm, idx_hbm, out_hbm):
        def body(idx_vmem, out_vmem):
            pltpu.sync_copy(src_hbm.at[idx_vmem.at[0]], out_vmem)

        pltpu.emit_pipeline(
            body,
            grid=(G, nb),
            in_specs=[pl.BlockSpec((1, GATHER_WINDOW), index_map=lambda g, j: (g, j))],
            out_specs=[pl.BlockSpec((GATHER_WINDOW, src.shape[1]), index_map=lambda g, j: (g * nb + j, 0))],
            core_axis_name=("core", "subcore"),
            dimension_semantics=(pltpu.PARALLEL, pltpu.PARALLEL),
        )(idx_hbm, out_hbm)

    return gather_kernel(src, idx)


def _combine_kernel(base_ref, w_ref, g2_ref, rows_ref, o_ref):
    tm = base_ref.shape[0]
    half = ROW_PIECES * LANES
    w = w_ref[...]
    acc_a = jnp.zeros((tm, half), F32)
    acc_b = jnp.zeros((tm, half), F32)
    for k in range(TOP_K):
        packed = jnp.concatenate([rows_ref[c * TOP_K + k] for c in range(ROW_PIECES)], axis=1)
        a, b = _unpack_bf16_pair(packed)
        wk = w[:, k:k + 1]
        acc_a = acc_a + wk * a
        acc_b = acc_b + wk * b
    g2 = g2_ref[0]
    o_ref[:, :half] = base_ref[:, :half] + g2[:, :half] * acc_a
    o_ref[:, half:] = base_ref[:, half:] + g2[:, half:] * acc_b


def _combine(base, w_tok, g2, gathered, token_block, n_blocks, gather_block, tm, rows_per_mod):
    N, D = base.shape
    K = w_tok.shape[1]
    R = g2.shape[1]
    planes = gathered.shape[0]
    return pl.pallas_call(
        _combine_kernel,
        grid=(n_blocks,),
        in_specs=[
            pl.BlockSpec((tm, D), lambda i: (token_block + i, 0)),
            pl.BlockSpec((tm, K), lambda i: (token_block + i, 0)),
            pl.BlockSpec((1, R, D), lambda i: ((token_block + i) // rows_per_mod, 0, 0)),
            pl.BlockSpec((planes, tm, LANES), lambda i: (0, gather_block + i, 0)),
        ],
        out_specs=pl.BlockSpec((tm, D), lambda i: (token_block + i, 0)),
        out_shape=jax.ShapeDtypeStruct((N, D), F32),
        input_output_aliases={0: 0},
        compiler_params=_cparams("parallel"),
        name="combine",
    )(base, w_tok, g2, gathered)


def _tile_plan(counts, n_tiles):
    T = EXPERT_TILE
    E = counts.shape[0]
    padded = (counts + T - 1) // T * T
    pad_end = jnp.cumsum(padded)
    pad_off = pad_end - padded
    idx = jnp.arange(n_tiles, dtype=jnp.int32)
    tile_start = idx * T
    blk_exp = jnp.minimum(jnp.sum((pad_end[None, :] <= tile_start[:, None]).astype(jnp.int32), axis=1), E - 1)
    used = tile_start < pad_end[-1]
    changed = jnp.concatenate([jnp.ones((1,), bool), blk_exp[1:] != blk_exp[:-1]])
    first = jnp.logical_and(used, changed)
    next_first = lax.cummin(jnp.where(first, idx, n_tiles), reverse=True)
    next_first = jnp.concatenate([next_first[1:], jnp.full((1,), n_tiles, jnp.int32)])
    nxt = jnp.where(next_first < n_tiles, blk_exp[jnp.minimum(next_first, n_tiles - 1)], -1)
    slot = (jnp.cumsum(first.astype(jnp.int32)) - 1) % 2
    n_used = (pad_end[-1] // T).reshape(1)
    as_i32 = lambda a: a.astype(jnp.int32)
    return pad_off, (as_i32(blk_exp), as_i32(first), as_i32(n_used), as_i32(nxt), as_i32(slot))


def _ret_sample_kernel(q_ref, k_ref, v_ref, sg_ref, gn_ref, dec_ref, s_ref, o_ref, so_ref):
    q = q_ref[0]
    k = k_ref[0]
    v = v_ref[0]
    gamma = dec_ref[pl.program_id(0)]
    qk = jnp.sum(q * k, axis=0, keepdims=True)
    o = qk * v
    Dh = q.shape[0]
    acc = jnp.zeros_like(v)
    for d in range(Dh):
        S = s_ref[0, d]
        acc = acc + q[d:d + 1, :] * S
        so_ref[0, d] = S * gamma + k[d:d + 1, :] * v
    o = o + acc * gamma
    mu = jnp.mean(o, axis=0, keepdims=True)
    oc = o - mu
    var = jnp.mean(oc * oc, axis=0, keepdims=True)
    o_ref[0] = oc * lax.rsqrt(var + GN_EPS) * gn_ref[0] * sg_ref[0]


def _retention_sample(qT, kT, vT, sgT, gn_col, gamma, S):
    H, Dh, Bt = qT.shape
    vec = pl.BlockSpec((1, Dh, Bt), lambda h: (h, 0, 0))
    st = pl.BlockSpec((1, Dh, Dh, Bt), lambda h: (h, 0, 0, 0))
    return pl.pallas_call(
        _ret_sample_kernel,
        grid=(H,),
        in_specs=[vec, vec, vec, vec, pl.BlockSpec((1, Dh, 1), lambda h: (h, 0, 0)),
                  pl.BlockSpec(memory_space=pltpu.SMEM), st],
        out_specs=[vec, st],
        out_shape=[jax.ShapeDtypeStruct((H, Dh, Bt), F32), jax.ShapeDtypeStruct((H, Dh, Dh, Bt), F32)],
        compiler_params=_cparams("parallel"),
        name="retention_sample",
    )(qT, kT, vT, sgT, gn_col, gamma, S)


def _attn_sample_kernel(q_ref, kn_ref, vn_ref, knT_ref, vnT_ref, kb_ref, vb_ref, o_ref, ko_ref, vo_ref):
    b = pl.program_id(0)
    H, Dh, Wn = kb_ref.shape[1], kb_ref.shape[2], kb_ref.shape[3]
    HD = H * Dh
    q = q_ref[0]
    kn = kn_ref[0]
    vn = vn_ref[0]
    row = lax.broadcasted_iota(jnp.int32, (H, HD), 0)
    col = lax.broadcasted_iota(jnp.int32, (H, HD), 1)
    own = (col // Dh) == row
    q_bd = jnp.where(own, q, 0.0)
    s_new = jnp.sum(q_bd * kn, axis=1, keepdims=True)

    bl = lax.broadcasted_iota(jnp.int32, knT_ref.shape, 1)
    k_col = jnp.sum(jnp.where(bl == b, knT_ref[...], 0.0), axis=1, keepdims=True)
    v_col = jnp.sum(jnp.where(bl == b, vnT_ref[...], 0.0), axis=1, keepdims=True)

    t = lax.broadcasted_iota(jnp.int32, (1, Wn), 1)
    last = t == (Wn - 1)
    prow = lax.broadcasted_iota(jnp.int32, (8, Wn), 0)
    q_bf = q_bd.astype(BF16)
    outs = []
    for h in range(H):
        K = kb_ref[0, h]
        V = vb_ref[0, h]
        hs = slice(h * Dh, (h + 1) * Dh)
        ko_ref[0, h] = jnp.where(last, k_col[hs], pltpu.roll(K, Wn - 1, 1))
        vo_ref[0, h] = jnp.where(last, v_col[hs], pltpu.roll(V, Wn - 1, 1))
        s = _dot(q_bf[:, hs], K.astype(BF16))[h:h + 1]
        sn = s_new[h:h + 1]
        Vb = V.astype(BF16)
        vn_h = vn[:, hs]
        lse_p, self_p = [], []
        probs = jnp.zeros((8, Wn), F32)
        for p, (window, d) in enumerate(DILATED_PATTERNS):
            valid = (((Wn - t) % d) == 0) & (t >= Wn - window)
            sm = jnp.where(valid, s, NEG_INF)
            m = jnp.maximum(jnp.max(sm, axis=1, keepdims=True), sn)
            e = jnp.exp(sm - m)
            en = jnp.exp(sn - m)
            l = jnp.sum(e, axis=1, keepdims=True) + en
            probs = jnp.where(prow == p, e / l, probs)
            self_p.append(en / l)
            lse_p.append(m + jnp.log(l))
        pv = _dot_nt(probs.astype(BF16), Vb)
        o_p = [pv[p:p + 1] + self_p[p] * vn_h for p in range(len(DILATED_PATTERNS))]
        mm = jnp.maximum(jnp.maximum(lse_p[0], lse_p[1]), lse_p[2])
        ws = [jnp.exp(x - mm) for x in lse_p]
        tot = ws[0] + ws[1] + ws[2]
        outs.append((ws[0] * o_p[0] + ws[1] * o_p[1] + ws[2] * o_p[2]) / tot)
    o_ref[0] = jnp.concatenate(outs, axis=1)


def _attention_sample(q, kn, vn, knT, vnT, k_buf, v_buf):
    Bt, H, Dh, Wn = k_buf.shape
    HD = H * Dh
    vec = pl.BlockSpec((1, 1, HD), lambda b: (b, 0, 0))
    tr = pl.BlockSpec((HD, Bt), lambda b: (0, 0))
    cache = pl.BlockSpec((1, H, Dh, Wn), lambda b: (b, 0, 0, 0))
    return pl.pallas_call(
        _attn_sample_kernel,
        grid=(Bt,),
        in_specs=[vec, vec, vec, tr, tr, cache, cache],
        out_specs=[vec, cache, cache],
        out_shape=[jax.ShapeDtypeStruct((Bt, 1, HD), F32),
                   jax.ShapeDtypeStruct(k_buf.shape, F32), jax.ShapeDtypeStruct(v_buf.shape, F32)],
        compiler_params=_cparams("parallel"),
        name="attention_sample",
    )(q, kn, vn, knT, vnT, k_buf, v_buf)


def _rope_tables(pos):
    half = HEAD_DIM // 2
    inv = ROPE_THETA ** (-jnp.arange(half, dtype=F32) / half)
    ang = pos.astype(F32)[:, None] * inv[None, :]
    cos, sin = jnp.cos(ang), jnp.sin(ang)
    cos_t = jnp.tile(cos, (1, 2 * N_RET_HEADS))
    sin_t = jnp.tile(jnp.concatenate([-sin, sin], axis=1), (1, N_RET_HEADS))
    return cos_t, sin_t


def _layer(x_prompt, x_sample, c_prompt, c_sample, state_ret, cache_k, cache_v,
           norm1, w_ada, b_ada, w_in, gn_ret, q_norm, k_norm, w_out, norm2,
           w_router, router_bias, w_exp_gate, w_exp_up, w_exp_down, w_sh_gate, w_sh_up, w_sh_down):
    B, L, D = x_prompt.shape
    Bs = x_sample.shape[0]
    H, Dh, W = N_RET_HEADS, HEAD_DIM, RET_WIDTH
    Np = B * L

    pad = (-(B + Bs)) % 8
    c_all = jnp.concatenate([c_prompt, c_sample, jnp.zeros((pad, D), F32)], axis=0)
    mod = _modulation(c_all, w_ada, b_ada)
    mod_p = [m.reshape(B, 1, D) for m in jnp.split(mod[:B], 6, axis=-1)]
    mod_s = [m.reshape(1, Bs, D) for m in jnp.split(mod[B:B + Bs], 6, axis=-1)]

    w_in_bf = w_in.astype(BF16)
    gi = jnp.arange(W) // Dh
    gmat = jnp.where(gi[:, None] == gi[None, :], 1.0 / Dh, 0.0).astype(BF16)
    qn_t = jnp.tile(q_norm, N_ATT_HEADS).reshape(1, W)
    kn_t = jnp.tile(k_norm, N_ATT_HEADS).reshape(1, W)
    n1 = norm1.reshape(1, D)
    n2 = norm2.reshape(1, D)
    gn = gn_ret.reshape(1, W)

    tm = TOKEN_TILE
    cos_p, sin_p = _rope_tables(jnp.arange(L, dtype=jnp.int32))
    cos_s, sin_s = _rope_tables(PAST_LEN + jnp.arange(1, dtype=jnp.int32))

    xp = x_prompt.reshape(Np, D)
    xs = x_sample.reshape(Bs, D)
    proj_p = _in_proj(xp, mod_p[1], mod_p[0], cos_p, sin_p, n1, qn_t, kn_t, gmat, w_in_bf, tm, L // tm, L // tm)
    proj_s = _in_proj(xs, mod_s[1], mod_s[0], cos_s, sin_s, n1, qn_t, kn_t, gmat, w_in_bf, Bs, 1, 1)
    qr, kr, vr, sg, qa, ka, va = [t.reshape(B, L, W) for t in proj_p]
    qr_s, kr_s, vr_s, sg_s, qa_s, ka_s, va_s = proj_s

    o_r, state_p = _retention_prompt(qr, kr, vr, sg, gn, gmat)
    o_a = _attention_prompt(qa, ka, va)
    keep = min(MAX_WINDOW, L)
    cache_kp = ka[:, L - keep:].reshape(B, keep, N_ATT_HEADS, Dh)
    cache_vp = va[:, L - keep:].reshape(B, keep, N_ATT_HEADS, Dh)

    log_g = jnp.log1p(-(2.0 ** (-5.0 - jnp.arange(H, dtype=F32))))
    gamma = jnp.exp(log_g).astype(F32)
    tr = lambda t: t.T.reshape(H, Dh, Bs)
    S_t = jnp.transpose(state_ret, (1, 2, 3, 0))
    orT, S_new_t = _retention_sample(tr(qr_s), tr(kr_s), tr(vr_s), tr(sg_s), gn_ret.reshape(H, Dh, 1), gamma, S_t)
    o_r_s = orT.reshape(W, Bs).T
    state_s = jnp.transpose(S_new_t, (3, 0, 1, 2))

    kb_t = jnp.transpose(cache_k, (0, 2, 3, 1))
    vb_t = jnp.transpose(cache_v, (0, 2, 3, 1))
    o_a_s, ko_t, vo_t = _attention_sample(qa_s.reshape(Bs, 1, W), ka_s.reshape(Bs, 1, W), va_s.reshape(Bs, 1, W),
                                          ka_s.T, va_s.T, kb_t, vb_t)
    cache_ks = jnp.transpose(ko_t, (0, 3, 1, 2))
    cache_vs = jnp.transpose(vo_t, (0, 3, 1, 2))

    wo = w_out.astype(BF16)
    wr_t = w_router.T
    wrh = wr_t.astype(BF16)
    wrl = (wr_t - wrh.astype(F32)).astype(BF16)
    rbias = router_bias.astype(F32).reshape(N_EXPERTS, 1)
    wsg, wsu, wsd = w_sh_gate.astype(BF16), w_sh_up.astype(BF16), w_sh_down.astype(BF16)
    zero_counts = jnp.zeros((N_EXPERTS, 1), F32)
    base_p, h2_p, eidx_p, w_p, rank_p, counts_p = _post_mixer(
        o_r.reshape(Np, W), o_a.reshape(Np, W), xp, mod_p[2], mod_p[4], mod_p[3], mod_p[5], n2, wo, wrh, wrl, rbias,
        zero_counts, wsg, wsu, wsd, tm, L // tm)
    base_s, h2_s, eidx_s, w_s, rank_s, counts = _post_mixer(
        o_r_s, o_a_s.reshape(Bs, W), xs, mod_s[2], mod_s[4], mod_s[3], mod_s[5], n2, wo, wrh, wrl, rbias,
        counts_p, wsg, wsu, wsd, Bs, 1)

    T = EXPERT_TILE
    n_tiles = ((Np + Bs) * TOP_K + N_EXPERTS * T + T - 1) // T
    pad_off, plan = _tile_plan(counts.reshape(N_EXPERTS).astype(jnp.int32), n_tiles)
    pad_off = pad_off.astype(F32).reshape(N_EXPERTS, 1)
    dest_p = _dest_rows(eidx_p, rank_p, pad_off)
    dest_s = _dest_rows(eidx_s, rank_s, pad_off)
    grouped = jnp.zeros((n_tiles * T * ROW_PIECES, LANES), jnp.uint32)
    grouped = _dispatch(dest_p, h2_p, grouped, tm)
    grouped = _dispatch(dest_s, h2_s, grouped, Bs)
    ys = _expert_matmul(grouped, plan, w_exp_gate, w_exp_up, w_exp_down)
    tc = COMBINE_TILE
    chunk = Np // GATHER_CHUNKS
    first_rows_p = dest_p * ROW_PIECES
    first_rows_s = dest_s * ROW_PIECES
    wt_p = w_p.T
    y_p = base_p
    for j in range(GATHER_CHUNKS):
        cols = first_rows_p[:, j * chunk:(j + 1) * chunk]
        last = j == GATHER_CHUNKS - 1
        if last:
            cols = jnp.concatenate([cols, first_rows_s], axis=1)
        row_idx = jnp.concatenate([cols + c for c in range(ROW_PIECES)], axis=0)
        gathered = _gather_rows(ys, row_idx).reshape(row_idx.shape + (LANES,))
        y_p = _combine(y_p, wt_p, mod_p[5], gathered, j * chunk // tc, chunk // tc, 0, tc, L // tc)
        if last:
            y_s = _combine(base_s, w_s.T, mod_s[5], gathered, 0, 1, chunk // Bs, Bs, 1)

    return (y_p.reshape(B, L, D), y_s.reshape(Bs, 1, D), state_p, cache_kp, cache_vp, state_s, cache_ks, cache_vs)


def kernel(x_prompt, x_sample, c_prompt, c_sample, state_ret, cache_win_k, cache_win_v, norm1, w_ada, b_ada, w_in,
           gn_ret, q_norm, k_norm, w_out, norm2, w_router, router_bias, w_exp_gate, w_exp_up, w_exp_down,
           w_sh_gate, w_sh_up, w_sh_down):
    assert w_in.shape[0] == 1, "single-layer step"
    assert x_sample.shape[1] == 1, "one new token per sample sequence"
    outs = _layer(x_prompt, x_sample, c_prompt, c_sample, state_ret[0], cache_win_k[0], cache_win_v[0],
                  norm1[0], w_ada[0], b_ada[0], w_in[0], gn_ret[0], q_norm[0], k_norm[0], w_out[0], norm2[0],
                  w_router[0], router_bias[0], w_exp_gate[0], w_exp_up[0], w_exp_down[0],
                  w_sh_gate[0], w_sh_up[0], w_sh_down[0])
    yp, ys, sp, kp, vp, ss, ksm, vsm = outs
    return (yp, ys, sp[None], kp[None], vp[None], ss[None], ksm[None], vsm[None])
```

```python
import functools

import jax
import jax.numpy as jnp
from jax import lax
from jax.experimental import pallas as pl
from jax.experimental.pallas import tpu as pltpu
from jax.experimental.pallas import tpu_sc as plsc

HEAD_DIM = 64
N_RET_HEADS = 8
N_ATT_HEADS = 8
RET_WIDTH = N_RET_HEADS * HEAD_DIM
ATT_WIDTH = N_ATT_HEADS * HEAD_DIM
RET_CHUNK = 128
DILATED_PATTERNS = ((128, 1), (512, 4), (2048, 16))
MAX_WINDOW = 2048
PAST_LEN = 8192
ROPE_THETA = 10000.0
N_EXPERTS = 256
TOP_K = 8
N_EXPERT_GROUPS = 8
TOPK_GROUPS = 4
ROUTED_SCALE = 2.5
NORM_EPS = 1e-6
GN_EPS = 1e-5
NEG_INF = -1e30

F32 = jnp.float32
BF16 = jnp.bfloat16

VMEM_LIMIT_BYTES = 56 * 1024 * 1024
EXPERT_TILE = 256
TOKEN_TILE = 512
COMBINE_TILE = 256
GATHER_WINDOW = 128
GATHER_CHUNKS = 4
DMA_QUEUES = 2
ISSUE_UNROLL = 4
RET_STEP_CHUNKS = 4
EXPERT_ROW_SLOTS = 3
DEST_TILE = 2048
ATTN_UNROLL = 8


def _cparams(*sem):
    return pltpu.CompilerParams(dimension_semantics=sem, vmem_limit_bytes=VMEM_LIMIT_BYTES)


def _dot(a, b):
    return jnp.dot(a, b, preferred_element_type=F32)


def _dot_nt(a, b):
    return lax.dot_general(a, b, (((1,), (1,)), ((), ())), preferred_element_type=F32)


def _dot_tn(a, b):
    return lax.dot_general(a, b, (((0,), (0,)), ((), ())), preferred_element_type=F32)


def _mod_kernel(c_ref, w_ref, b_ref, o_ref):
    c = c_ref[...]
    a = (c * jax.nn.sigmoid(c)).astype(BF16)
    o_ref[...] = _dot(a, w_ref[...].astype(BF16)) + b_ref[...]


def _modulation(c, w_ada, b_ada):
    R, D = c.shape
    n_out = w_ada.shape[1]
    tn = 1024
    return pl.pallas_call(
        _mod_kernel,
        grid=(n_out // tn,),
        in_specs=[
            pl.BlockSpec((R, D), lambda j: (0, 0)),
            pl.BlockSpec((D, tn), lambda j: (0, j)),
            pl.BlockSpec((1, tn), lambda j: (0, j)),
        ],
        out_specs=pl.BlockSpec((R, tn), lambda j: (0, j)),
        out_shape=jax.ShapeDtypeStruct((R, n_out), F32),
        compiler_params=_cparams("parallel"),
        name="modulation",
    )(c, w_ada, b_ada.reshape(1, n_out))


def _swap_halves(x, first_half):
    n = x.shape[-1]
    return jnp.where(first_half, pltpu.roll(x, n - HEAD_DIM // 2, 1), pltpu.roll(x, HEAD_DIM // 2, 1))


def _head_mean(x, g):
    hi = x.astype(BF16)
    lo = (x - hi.astype(F32)).astype(BF16)
    return _dot(hi, g) + _dot(lo, g)


def _in_proj_kernel(x_ref, sc_ref, sh_ref, cos_ref, sin_ref, n1_ref, qn_ref, kn_ref, g_ref, w_ref,
                    qr_ref, kr_ref, vr_ref, sg_ref, qa_ref, ka_ref, va_ref):
    x = x_ref[...]
    ms = jnp.mean(x * x, axis=-1, keepdims=True)
    h = x * lax.rsqrt(ms + NORM_EPS) * n1_ref[...]
    h = (h * (1.0 + sc_ref[0]) + sh_ref[0]).astype(BF16)
    cos = cos_ref[...]
    sin = sin_ref[...]
    W = RET_WIDTH
    lane = lax.broadcasted_iota(jnp.int32, (1, W), 1)
    first_half = (lane % HEAD_DIM) < (HEAD_DIM // 2)

    def proj(c):
        return _dot(h, w_ref[:, c * W:(c + 1) * W])

    def rot(t):
        return t * cos + _swap_halves(t, first_half) * sin

    def qk_norm(t, gain):
        return t * lax.rsqrt(_head_mean(t * t, g_ref[...]) + NORM_EPS) * gain

    qr_ref[...] = rot(proj(0))
    kr_ref[...] = rot(proj(1)) * (HEAD_DIM ** -0.5)
    vr_ref[...] = proj(2)
    gr = proj(3)
    sg_ref[...] = gr * jax.nn.sigmoid(gr)
    qa_ref[...] = rot(qk_norm(proj(4), qn_ref[...])) * (HEAD_DIM ** -0.5)
    ka_ref[...] = rot(qk_norm(proj(5), kn_ref[...]))
    va_ref[...] = proj(6)


def _in_proj(x, sc, sh, cos, sin, norm1, q_norm_t, k_norm_t, gmat, w_in_bf, tm, rows_per_mod, pos_blocks):
    N, D = x.shape
    W = RET_WIDTH
    R = sc.shape[1]
    mod_spec = pl.BlockSpec((1, R, D), lambda i: (i // rows_per_mod, 0, 0))
    pos_spec = pl.BlockSpec((cos.shape[0] // pos_blocks, W), lambda i: (i % pos_blocks, 0))
    const = lambda shape: pl.BlockSpec(shape, lambda i: (0,) * len(shape))
    out_spec = pl.BlockSpec((tm, W), lambda i: (i, 0))
    return pl.pallas_call(
        _in_proj_kernel,
        grid=(N // tm,),
        in_specs=[
            pl.BlockSpec((tm, D), lambda i: (i, 0)),
            mod_spec, mod_spec, pos_spec, pos_spec,
            const((1, D)), const((1, W)), const((1, W)), const((W, W)), const(w_in_bf.shape),
        ],
        out_specs=[out_spec] * 7,
        out_shape=[jax.ShapeDtypeStruct((N, W), F32)] * 7,
        compiler_params=_cparams("parallel"),
        name="in_proj",
    )(x, sc, sh, cos, sin, norm1, q_norm_t, k_norm_t, gmat, w_in_bf)


def _ret_kernel(q_ref, k_ref, v_ref, sg_ref, dm_ref, qd_ref, kd_ref, sd_ref, bd_ref, gm_ref, gn_ref, o_ref, s_ref):
    @pl.when(pl.program_id(1) == 0)
    def _():
        s_ref[...] = jnp.zeros_like(s_ref)

    C = dm_ref.shape[2]
    Dh = HEAD_DIM
    P2 = 2 * Dh
    n_pairs = N_RET_HEADS // 2
    lane = lax.broadcasted_iota(jnp.int32, (1, P2), 1)
    head0 = lane < Dh
    zero = jnp.zeros((Dh, Dh), F32)
    state = []
    for p in range(n_pairs):
        top = jnp.concatenate([s_ref[0, 2 * p], zero], axis=1)
        bot = jnp.concatenate([zero, s_ref[0, 2 * p + 1]], axis=1)
        state.append(jnp.concatenate([top, bot], axis=0))
    bd = bd_ref[...]
    gm = gm_ref[...]
    for j in range(q_ref.shape[1] // C):
        rows = slice(j * C, (j + 1) * C)
        for p in range(n_pairs):
            sl = slice(P2 * p, P2 * (p + 1))
            q = q_ref[0, rows, sl]
            k = k_ref[0, rows, sl]
            vb = v_ref[0, rows, sl].astype(BF16)
            kdb = (k * kd_ref[:, sl]).astype(BF16)
            q2 = jnp.concatenate([jnp.where(head0, q, 0.0), jnp.where(head0, 0.0, q)], axis=0).astype(BF16)
            sc = _dot_nt(q2, k.astype(BF16)) * dm_ref[p]
            o2 = _dot(sc.astype(BF16), vb)
            S = state[p]
            o = jnp.where(head0, o2[:C], o2[C:]) + _dot(q.astype(BF16), S.astype(BF16)) * qd_ref[:, sl]
            state[p] = S * sd_ref[p] + _dot_tn(kdb, vb) * bd
            o_ref[0, rows, sl] = o
    for p in range(n_pairs):
        s_ref[0, 2 * p] = state[p][:Dh, :Dh]
        s_ref[0, 2 * p + 1] = state[p][Dh:, Dh:]
    o = o_ref[0]
    oc = o - _head_mean(o, gm)
    var = _head_mean(oc * oc, gm)
    o_ref[0] = oc * lax.rsqrt(var + GN_EPS) * gn_ref[...] * sg_ref[0]


def _decay_tables(C):
    H = N_RET_HEADS
    log_g = jnp.log1p(-(2.0 ** (-5.0 - jnp.arange(H, dtype=F32))))
    i = jnp.arange(C)
    diff = i[:, None] - i[None, :]
    dmask = jnp.where(diff[None] >= 0, jnp.exp(log_g[:, None, None] * jnp.maximum(diff, 0)[None]), 0.0).astype(F32)
    q_decay = jnp.exp(log_g[None, :] * (i[:, None] + 1)).astype(F32)
    k_decay = jnp.exp(log_g[None, :] * (C - 1 - i)[:, None]).astype(F32)
    s_decay = jnp.exp(log_g * C).astype(F32)
    return dmask, q_decay, k_decay, s_decay


def _retention_prompt(q, k, v, sg, gn, gm):
    B, L, W = q.shape
    C = RET_CHUNK
    H, Dh = N_RET_HEADS, HEAD_DIM
    dmask, q_decay, k_decay, s_decay = _decay_tables(C)
    qd = jnp.repeat(q_decay, Dh, axis=1)
    kd = jnp.repeat(k_decay, Dh, axis=1)
    P2 = 2 * Dh
    pair_head = jnp.arange(P2) // Dh
    bd = (pair_head[:, None] == pair_head[None, :]).astype(F32)
    sd = s_decay.reshape(H // 2, 2)[:, pair_head][:, :, None] * bd[None]
    rows = C * RET_STEP_CHUNKS
    blk = pl.BlockSpec((1, rows, W), lambda b, c: (b, c, 0))
    const = lambda shape: pl.BlockSpec(shape, lambda b, c: (0,) * len(shape))
    return pl.pallas_call(
        _ret_kernel,
        grid=(B, L // rows),
        in_specs=[blk, blk, blk, blk, const((H // 2, 2 * C, C)), const((C, W)), const((C, W)),
                  const((H // 2, P2, P2)), const((P2, P2)), const((W, W)), const((1, W))],
        out_specs=[blk, pl.BlockSpec((1, H, Dh, Dh), lambda b, c: (b, 0, 0, 0))],
        out_shape=[jax.ShapeDtypeStruct((B, L, W), F32), jax.ShapeDtypeStruct((B, H, Dh, Dh), F32)],
        compiler_params=_cparams("parallel", "arbitrary"),
        name="retention_prompt",
    )(q, k, v, sg, dmask.reshape(H // 2, 2 * C, C), qd, kd, sd, bd, gm, gn)


def _attn_kernel(q_ref, k_ref, v_ref, o_ref, oacc, lacc):
    L = q_ref.shape[1]
    P2 = q_ref.shape[2]
    lane = lax.broadcasted_iota(jnp.int32, (1, P2), 1)
    head0 = lane < HEAD_DIM

    for p, (window, d) in enumerate(DILATED_PATTERNS):
        band = window // d
        nb = L // (d * band)
        qi = lax.broadcasted_iota(jnp.int32, (2 * band, 2 * band), 0) % band
        ki = lax.broadcasted_iota(jnp.int32, (2 * band, 2 * band), 1)
        dist = qi + band - ki
        in_band = (dist >= 0) & (dist <= band)
        bias_any = jnp.where(in_band, 0.0, NEG_INF)
        bias_first = jnp.where(in_band & (ki >= band), 0.0, NEG_INF)

        def block(idx, d=d, band=band, nb=nb, p=p, bias_any=bias_any, bias_first=bias_first):
            r = idx // nb
            n = idx % nb
            qs = r + d * band * n
            ps = jnp.maximum(qs - d * band, r)
            cur = pl.ds(qs, band, stride=d) if d > 1 else pl.ds(qs, band)
            prev = pl.ds(ps, band, stride=d) if d > 1 else pl.ds(ps, band)
            qb = q_ref[0, cur, :]
            kk = jnp.concatenate([k_ref[0, prev, :], k_ref[0, cur, :]], axis=0).astype(BF16)
            vv = jnp.concatenate([v_ref[0, prev, :], v_ref[0, cur, :]], axis=0).astype(BF16)
            q2 = jnp.concatenate([jnp.where(head0, qb, 0.0), jnp.where(head0, 0.0, qb)], axis=0).astype(BF16)
            s = _dot_nt(q2, kk) + jnp.where(n > 0, bias_any, bias_first)
            m = jnp.max(jnp.maximum(s[:, :band], s[:, band:]), axis=-1, keepdims=True)
            e = jnp.exp(s - m)
            l = jnp.sum(e[:, :band] + e[:, band:], axis=-1, keepdims=True)
            o = _dot(e.astype(BF16), vv) * (1.0 / l)
            lse = m + jnp.log(l)
            oacc[p, cur, :] = jnp.where(head0, o[:band], o[band:])
            lacc[p, cur, :] = jnp.where(head0, lse[:band], lse[band:])

        def body(i, carry, block=block):
            for u in range(ATTN_UNROLL):
                block(i * ATTN_UNROLL + u)
            return carry

        lax.fori_loop(0, d * nb // ATTN_UNROLL, body, 0)

    rows = 512

    def combine(i, carry):
        sl = pl.ds(pl.multiple_of(i * rows, rows), rows)
        l0, l1, l2 = lacc[0, sl, :], lacc[1, sl, :], lacc[2, sl, :]
        m = jnp.maximum(jnp.maximum(l0, l1), l2)
        w0, w1, w2 = jnp.exp(l0 - m), jnp.exp(l1 - m), jnp.exp(l2 - m)
        tot = w0 + w1 + w2
        o_ref[0, sl, :] = (w0 * oacc[0, sl, :] + w1 * oacc[1, sl, :] + w2 * oacc[2, sl, :]) / tot
        return carry

    lax.fori_loop(0, L // rows, combine, 0)


def _attention_prompt(q, k, v):
    B, L, W = q.shape
    P2 = 2 * HEAD_DIM
    blk = pl.BlockSpec((1, L, P2), lambda b, hp: (b, 0, hp))
    return pl.pallas_call(
        _attn_kernel,
        grid=(B, W // P2),
        in_specs=[blk, blk, blk],
        out_specs=blk,
        out_shape=jax.ShapeDtypeStruct((B, L, W), F32),
        scratch_shapes=[pltpu.VMEM((len(DILATED_PATTERNS), L, P2), F32)] * 2,
        compiler_params=_cparams("parallel", "parallel"),
        name="attention_prompt",
    )(q, k, v)


def _pack_bf16_pair(a, b):
    ua = pltpu.bitcast(a.astype(BF16).astype(F32), jnp.uint32)
    ub = pltpu.bitcast(b.astype(BF16).astype(F32), jnp.uint32)
    return ua | (ub >> 16)


def _unpack_bf16_pair(p):
    a = pltpu.bitcast(p & jnp.uint32(0xFFFF0000), F32)
    b = pltpu.bitcast(p << 16, F32)
    return a, b


ROUTE_LANES = 128
LANES = 128


ROW_PIECES = 4


def _store_rows(ref, packed):
    n = packed.shape[0]
    for c in range(ROW_PIECES):
        ref[pl.ds(c, n, stride=ROW_PIECES), :] = packed[:, c * LANES:(c + 1) * LANES]


def _load_rows(ref):
    n = ref.shape[0] // ROW_PIECES
    return jnp.concatenate([ref[pl.ds(c, n, stride=ROW_PIECES), :] for c in range(ROW_PIECES)], axis=1)


def _token_rows(t):
    return pl.ds(pl.multiple_of(t * ROW_PIECES, ROW_PIECES), ROW_PIECES)


def _route_chunk(s, bias, carry):
    E, n = s.shape
    G, GS = N_EXPERT_GROUPS, E // N_EXPERT_GROUPS
    NEG = -jnp.inf
    choice = s + bias
    row = lax.broadcasted_iota(jnp.int32, (E, n), 0).astype(F32)
    lrow = row[:GS]
    gs_rows = []
    for g in range(G):
        c = choice[g * GS:(g + 1) * GS]
        m1 = jnp.max(c, axis=0, keepdims=True)
        i1 = jnp.min(jnp.where(c == m1, lrow, float(GS)), axis=0, keepdims=True)
        m2 = jnp.max(jnp.where(lrow == i1, NEG, c), axis=0, keepdims=True)
        gs_rows.append(m1 + m2)
    gs = jnp.concatenate(gs_rows, axis=0)
    grow = lax.broadcasted_iota(jnp.int32, (G, n), 0).astype(F32)
    gsel = jnp.zeros((G, n), F32)
    for _ in range(TOPK_GROUPS):
        gm = jnp.max(gs, axis=0, keepdims=True)
        gi = jnp.min(jnp.where(gs == gm, grow, float(G)), axis=0, keepdims=True)
        hit = grow == gi
        gsel = jnp.where(hit, 1.0, gsel)
        gs = jnp.where(hit, NEG, gs)
    emask = jnp.concatenate([jnp.broadcast_to(gsel[g:g + 1], (GS, n)) for g in range(G)], axis=0)
    masked = jnp.where(emask > 0.5, choice, NEG)
    ids, sks = [], []
    member = jnp.zeros((E, n), F32)
    for _ in range(TOP_K):
        mk = jnp.max(masked, axis=0, keepdims=True)
        ik = jnp.min(jnp.where(masked == mk, row, float(E)), axis=0, keepdims=True)
        sel = row == ik
        sks.append(jnp.sum(jnp.where(sel, s, 0.0), axis=0, keepdims=True))
        masked = jnp.where(sel, NEG, masked)
        member = jnp.where(sel, 1.0, member)
        ids.append(ik)
    eidx = jnp.concatenate(ids, axis=0)
    sk = jnp.concatenate(sks, axis=0)
    w = sk / jnp.sum(sk, axis=0, keepdims=True) * ROUTED_SCALE
    ti = lax.broadcasted_iota(jnp.int32, (n, n), 0)
    tj = lax.broadcasted_iota(jnp.int32, (n, n), 1)
    upper = jnp.where(ti <= tj, 1.0, 0.0).astype(BF16)
    incl = _dot(member.astype(BF16), upper)
    rank_dense = carry + incl - member
    ranks = [jnp.sum(jnp.where(row == ids[k], rank_dense, 0.0), axis=0, keepdims=True) for k in range(TOP_K)]
    return eidx, w, jnp.concatenate(ranks, axis=0), carry + incl[:, n - 1:n]


def _post_kernel(or_ref, oa_ref, x_ref, g1_ref, sc_ref, sh_ref, g2_ref, n2_ref, wo_ref, wrh_ref, wrl_ref, rb_ref,
                 cin_ref, wsg_ref, wsu_ref, wsd_ref, base_ref, h2_ref, eidx_ref, w_ref, rank_ref, cnt_ref):
    @pl.when(pl.program_id(0) == 0)
    def _():
        cnt_ref[...] = cin_ref[...]

    W = RET_WIDTH
    mix = _dot(or_ref[...].astype(BF16), wo_ref[:W, :]) + _dot(oa_ref[...].astype(BF16), wo_ref[W:, :])
    x2 = x_ref[...] + g1_ref[0] * mix
    ms = jnp.mean(x2 * x2, axis=-1, keepdims=True)
    h2 = x2 * lax.rsqrt(ms + NORM_EPS) * n2_ref[...]
    h2 = h2 * (1.0 + sc_ref[0]) + sh_ref[0]
    hb = h2.astype(BF16)
    hl = (h2 - hb.astype(F32)).astype(BF16)
    half = h2.shape[1] // 2
    _store_rows(h2_ref, _pack_bf16_pair(h2[:, :half], h2[:, half:]))
    g = _dot(hb, wsg_ref[...])
    u = _dot(hb, wsu_ref[...])
    shared = _dot((g * jax.nn.sigmoid(g) * u).astype(BF16), wsd_ref[...])
    base_ref[...] = x2 + g2_ref[0] * shared

    logits = _dot_nt(wrh_ref[...], hb) + (_dot_nt(wrh_ref[...], hl) + _dot_nt(wrl_ref[...], hb))
    s = jax.nn.sigmoid(logits)
    carry = cnt_ref[...]
    n = ROUTE_LANES
    for j in range(s.shape[1] // n):
        sl = slice(j * n, (j + 1) * n)
        eidx, w, rank, carry = _route_chunk(s[:, sl], rb_ref[...], carry)
        eidx_ref[:, sl] = eidx.astype(jnp.int32)
        w_ref[:, sl] = w
        rank_ref[:, sl] = rank.astype(jnp.int32)
    cnt_ref[...] = carry


def _post_mixer(o_r, o_a, x, g1, sc2, sh2, g2, norm2, wo, wrh_t, wrl_t, rbias, counts_in, wsg, wsu, wsd, tm,
                rows_per_mod):
    N, D = x.shape
    assert D // 2 == ROW_PIECES * LANES
    W = RET_WIDTH
    E = wrh_t.shape[0]
    R = g1.shape[1]
    mod_spec = pl.BlockSpec((1, R, D), lambda i: (i // rows_per_mod, 0, 0))
    const = lambda a: pl.BlockSpec(a.shape, lambda i: (0,) * a.ndim)
    tok = lambda rows: pl.BlockSpec((rows, tm), lambda i: (0, i))
    return pl.pallas_call(
        _post_kernel,
        grid=(N // tm,),
        in_specs=[
            pl.BlockSpec((tm, W), lambda i: (i, 0)),
            pl.BlockSpec((tm, W), lambda i: (i, 0)),
            pl.BlockSpec((tm, D), lambda i: (i, 0)),
            mod_spec, mod_spec, mod_spec, mod_spec,
            const(norm2), const(wo), const(wrh_t), const(wrl_t), const(rbias), const(counts_in),
            const(wsg), const(wsu), const(wsd),
        ],
        out_specs=[
            pl.BlockSpec((tm, D), lambda i: (i, 0)),
            pl.BlockSpec((tm * ROW_PIECES, LANES), lambda i: (i, 0)),
            tok(TOP_K), tok(TOP_K), tok(TOP_K),
            pl.BlockSpec((E, 1), lambda i: (0, 0)),
        ],
        out_shape=[
            jax.ShapeDtypeStruct((N, D), F32),
            jax.ShapeDtypeStruct((N * ROW_PIECES, LANES), jnp.uint32),
            jax.ShapeDtypeStruct((TOP_K, N), jnp.int32),
            jax.ShapeDtypeStruct((TOP_K, N), F32),
            jax.ShapeDtypeStruct((TOP_K, N), jnp.int32),
            jax.ShapeDtypeStruct((E, 1), F32),
        ],
        compiler_params=_cparams("arbitrary"),
        name="post_mixer",
    )(o_r, o_a, x, g1, sc2, sh2, g2, norm2, wo, wrh_t, wrl_t, rbias, counts_in, wsg, wsu, wsd)


def _expert_kernel(be_ref, first_ref, nused_ref, nxt_ref, slot_ref, x_hbm, wg_hbm, wu_hbm, wd_hbm, y_ref,
                   xbuf, wg_f, wu_f, wd_f, wg_s, wu_s, wd_s, xsem, sem):
    i = pl.program_id(0)
    is_first = first_ref[i] == 1
    n_used = nused_ref[0]
    used = i < n_used
    tile_rows = xbuf.shape[1]
    x_slots = xbuf.shape[0]

    def weight_copies(e, s):
        return (pltpu.make_async_copy(wg_hbm.at[e], wg_f.at[s], sem.at[s, 0]),
                pltpu.make_async_copy(wu_hbm.at[e], wu_f.at[s], sem.at[s, 1]),
                pltpu.make_async_copy(wd_hbm.at[e], wd_f.at[s], sem.at[s, 2]))

    def x_copy(tile):
        s = tile % x_slots
        rows = pl.ds(pl.multiple_of(tile * tile_rows, tile_rows), tile_rows)
        return pltpu.make_async_copy(x_hbm.at[rows], xbuf.at[s], xsem.at[s])

    @pl.when(i == 0)
    def _():
        for t in range(x_slots - 1):
            @pl.when(t < n_used)
            def _():
                x_copy(t).start()

    @pl.when(used)
    def _():
        x_copy(i).wait()

    @pl.when(i + (x_slots - 1) < n_used)
    def _():
        x_copy(i + (x_slots - 1)).start()

    def compute():
        xa, xb = _unpack_bf16_pair(_load_rows(xbuf.at[i % x_slots]))
        half = xa.shape[1]
        x = jnp.concatenate([xa.astype(BF16), xb.astype(BF16)], axis=1)
        g = _dot(x, wg_s[...])
        u = _dot(x, wu_s[...])
        hmid = (g * jax.nn.sigmoid(g) * u).astype(BF16)
        _store_rows(y_ref, _pack_bf16_pair(_dot(hmid, wd_s[:, :half]), _dot(hmid, wd_s[:, half:])))

    @pl.when(i == 0)
    def _():
        for c in weight_copies(be_ref[0], 0):
            c.start()

    @pl.when(is_first)
    def _():
        s = slot_ref[i]
        for c in weight_copies(be_ref[i], s):
            c.wait()

        @pl.when(nxt_ref[i] >= 0)
        def _():
            for c in weight_copies(nxt_ref[i], 1 - s):
                c.start()

        wg_s[...] = wg_f[s].astype(BF16)
        wu_s[...] = wu_f[s].astype(BF16)
        wd_s[...] = wd_f[s].astype(BF16)
        compute()

    @pl.when(jnp.logical_and(jnp.logical_not(is_first), used))
    def _():
        compute()

    @pl.when(jnp.logical_not(used))
    def _():
        y_ref[...] = jnp.zeros_like(y_ref)


def _expert_matmul(xs, plan, w_gate, w_up, w_down):
    P = xs.shape[0] // ROW_PIECES
    T = EXPERT_TILE
    D, F = w_gate.shape[1], w_gate.shape[2]
    row = pl.BlockSpec((T * ROW_PIECES, LANES), lambda i, *_: (i, 0))
    hbm = pl.BlockSpec(memory_space=pl.ANY)
    return pl.pallas_call(
        _expert_kernel,
        grid_spec=pltpu.PrefetchScalarGridSpec(
            num_scalar_prefetch=5,
            grid=(P // T,),
            in_specs=[hbm, hbm, hbm, hbm],
            out_specs=row,
            scratch_shapes=[
                pltpu.VMEM((EXPERT_ROW_SLOTS, T * ROW_PIECES, LANES), jnp.uint32),
                pltpu.VMEM((2, D, F), F32), pltpu.VMEM((2, D, F), F32), pltpu.VMEM((2, F, D), F32),
                pltpu.VMEM((D, F), BF16), pltpu.VMEM((D, F), BF16), pltpu.VMEM((F, D), BF16),
                pltpu.SemaphoreType.DMA((EXPERT_ROW_SLOTS,)),
                pltpu.SemaphoreType.DMA((2, 3)),
            ],
        ),
        out_shape=jax.ShapeDtypeStruct(xs.shape, jnp.uint32),
        compiler_params=_cparams("arbitrary"),
        name="routed_experts",
    )(*plan, xs, w_gate, w_up, w_down)


def _dest_kernel(eidx_ref, rank_ref, off_ref, dest_ref):
    E = off_ref.shape[0]
    n = eidx_ref.shape[1]
    row = lax.broadcasted_iota(jnp.int32, (E, n), 0)
    off = off_ref[...]
    rows = []
    for k in range(eidx_ref.shape[0]):
        hit = row == eidx_ref[k:k + 1, :]
        rows.append(jnp.sum(jnp.where(hit, off, 0.0), axis=0, keepdims=True))
    dest_ref[...] = rank_ref[...] + jnp.concatenate(rows, axis=0).astype(jnp.int32)


def _dest_rows(eidx, rank, pad_off):
    K, N = eidx.shape
    tn = min(N, DEST_TILE)
    blk = pl.BlockSpec((K, tn), lambda i: (0, i))
    return pl.pallas_call(
        _dest_kernel,
        grid=(N // tn,),
        in_specs=[blk, blk, pl.BlockSpec(pad_off.shape, lambda i: (0, 0))],
        out_specs=blk,
        out_shape=jax.ShapeDtypeStruct((K, N), jnp.int32),
        compiler_params=_cparams("parallel"),
        name="dest_rows",
    )(eidx, rank, pad_off)


def _row_copy(src, dst, sem):
    return pltpu.make_async_copy(src, dst, sem)


def _dispatch_kernel(dest_ref, h_ref, xs_in_ref, xs_ref, sem):
    del xs_in_ref
    K, tm = dest_ref.shape

    def issue(t, carry):
        for k in range(K):
            _row_copy(h_ref.at[_token_rows(t)], xs_ref.at[_token_rows(dest_ref[k, t])], sem).start(
                priority=k % DMA_QUEUES)
        return carry

    lax.fori_loop(0, tm, issue, 0, unroll=ISSUE_UNROLL)
    for _ in range(K):
        _row_copy(h_ref, xs_ref.at[pl.ds(0, tm * ROW_PIECES)], sem).wait()


def _dispatch(dest, h2p, xs, tm):
    K, N = dest.shape
    return pl.pallas_call(
        _dispatch_kernel,
        grid=(N // tm,),
        in_specs=[
            pl.BlockSpec((K, tm), lambda i: (0, i), memory_space=pltpu.SMEM),
            pl.BlockSpec((tm * ROW_PIECES, LANES), lambda i: (i, 0)),
            pl.BlockSpec(memory_space=pl.ANY),
        ],
        out_specs=pl.BlockSpec(memory_space=pl.ANY),
        out_shape=jax.ShapeDtypeStruct(xs.shape, xs.dtype),
        scratch_shapes=[pltpu.SemaphoreType.DMA],
        input_output_aliases={2: 0},
        compiler_params=_cparams("arbitrary"),
        name="dispatch",
    )(dest, h2p, xs)


def _gather_rows(src, idx):
    G, n = idx.shape
    nb = n // GATHER_WINDOW
    mesh = plsc.VectorSubcoreMesh(core_axis_name="core", subcore_axis_name="subcore")

    @pl.kernel(out_type=jax.ShapeDtypeStruct((G * n, src.shape[1]), src.dtype), mesh=mesh, scratch_types=[])
    def gather_kernel(src_hbm, idx_hbm, out_hbm):
        def body(idx_vmem, out_vmem):
            pltpu.sync_copy(src_hbm.at[idx_vmem.at[0]], out_vmem)

        pltpu.emit_pipeline(
            body,
            grid=(G, nb),
            in_specs=[pl.BlockSpec((1, GATHER_WINDOW), index_map=lambda g, j: (g, j))],
            out_specs=[pl.BlockSpec((GATHER_WINDOW, src.shape[1]), index_map=lambda g, j: (g * nb + j, 0))],
            core_axis_name=("core", "subcore"),
            dimension_semantics=(pltpu.PARALLEL, pltpu.PARALLEL),
        )(idx_hbm, out_hbm)

    return gather_kernel(src, idx)


def _zero_rows(n_rows):
    mesh = plsc.VectorSubcoreMesh(core_axis_name="core", subcore_axis_name="subcore")
    sc_lanes = 16

    @pl.kernel(out_type=jax.ShapeDtypeStruct((n_rows, LANES), jnp.uint32), mesh=mesh, scratch_types=[])
    def zero_kernel(unused_hbm, out_hbm):
        del unused_hbm

        def body(out_vmem):
            @pl.loop(0, out_vmem.shape[0])
            def _(r):
                @pl.loop(0, out_vmem.shape[1], step=sc_lanes)
                def _(c):
                    out_vmem.at[pl.ds(r, 1), pl.ds(c, sc_lanes)][...] = jnp.zeros((1, sc_lanes), jnp.uint32)

        pltpu.emit_pipeline(
            body,
            grid=(n_rows // GATHER_WINDOW,),
            in_specs=[],
            out_specs=[pl.BlockSpec((GATHER_WINDOW, LANES), index_map=lambda i: (i, 0))],
            core_axis_name=("core", "subcore"),
            dimension_semantics=(pltpu.PARALLEL,),
        )(out_hbm)

    return zero_kernel(jnp.zeros((8, LANES), jnp.uint32))


def _combine_kernel(base_ref, w_ref, g2_ref, rows_ref, o_ref):
    tm = base_ref.shape[0]
    half = ROW_PIECES * LANES
    w = w_ref[...]
    acc_a = jnp.zeros((tm, half), F32)
    acc_b = jnp.zeros((tm, half), F32)
    for k in range(TOP_K):
        packed = jnp.concatenate([rows_ref[c * TOP_K + k] for c in range(ROW_PIECES)], axis=1)
        a, b = _unpack_bf16_pair(packed)
        wk = w[:, k:k + 1]
        acc_a = acc_a + wk * a
        acc_b = acc_b + wk * b
    g2 = g2_ref[0]
    o_ref[:, :half] = base_ref[:, :half] + g2[:, :half] * acc_a
    o_ref[:, half:] = base_ref[:, half:] + g2[:, half:] * acc_b


def _combine(base, w_tok, g2, gathered, token_block, n_blocks, gather_block, tm, rows_per_mod):
    N, D = base.shape
    K = w_tok.shape[1]
    R = g2.shape[1]
    planes = gathered.shape[0]
    return pl.pallas_call(
        _combine_kernel,
        grid=(n_blocks,),
        in_specs=[
            pl.BlockSpec((tm, D), lambda i: (token_block + i, 0)),
            pl.BlockSpec((tm, K), lambda i: (token_block + i, 0)),
            pl.BlockSpec((1, R, D), lambda i: ((token_block + i) // rows_per_mod, 0, 0)),
            pl.BlockSpec((planes, tm, LANES), lambda i: (0, gather_block + i, 0)),
        ],
        out_specs=pl.BlockSpec((tm, D), lambda i: (token_block + i, 0)),
        out_shape=jax.ShapeDtypeStruct((N, D), F32),
        input_output_aliases={0: 0},
        compiler_params=_cparams("parallel"),
        name="combine",
    )(base, w_tok, g2, gathered)


def _tile_plan(counts, n_tiles):
    T = EXPERT_TILE
    E = counts.shape[0]
    padded = (counts + T - 1) // T * T
    pad_end = jnp.cumsum(padded)
    pad_off = pad_end - padded
    idx = jnp.arange(n_tiles, dtype=jnp.int32)
    tile_start = idx * T
    blk_exp = jnp.minimum(jnp.sum((pad_end[None, :] <= tile_start[:, None]).astype(jnp.int32), axis=1), E - 1)
    used = tile_start < pad_end[-1]
    changed = jnp.concatenate([jnp.ones((1,), bool), blk_exp[1:] != blk_exp[:-1]])
    first = jnp.logical_and(used, changed)
    next_first = lax.cummin(jnp.where(first, idx, n_tiles), reverse=True)
    next_first = jnp.concatenate([next_first[1:], jnp.full((1,), n_tiles, jnp.int32)])
    nxt = jnp.where(next_first < n_tiles, blk_exp[jnp.minimum(next_first, n_tiles - 1)], -1)
    slot = (jnp.cumsum(first.astype(jnp.int32)) - 1) % 2
    n_used = (pad_end[-1] // T).reshape(1)
    as_i32 = lambda a: a.astype(jnp.int32)
    return pad_off, (as_i32(blk_exp), as_i32(first), as_i32(n_used), as_i32(nxt), as_i32(slot))


def _ret_sample_kernel(q_ref, k_ref, v_ref, sg_ref, gn_ref, dec_ref, s_ref, o_ref, so_ref):
    q = q_ref[0]
    k = k_ref[0]
    v = v_ref[0]
    gamma = dec_ref[pl.program_id(0)]
    qk = jnp.sum(q * k, axis=0, keepdims=True)
    o = qk * v
    Dh = q.shape[0]
    acc = jnp.zeros_like(v)
    for d in range(Dh):
        S = s_ref[0, d]
        acc = acc + q[d:d + 1, :] * S
        so_ref[0, d] = S * gamma + k[d:d + 1, :] * v
    o = o + acc * gamma
    mu = jnp.mean(o, axis=0, keepdims=True)
    oc = o - mu
    var = jnp.mean(oc * oc, axis=0, keepdims=True)
    o_ref[0] = oc * lax.rsqrt(var + GN_EPS) * gn_ref[0] * sg_ref[0]


def _retention_sample(qT, kT, vT, sgT, gn_col, gamma, S):
    H, Dh, Bt = qT.shape
    vec = pl.BlockSpec((1, Dh, Bt), lambda h: (h, 0, 0))
    st = pl.BlockSpec((1, Dh, Dh, Bt), lambda h: (h, 0, 0, 0))
    return pl.pallas_call(
        _ret_sample_kernel,
        grid=(H,),
        in_specs=[vec, vec, vec, vec, pl.BlockSpec((1, Dh, 1), lambda h: (h, 0, 0)),
                  pl.BlockSpec(memory_space=pltpu.SMEM), st],
        out_specs=[vec, st],
        out_shape=[jax.ShapeDtypeStruct((H, Dh, Bt), F32), jax.ShapeDtypeStruct((H, Dh, Dh, Bt), F32)],
        compiler_params=_cparams("parallel"),
        name="retention_sample",
    )(qT, kT, vT, sgT, gn_col, gamma, S)


def _attn_sample_kernel(q_ref, kn_ref, vn_ref, knT_ref, vnT_ref, kb_ref, vb_ref, o_ref, ko_ref, vo_ref):
    b = pl.program_id(0)
    H, Dh, Wn = kb_ref.shape[1], kb_ref.shape[2], kb_ref.shape[3]
    HD = H * Dh
    q = q_ref[0]
    kn = kn_ref[0]
    vn = vn_ref[0]
    row = lax.broadcasted_iota(jnp.int32, (H, HD), 0)
    col = lax.broadcasted_iota(jnp.int32, (H, HD), 1)
    own = (col // Dh) == row
    q_bd = jnp.where(own, q, 0.0)
    s_new = jnp.sum(q_bd * kn, axis=1, keepdims=True)

    bl = lax.broadcasted_iota(jnp.int32, knT_ref.shape, 1)
    k_col = jnp.sum(jnp.where(bl == b, knT_ref[...], 0.0), axis=1, keepdims=True)
    v_col = jnp.sum(jnp.where(bl == b, vnT_ref[...], 0.0), axis=1, keepdims=True)

    t = lax.broadcasted_iota(jnp.int32, (1, Wn), 1)
    last = t == (Wn - 1)
    prow = lax.broadcasted_iota(jnp.int32, (8, Wn), 0)
    q_bf = q_bd.astype(BF16)
    outs = []
    for h in range(H):
        K = kb_ref[0, h]
        V = vb_ref[0, h]
        hs = slice(h * Dh, (h + 1) * Dh)
        ko_ref[0, h] = jnp.where(last, k_col[hs], pltpu.roll(K, Wn - 1, 1))
        vo_ref[0, h] = jnp.where(last, v_col[hs], pltpu.roll(V, Wn - 1, 1))
        s = _dot(q_bf[:, hs], K.astype(BF16))[h:h + 1]
        sn = s_new[h:h + 1]
        Vb = V.astype(BF16)
        vn_h = vn[:, hs]
        lse_p, self_p = [], []
        probs = jnp.zeros((8, Wn), F32)
        for p, (window, d) in enumerate(DILATED_PATTERNS):
            valid = (((Wn - t) % d) == 0) & (t >= Wn - window)
            sm = jnp.where(valid, s, NEG_INF)
            m = jnp.maximum(jnp.max(sm, axis=1, keepdims=True), sn)
            e = jnp.exp(sm - m)
            en = jnp.exp(sn - m)
            l = jnp.sum(e, axis=1, keepdims=True) + en
            probs = jnp.where(prow == p, e / l, probs)
            self_p.append(en / l)
            lse_p.append(m + jnp.log(l))
        pv = _dot_nt(probs.astype(BF16), Vb)
        o_p = [pv[p:p + 1] + self_p[p] * vn_h for p in range(len(DILATED_PATTERNS))]
        mm = jnp.maximum(jnp.maximum(lse_p[0], lse_p[1]), lse_p[2])
        ws = [jnp.exp(x - mm) for x in lse_p]
        tot = ws[0] + ws[1] + ws[2]
        outs.append((ws[0] * o_p[0] + ws[1] * o_p[1] + ws[2] * o_p[2]) / tot)
    o_ref[0] = jnp.concatenate(outs, axis=1)


def _attention_sample(q, kn, vn, knT, vnT, k_buf, v_buf):
    Bt, H, Dh, Wn = k_buf.shape
    HD = H * Dh
    vec = pl.BlockSpec((1, 1, HD), lambda b: (b, 0, 0))
    tr = pl.BlockSpec((HD, Bt), lambda b: (0, 0))
    cache = pl.BlockSpec((1, H, Dh, Wn), lambda b: (b, 0, 0, 0))
    return pl.pallas_call(
        _attn_sample_kernel,
        grid=(Bt,),
        in_specs=[vec, vec, vec, tr, tr, cache, cache],
        out_specs=[vec, cache, cache],
        out_shape=[jax.ShapeDtypeStruct((Bt, 1, HD), F32),
                   jax.ShapeDtypeStruct(k_buf.shape, F32), jax.ShapeDtypeStruct(v_buf.shape, F32)],
        compiler_params=_cparams("parallel"),
        name="attention_sample",
    )(q, kn, vn, knT, vnT, k_buf, v_buf)


def _rope_tables(pos):
    half = HEAD_DIM // 2
    inv = ROPE_THETA ** (-jnp.arange(half, dtype=F32) / half)
    ang = pos.astype(F32)[:, None] * inv[None, :]
    cos, sin = jnp.cos(ang), jnp.sin(ang)
    cos_t = jnp.tile(cos, (1, 2 * N_RET_HEADS))
    sin_t = jnp.tile(jnp.concatenate([-sin, sin], axis=1), (1, N_RET_HEADS))
    return cos_t, sin_t


def _layer(x_prompt, x_sample, c_prompt, c_sample, state_ret, cache_k, cache_v,
           norm1, w_ada, b_ada, w_in, gn_ret, q_norm, k_norm, w_out, norm2,
           w_router, router_bias, w_exp_gate, w_exp_up, w_exp_down, w_sh_gate, w_sh_up, w_sh_down):
    B, L, D = x_prompt.shape
    Bs = x_sample.shape[0]
    H, Dh, W = N_RET_HEADS, HEAD_DIM, RET_WIDTH
    Np = B * L

    pad = (-(B + Bs)) % 8
    c_all = jnp.concatenate([c_prompt, c_sample, jnp.zeros((pad, D), F32)], axis=0)
    mod = _modulation(c_all, w_ada, b_ada)
    mod_p = [m.reshape(B, 1, D) for m in jnp.split(mod[:B], 6, axis=-1)]
    mod_s = [m.reshape(1, Bs, D) for m in jnp.split(mod[B:B + Bs], 6, axis=-1)]

    w_in_bf = w_in.astype(BF16)
    gi = jnp.arange(W) // Dh
    gmat = jnp.where(gi[:, None] == gi[None, :], 1.0 / Dh, 0.0).astype(BF16)
    qn_t = jnp.tile(q_norm, N_ATT_HEADS).reshape(1, W)
    kn_t = jnp.tile(k_norm, N_ATT_HEADS).reshape(1, W)
    n1 = norm1.reshape(1, D)
    n2 = norm2.reshape(1, D)
    gn = gn_ret.reshape(1, W)

    tm = TOKEN_TILE
    cos_p, sin_p = _rope_tables(jnp.arange(L, dtype=jnp.int32))
    cos_s, sin_s = _rope_tables(PAST_LEN + jnp.arange(1, dtype=jnp.int32))

    xp = x_prompt.reshape(Np, D)
    xs = x_sample.reshape(Bs, D)
    proj_p = _in_proj(xp, mod_p[1], mod_p[0], cos_p, sin_p, n1, qn_t, kn_t, gmat, w_in_bf, tm, L // tm, L // tm)
    proj_s = _in_proj(xs, mod_s[1], mod_s[0], cos_s, sin_s, n1, qn_t, kn_t, gmat, w_in_bf, Bs, 1, 1)
    qr, kr, vr, sg, qa, ka, va = [t.reshape(B, L, W) for t in proj_p]
    qr_s, kr_s, vr_s, sg_s, qa_s, ka_s, va_s = proj_s

    o_r, state_p = _retention_prompt(qr, kr, vr, sg, gn, gmat)
    o_a = _attention_prompt(qa, ka, va)
    keep = min(MAX_WINDOW, L)
    cache_kp = ka[:, L - keep:].reshape(B, keep, N_ATT_HEADS, Dh)
    cache_vp = va[:, L - keep:].reshape(B, keep, N_ATT_HEADS, Dh)

    log_g = jnp.log1p(-(2.0 ** (-5.0 - jnp.arange(H, dtype=F32))))
    gamma = jnp.exp(log_g).astype(F32)
    tr = lambda t: t.T.reshape(H, Dh, Bs)
    S_t = jnp.transpose(state_ret, (1, 2, 3, 0))
    orT, S_new_t = _retention_sample(tr(qr_s), tr(kr_s), tr(vr_s), tr(sg_s), gn_ret.reshape(H, Dh, 1), gamma, S_t)
    o_r_s = orT.reshape(W, Bs).T
    state_s = jnp.transpose(S_new_t, (3, 0, 1, 2))

    kb_t = jnp.transpose(cache_k, (0, 2, 3, 1))
    vb_t = jnp.transpose(cache_v, (0, 2, 3, 1))
    o_a_s, ko_t, vo_t = _attention_sample(qa_s.reshape(Bs, 1, W), ka_s.reshape(Bs, 1, W), va_s.reshape(Bs, 1, W),
                                          ka_s.T, va_s.T, kb_t, vb_t)
    cache_ks = jnp.transpose(ko_t, (0, 3, 1, 2))
    cache_vs = jnp.transpose(vo_t, (0, 3, 1, 2))

    wo = w_out.astype(BF16)
    wr_t = w_router.T
    wrh = wr_t.astype(BF16)
    wrl = (wr_t - wrh.astype(F32)).astype(BF16)
    rbias = router_bias.astype(F32).reshape(N_EXPERTS, 1)
    wsg, wsu, wsd = w_sh_gate.astype(BF16), w_sh_up.astype(BF16), w_sh_down.astype(BF16)
    zero_counts = jnp.zeros((N_EXPERTS, 1), F32)
    base_p, h2_p, eidx_p, w_p, rank_p, counts_p = _post_mixer(
        o_r.reshape(Np, W), o_a.reshape(Np, W), xp, mod_p[2], mod_p[4], mod_p[3], mod_p[5], n2, wo, wrh, wrl, rbias,
        zero_counts, wsg, wsu, wsd, tm, L // tm)
    base_s, h2_s, eidx_s, w_s, rank_s, counts = _post_mixer(
        o_r_s, o_a_s.reshape(Bs, W), xs, mod_s[2], mod_s[4], mod_s[3], mod_s[5], n2, wo, wrh, wrl, rbias,
        counts_p, wsg, wsu, wsd, Bs, 1)

    T = EXPERT_TILE
    n_tiles = ((Np + Bs) * TOP_K + N_EXPERTS * T + T - 1) // T
    pad_off, plan = _tile_plan(counts.reshape(N_EXPERTS).astype(jnp.int32), n_tiles)
    pad_off = pad_off.astype(F32).reshape(N_EXPERTS, 1)
    dest_p = _dest_rows(eidx_p, rank_p, pad_off)
    dest_s = _dest_rows(eidx_s, rank_s, pad_off)
    grouped = _zero_rows(n_tiles * T * ROW_PIECES)
    grouped = _dispatch(dest_p, h2_p, grouped, tm)
    grouped = _dispatch(dest_s, h2_s, grouped, Bs)
    ys = _expert_matmul(grouped, plan, w_exp_gate, w_exp_up, w_exp_down)
    tc = COMBINE_TILE
    chunk = Np // GATHER_CHUNKS
    first_rows_p = dest_p * ROW_PIECES
    first_rows_s = dest_s * ROW_PIECES
    wt_p = w_p.T
    y_p = base_p
    for j in range(GATHER_CHUNKS):
        cols = first_rows_p[:, j * chunk:(j + 1) * chunk]
        last = j == GATHER_CHUNKS - 1
        if last:
            cols = jnp.concatenate([cols, first_rows_s], axis=1)
        row_idx = jnp.concatenate([cols + c for c in range(ROW_PIECES)], axis=0)
        gathered = _gather_rows(ys, row_idx).reshape(row_idx.shape + (LANES,))
        y_p = _combine(y_p, wt_p, mod_p[5], gathered, j * chunk // tc, chunk // tc, 0, tc, L // tc)
        if last:
            y_s = _combine(base_s, w_s.T, mod_s[5], gathered, 0, 1, chunk // Bs, Bs, 1)

    return (y_p.reshape(B, L, D), y_s.reshape(Bs, 1, D), state_p, cache_kp, cache_vp, state_s, cache_ks, cache_vs)


def kernel(x_prompt, x_sample, c_prompt, c_sample, state_ret, cache_win_k, cache_win_v, norm1, w_ada, b_ada, w_in,
           gn_ret, q_norm, k_norm, w_out, norm2, w_router, router_bias, w_exp_gate, w_exp_up, w_exp_down,
           w_sh_gate, w_sh_up, w_sh_down):
    assert w_in.shape[0] == 1, "single-layer step"
    assert x_sample.shape[1] == 1, "one new token per sample sequence"
    outs = _layer(x_prompt, x_sample, c_prompt, c_sample, state_ret[0], cache_win_k[0], cache_win_v[0],
                  norm1[0], w_ada[0], b_ada[0], w_in[0], gn_ret[0], q_norm[0], k_norm[0], w_out[0], norm2[0],
                  w_router[0], router_bias[0], w_exp_gate[0], w_exp_up[0], w_exp_down[0],
                  w_sh_gate[0], w_sh_up[0], w_sh_down[0])
    yp, ys, sp, kp, vp, ss, ksm, vsm = outs
    return (yp, ys, sp[None], kp[None], vp[None], ss[None], ksm[None], vsm[None])
```
